```python
import jax, jax.numpy as jnp
from jax import lax
import numpy as np

D_MODEL = 2048
BATCH = 4
SEQ = 2048
DEPTH = 2
DEC_BATCH = 128
DEC_SEQ = 8
PAST_LEN = 16384
PAGE_SIZE = 128

N_MIXERS = 2
N_RET = (DEPTH + 1) // 2
N_POOL = DEPTH // 2
RET_HEADS = 8
RET_DK = D_MODEL // RET_HEADS
RET_DV = 2 * D_MODEL // RET_HEADS
RET_CHUNK = 128
ROPE_BASE = 10000.0
POOL_WINDOWS = (2, 4, 8, 16)
POOL_GROUPS = len(POOL_WINDOWS)
POOL_GC = D_MODEL // POOL_GROUPS
POOL_HIST = max(POOL_WINDOWS) - 1
D_FF = -(-8 * D_MODEL // 768) * 256
N_ADA = 6
EPS = 1e-6

kernel_name = "retention_pool_hybrid_step"


def rmsnorm(x, g):
    xf = x.astype(jnp.float32)
    xf = xf * lax.rsqrt(jnp.mean(xf * xf, axis=-1, keepdims=True) + EPS)
    return (xf * g.astype(jnp.float32)).astype(x.dtype)


def rope(x, pos):
    half = x.shape[-1] // 2
    inv = 1.0 / (ROPE_BASE ** (jnp.arange(half, dtype=jnp.float32) / half))
    ang = pos[:, None] * inv[None, :]
    cos = jnp.cos(ang)[None, :, None, :]
    sin = jnp.sin(ang)[None, :, None, :]
    xf = x.astype(jnp.float32)
    x1, x2 = xf[..., :half], xf[..., half:]
    return jnp.concatenate([x1 * cos - x2 * sin, x1 * sin + x2 * cos], axis=-1)


def retention_decays(C):
    lg = jnp.log1p(-jnp.exp2(-5.0 - jnp.arange(RET_HEADS, dtype=jnp.float32)))
    idx = jnp.arange(C, dtype=jnp.float32)
    diff = idx[:, None] - idx[None, :]
    causal = diff >= 0
    dmask = jnp.where(causal[None], jnp.exp(jnp.where(causal, diff, 0.0)[None] * lg[:, None, None]), 0.0)
    qdec = jnp.exp((idx + 1.0)[None, :] * lg[:, None])
    kdec = jnp.exp((C - 1.0 - idx)[None, :] * lg[:, None])
    sdec = jnp.exp(C * lg)
    return dmask, qdec, kdec, sdec


def retention(h, pos0, S0, w_in, gn_g, w_out, chunk):
    B, T, _ = h.shape
    proj = h @ w_in
    q, k, v, g = jnp.split(proj, [RET_HEADS * RET_DK, 2 * RET_HEADS * RET_DK,
                                  2 * RET_HEADS * RET_DK + RET_HEADS * RET_DV], axis=-1)
    pos = pos0 + jnp.arange(T, dtype=jnp.float32)
    q = rope(q.reshape(B, T, RET_HEADS, RET_DK), pos)
    k = rope(k.reshape(B, T, RET_HEADS, RET_DK), pos) * (RET_DK ** -0.5)
    v = v.reshape(B, T, RET_HEADS, RET_DV).astype(jnp.float32)
    nC = T // chunk

    def to_chunks(a):
        return a.reshape(B, nC, chunk, RET_HEADS, a.shape[-1]).transpose(1, 0, 3, 2, 4)

    dmask, qdec, kdec, sdec = retention_decays(chunk)

    def step(S, xs):
        qc, kc, vc = xs
        scores = jnp.einsum('bhnd,bhmd->bhnm', qc, kc) * dmask[None]
        o = (jnp.einsum('bhnm,bhme->bhne', scores, vc)
             + jnp.einsum('bhnd,bhde->bhne', qc * qdec[None, :, :, None], S))
        S = sdec[None, :, None, None] * S + jnp.einsum('bhmd,bhme->bhde', kc * kdec[None, :, :, None], vc)
        return S, o

    S, o = lax.scan(step, S0.astype(jnp.float32), (to_chunks(q), to_chunks(k), to_chunks(v)))
    o = o.transpose(1, 0, 3, 2, 4).reshape(B, T, RET_HEADS, RET_DV)
    mu = jnp.mean(o, axis=-1, keepdims=True)
    var = jnp.mean(jnp.square(o - mu), axis=-1, keepdims=True)
    o = ((o - mu) * lax.rsqrt(var + EPS)).reshape(B, T, RET_HEADS * RET_DV) * gn_g.astype(jnp.float32)
    out = (jax.nn.silu(g.astype(jnp.float32)) * o).astype(h.dtype) @ w_out
    return out, S.astype(h.dtype)


def pool_mixer(u, hist, hist_len, w, scale):
    B, T, D = u.shape
    P = hist.shape[1]
    full = jnp.concatenate([hist, u], axis=1).astype(jnp.float32)
    cs = jnp.concatenate([jnp.zeros((B, 1, D), jnp.float32), jnp.cumsum(full, axis=1)], axis=1)
    t = jnp.arange(T)
    means = []
    for gi, win in enumerate(POOL_WINDOWS):
        sl = slice(gi * POOL_GC, (gi + 1) * POOL_GC)
        s = cs[:, P + 1:P + 1 + T, sl] - cs[:, P + 1 - win:P + 1 - win + T, sl]
        cnt = jnp.minimum(win, t + 1 + hist_len).astype(jnp.float32)
        means.append(s / cnt[None, :, None])
    m = jnp.concatenate(means, axis=-1) - full[:, P:]
    m = m.reshape(B, T, POOL_GROUPS, POOL_GC).astype(u.dtype)
    y = jnp.einsum('btgc,gce->btge', m, w).reshape(B, T, D) * scale
    new_hist = full[:, -P:].astype(u.dtype)
    return y, new_hist


def swiglu(h, w_in, w_out):
    gate, up = jnp.split(h @ w_in, 2, axis=-1)
    return (jax.nn.silu(gate) * up) @ w_out


def trunk(x, c, pos0, ret_state, pool_state, pool_hist_len, chunk,
          norm_mix_g, norm_ffn_g, ada_w, ada_b, ret_w_in, ret_gn_g, ret_w_out,
          pool_w, pool_scale, ffn_w_in, ffn_w_out, final_norm_g):
    new_ret, new_pool = [], []
    cs = jax.nn.silu(c)
    for i in range(DEPTH):
        mod = cs @ ada_w[i] + ada_b[i]
        sh1, sc1, g1, sh2, sc2, g2 = jnp.split(mod, N_ADA, axis=-1)
        h = rmsnorm(x, norm_mix_g[i]) * (1.0 + sc1[:, None]) + sh1[:, None]
        j = i // N_MIXERS
        if i % N_MIXERS == 0:
            out, s = retention(h, pos0, ret_state[j], ret_w_in[j], ret_gn_g[j], ret_w_out[j], chunk)
            new_ret.append(s)
        else:
            out, s = pool_mixer(h, pool_state[j], pool_hist_len, pool_w[j], pool_scale[j])
            new_pool.append(s)
        x = x + g1[:, None] * out
        h = rmsnorm(x, norm_ffn_g[i]) * (1.0 + sc2[:, None]) + sh2[:, None]
        x = x + g2[:, None] * swiglu(h, ffn_w_in[i], ffn_w_out[i])
    return rmsnorm(x, final_norm_g), jnp.stack(new_ret), jnp.stack(new_pool)


def setup_inputs(seed: int = 0) -> dict:
    key = jax.random.key(seed)
    ks = jax.random.split(key, 20)
    f32 = jnp.float32
    nrm = lambda k, s, sc: jax.random.normal(k, s, f32) * sc
    HQK = RET_HEADS * RET_DK
    HV = RET_HEADS * RET_DV
    return {
        "x_prompt": nrm(ks[0], (BATCH, SEQ, D_MODEL), 1.0),
        "x_sample": nrm(ks[1], (DEC_BATCH, DEC_SEQ, D_MODEL), 1.0),
        "c_prompt": nrm(ks[2], (BATCH, D_MODEL), 1.0),
        "c_sample": nrm(ks[3], (DEC_BATCH, D_MODEL), 1.0),
        "state_ret": nrm(ks[4], (N_RET, DEC_BATCH, RET_HEADS, RET_DK, RET_DV), 0.1),
        "state_pool": nrm(ks[5], (N_POOL, DEC_BATCH, POOL_HIST, D_MODEL), 1.0),
        "norm_mix_g": 1.0 + nrm(ks[6], (DEPTH, D_MODEL), 0.02),
        "norm_ffn_g": 1.0 + nrm(ks[7], (DEPTH, D_MODEL), 0.02),
        "ada_w": nrm(ks[8], (DEPTH, D_MODEL, N_ADA * D_MODEL), 0.3 * D_MODEL ** -0.5),
        "ada_b": nrm(ks[9], (DEPTH, N_ADA * D_MODEL), 0.02),
        "ret_w_in": nrm(ks[10], (N_RET, D_MODEL, 2 * HQK + 2 * HV), D_MODEL ** -0.5),
        "ret_gn_g": 1.0 + nrm(ks[11], (N_RET, HV), 0.02),
        "ret_w_out": nrm(ks[12], (N_RET, HV, D_MODEL), HV ** -0.5),
        "pool_w": nrm(ks[13], (N_POOL, POOL_GROUPS, POOL_GC, POOL_GC), POOL_GC ** -0.5),
        "pool_scale": 1.0 + nrm(ks[14], (N_POOL, D_MODEL), 0.1),
        "ffn_w_in": nrm(ks[15], (DEPTH, D_MODEL, 2 * D_FF), D_MODEL ** -0.5),
        "ffn_w_out": nrm(ks[16], (DEPTH, D_FF, D_MODEL), D_FF ** -0.5),
        "final_norm_g": 1.0 + nrm(ks[17], (D_MODEL,), 0.02),
    }


def reference(x_prompt, x_sample, c_prompt, c_sample, state_ret, state_pool,
              norm_mix_g, norm_ffn_g, ada_w, ada_b, ret_w_in, ret_gn_g, ret_w_out,
              pool_w, pool_scale, ffn_w_in, ffn_w_out, final_norm_g):
    B, T_p, _ = x_prompt.shape
    T_s = x_sample.shape[1]
    ret0 = jnp.zeros((N_RET, B, RET_HEADS, RET_DK, RET_DV), x_prompt.dtype)
    pool0 = jnp.zeros((N_POOL, B, POOL_HIST, D_MODEL), x_prompt.dtype)
    chunk_p = RET_CHUNK if T_p % RET_CHUNK == 0 else T_p
    y_prompt, state_ret_prompt, state_pool_prompt = trunk(
        x_prompt, c_prompt, 0.0, ret0, pool0, 0, chunk_p,
        norm_mix_g, norm_ffn_g, ada_w, ada_b, ret_w_in, ret_gn_g, ret_w_out,
        pool_w, pool_scale, ffn_w_in, ffn_w_out, final_norm_g)
    y_sample, state_ret_sample, state_pool_sample = trunk(
        x_sample, c_sample, float(PAST_LEN), state_ret, state_pool, min(PAST_LEN, POOL_HIST), T_s,
        norm_mix_g, norm_ffn_g, ada_w, ada_b, ret_w_in, ret_gn_g, ret_w_out,
        pool_w, pool_scale, ffn_w_in, ffn_w_out, final_norm_g)
    return (y_prompt, y_sample, state_ret_prompt, state_pool_prompt, state_ret_sample, state_pool_sample)
```

```python
import functools

import jax
import jax.numpy as jnp
from jax import lax
from jax.experimental import pallas as pl
from jax.experimental.pallas import tpu as pltpu

F32 = jnp.float32
BF16 = jnp.bfloat16

RET_HEADS = 8
RET_CHUNK = 128
ROPE_BASE = 10000.0
POOL_WINDOWS = (2, 4, 8, 16)
N_ADA = 6
EPS = 1e-6
PAST_LEN = 16384

V7X_VMEM_BYTES = 64 * 1024 * 1024
V7X_LANES = 128
V7X_SUBLANES = 8
NORM_ROWS = 16
C_PAD = 8


def _vmem_limit(block_bytes, scratch_bytes=0):
    need = 2 * block_bytes + scratch_bytes + 12 * 1024 * 1024
    return int(min(need, V7X_VMEM_BYTES - 6 * 1024 * 1024))


def _nbytes(shape, dtype):
    n = 1
    for s in shape:
        n *= s
    return n * jnp.dtype(dtype).itemsize


def _silu(x):
    return x / (1.0 + jnp.exp(-x))


def _ada_kernel(c_ref, w_ref, b_ref, o_ref, *, n_prompt, sub):
    cs = _silu(c_ref[...])
    res = jnp.dot(cs.astype(BF16), w_ref[0].astype(BF16), preferred_element_type=F32) + b_ref[0]
    tn = res.shape[-1]
    for b in range(n_prompt):
        o_ref[0, b * sub:(b + 1) * sub, :] = jnp.broadcast_to(res[b:b + 1, :], (sub, tn))
    o_ref[0, n_prompt * sub:, :] = res[C_PAD:C_PAD + sub, :]


def _ada_call(c_all, ada_w, ada_b, n_prompt, sub, tn=1024):
    depth, d, n6 = ada_w.shape
    rows = c_all.shape[0]
    out_rows = (n_prompt + 1) * sub
    blocks = (_nbytes((rows, d), F32) + _nbytes((d, tn), F32) + _nbytes((1, tn), F32)
              + _nbytes((out_rows, tn), F32))
    return pl.pallas_call(
        functools.partial(_ada_kernel, n_prompt=n_prompt, sub=sub),
        grid=(depth, n6 // tn),
        in_specs=[
            pl.BlockSpec((rows, d), lambda l, j: (0, 0)),
            pl.BlockSpec((1, d, tn), lambda l, j: (l, 0, j)),
            pl.BlockSpec((1, 1, tn), lambda l, j: (l, 0, j)),
        ],
        out_specs=pl.BlockSpec((1, out_rows, tn), lambda l, j: (l, 0, j)),
        out_shape=jax.ShapeDtypeStruct((depth, out_rows, n6), F32),
        compiler_params=pltpu.CompilerParams(
            dimension_semantics=("arbitrary", "arbitrary"),
            vmem_limit_bytes=_vmem_limit(blocks, _nbytes((d, tn), BF16))),
        name="ada_mod",
    )(c_all, ada_w, ada_b.reshape(depth, 1, n6))


def _norm_mod_rows(x_ref, gam_ref, sc_ref, sh_ref, out_ref, *, tm, sub):
    gam = gam_ref[...]
    per_sub = sub // NORM_ROWS

    def body(c, carry):
        r0 = pl.multiple_of(c * NORM_ROWS, NORM_ROWS)
        x = x_ref[pl.ds(r0, NORM_ROWS), :]
        hn = (x * lax.rsqrt(jnp.mean(x * x, axis=-1, keepdims=True) + EPS)) * gam
        if sc_ref is not None:
            m0 = pl.multiple_of((c % per_sub) * NORM_ROWS, NORM_ROWS)
            hn = hn * (1.0 + sc_ref[pl.ds(m0, NORM_ROWS), :]) + sh_ref[pl.ds(m0, NORM_ROWS), :]
        out_ref[pl.ds(r0, NORM_ROWS), :] = hn.astype(out_ref.dtype)
        return carry

    lax.fori_loop(0, tm // NORM_ROWS, body, 0)


def _mod_index(i, n_prompt_tiles, tiles_per_seq, n_prompt):
    return jnp.where(i < n_prompt_tiles, i // tiles_per_seq, n_prompt)


def _norm_kernel(x_ref, gam_ref, sc_ref, sh_ref, o_ref, *, tm, sub):
    _norm_mod_rows(x_ref, gam_ref, sc_ref, sh_ref, o_ref, tm=tm, sub=sub)


def _final_norm_kernel(x_ref, gam_ref, o_ref, *, tm, sub):
    _norm_mod_rows(x_ref, gam_ref, None, None, o_ref, tm=tm, sub=sub)


def _norm_call(x, gam, mod, col_sc, col_sh, lay, tm=512):
    n, d = x.shape
    sub = lay["sub"]
    tps = lay["t_prompt"] // tm
    npt = lay["n_prompt_rows"] // tm
    midx = functools.partial(_mod_index, n_prompt_tiles=npt, tiles_per_seq=tps, n_prompt=lay["n_prompt"])
    in_specs = [pl.BlockSpec((tm, d), lambda i: (i, 0)), pl.BlockSpec((1, d), lambda i: (0, 0))]
    args = [x, gam.reshape(1, d)]
    if mod is not None:
        in_specs += [pl.BlockSpec((sub, d), lambda i: (midx(i), col_sc)),
                     pl.BlockSpec((sub, d), lambda i: (midx(i), col_sh))]
        args += [mod, mod]
        body = functools.partial(_norm_kernel, tm=tm, sub=sub)
    else:
        body = functools.partial(_final_norm_kernel, tm=tm, sub=sub)
    blocks = 2 * _nbytes((tm, d), F32) + 2 * _nbytes((sub, d), F32)
    return pl.pallas_call(
        body,
        grid=(n // tm,),
        in_specs=in_specs,
        out_specs=pl.BlockSpec((tm, d), lambda i: (i, 0)),
        out_shape=jax.ShapeDtypeStruct((n, d), F32),
        compiler_params=pltpu.CompilerParams(
            dimension_semantics=("arbitrary",), vmem_limit_bytes=_vmem_limit(blocks)),
        name="row_norm",
    )(*args)


def _ret_proj_kernel(x_ref, gam_ref, sc_ref, sh_ref, w_ref, cos_ref, sin_ref, qkv_ref, g_ref, h_ref,
                     *, tm, tn, sub, d_qk, d_v, head_dk, k_scale):
    j = pl.program_id(1)

    @pl.when(j == 0)
    def _():
        _norm_mod_rows(x_ref, gam_ref, sc_ref, sh_ref, h_ref, tm=tm, sub=sub)

    acc = jnp.dot(h_ref[...], w_ref[...], preferred_element_type=F32)
    n_qk_tiles = 2 * d_qk // tn
    n_v_tiles = d_v // tn
    half = head_dk // 2

    @pl.when(j < n_qk_tiles)
    def _():
        scale = jnp.where(j * tn >= d_qk, k_scale, 1.0).astype(F32)
        cos = cos_ref[...]
        sin = sin_ref[...]
        for hh in range(tn // head_dk):
            x1 = acc[:, hh * head_dk:hh * head_dk + half]
            x2 = acc[:, hh * head_dk + half:(hh + 1) * head_dk]
            qkv_ref[:, hh * head_dk:hh * head_dk + half] = ((x1 * cos - x2 * sin) * scale).astype(qkv_ref.dtype)
            qkv_ref[:, hh * head_dk + half:(hh + 1) * head_dk] = ((x1 * sin + x2 * cos) * scale).astype(qkv_ref.dtype)

    @pl.when(jnp.logical_and(j >= n_qk_tiles, j < n_qk_tiles + n_v_tiles))
    def _():
        qkv_ref[...] = acc.astype(qkv_ref.dtype)

    @pl.when(j >= n_qk_tiles + n_v_tiles)
    def _():
        g_ref[...] = acc


def _ret_proj_call(x, gam, mod, w, cos, sin, lay, tm=1024, tn=1024):
    n, d = x.shape
    sub = lay["sub"]
    d_qk = RET_HEADS * lay["dk"]
    d_v = RET_HEADS * lay["dv"]
    n_a = (2 * d_qk + d_v) // tn
    n_all = w.shape[1] // tn
    tps = lay["t_prompt"] // tm
    npt = lay["n_prompt_rows"] // tm
    midx = functools.partial(_mod_index, n_prompt_tiles=npt, tiles_per_seq=tps, n_prompt=lay["n_prompt"])
    half = lay["dk"] // 2
    blocks = (_nbytes((tm, d), F32) + 2 * _nbytes((sub, d), F32) + _nbytes((d, tn), BF16)
              + 2 * _nbytes((tm, half), F32) + _nbytes((tm, tn), BF16) + _nbytes((tm, tn), F32))
    return pl.pallas_call(
        functools.partial(_ret_proj_kernel, tm=tm, tn=tn, sub=sub, d_qk=d_qk, d_v=d_v,
                          head_dk=lay["dk"], k_scale=float(lay["dk"]) ** -0.5),
        grid=(n // tm, n_all),
        in_specs=[
            pl.BlockSpec((tm, d), lambda i, j: (i, 0)),
            pl.BlockSpec((1, d), lambda i, j: (0, 0)),
            pl.BlockSpec((sub, d), lambda i, j: (midx(i), 1)),
            pl.BlockSpec((sub, d), lambda i, j: (midx(i), 0)),
            pl.BlockSpec((d, tn), lambda i, j: (0, j)),
            pl.BlockSpec((tm, half), lambda i, j: (i, 0)),
            pl.BlockSpec((tm, half), lambda i, j: (i, 0)),
        ],
        out_specs=[
            pl.BlockSpec((tm, tn), lambda i, j: (i, jnp.minimum(j, n_a - 1))),
            pl.BlockSpec((tm, tn), lambda i, j: (i, jnp.maximum(j - n_a, 0))),
        ],
        out_shape=[jax.ShapeDtypeStruct((n, 2 * d_qk + d_v), BF16),
                   jax.ShapeDtypeStruct((n, d_v), F32)],
        scratch_shapes=[pltpu.VMEM((tm, d), BF16)],
        compiler_params=pltpu.CompilerParams(
            dimension_semantics=("arbitrary", "arbitrary"),
            vmem_limit_bytes=_vmem_limit(blocks, _nbytes((tm, d), BF16))),
        name="ret_in_proj",
    )(x, gam.reshape(1, d), mod, mod, w, cos, sin)


def _ffn_in_kernel(x_ref, gam_ref, sc_ref, sh_ref, wg_ref, wu_ref, o_ref, h_ref, *, tm, sub):
    @pl.when(pl.program_id(1) == 0)
    def _():
        _norm_mod_rows(x_ref, gam_ref, sc_ref, sh_ref, h_ref, tm=tm, sub=sub)

    h = h_ref[...]
    gate = jnp.dot(h, wg_ref[...], preferred_element_type=F32)
    up = jnp.dot(h, wu_ref[...], preferred_element_type=F32)
    o_ref[...] = (_silu(gate) * up).astype(o_ref.dtype)


def _ffn_in_call(x, gam, mod, w, lay, tm=1024, tn=512):
    n, d = x.shape
    sub = lay["sub"]
    d_ff = w.shape[1] // 2
    n_j = d_ff // tn
    tps = lay["t_prompt"] // tm
    npt = lay["n_prompt_rows"] // tm
    midx = functools.partial(_mod_index, n_prompt_tiles=npt, tiles_per_seq=tps, n_prompt=lay["n_prompt"])
    blocks = (_nbytes((tm, d), F32) + 2 * _nbytes((sub, d), F32) + 2 * _nbytes((d, tn), BF16)
              + _nbytes((tm, tn), BF16))
    return pl.pallas_call(
        functools.partial(_ffn_in_kernel, tm=tm, sub=sub),
        grid=(n // tm, n_j),
        in_specs=[
            pl.BlockSpec((tm, d), lambda i, j: (i, 0)),
            pl.BlockSpec((1, d), lambda i, j: (0, 0)),
            pl.BlockSpec((sub, d), lambda i, j: (midx(i), 4)),
            pl.BlockSpec((sub, d), lambda i, j: (midx(i), 3)),
            pl.BlockSpec((d, tn), lambda i, j: (0, j)),
            pl.BlockSpec((d, tn), lambda i, j: (0, j + n_j)),
        ],
        out_specs=pl.BlockSpec((tm, tn), lambda i, j: (i, j)),
        out_shape=jax.ShapeDtypeStruct((n, d_ff), BF16),
        scratch_shapes=[pltpu.VMEM((tm, d), BF16)],
        compiler_params=pltpu.CompilerParams(
            dimension_semantics=("arbitrary", "arbitrary"),
            vmem_limit_bytes=_vmem_limit(blocks, _nbytes((tm, d), BF16) + 2 * _nbytes((tm, tn), F32))),
        name="ffn_in_swiglu",
    )(x, gam.reshape(1, d), mod, mod, w, w)


def _out_proj_kernel(a_ref, w_ref, x_ref, gate_ref, o_ref, *, tm, sub):
    acc = jnp.dot(a_ref[...], w_ref[...], preferred_element_type=F32)
    gate = gate_ref[...]
    for s in range(tm // sub):
        rows = slice(s * sub, (s + 1) * sub)
        o_ref[rows, :] = x_ref[rows, :] + gate * acc[rows, :]


def _out_proj_call(a, w, x, mod, gate_col, lay, tm=1024, tn=512):
    n, k = a.shape
    d = w.shape[1]
    sub = lay["sub"]
    tps = lay["t_prompt"] // tm
    npt = lay["n_prompt_rows"] // tm
    midx = functools.partial(_mod_index, n_prompt_tiles=npt, tiles_per_seq=tps, n_prompt=lay["n_prompt"])
    n_j = d // tn
    blocks = (_nbytes((tm, k), BF16) + _nbytes((k, tn), BF16) + 2 * _nbytes((tm, tn), F32)
              + _nbytes((sub, tn), F32))
    return pl.pallas_call(
        functools.partial(_out_proj_kernel, tm=tm, sub=sub),
        grid=(n // tm, n_j),
        in_specs=[
            pl.BlockSpec((tm, k), lambda i, j: (i, 0)),
            pl.BlockSpec((k, tn), lambda i, j: (0, j)),
            pl.BlockSpec((tm, tn), lambda i, j: (i, j)),
            pl.BlockSpec((sub, tn), lambda i, j: (midx(i), gate_col * n_j + j)),
        ],
        out_specs=pl.BlockSpec((tm, tn), lambda i, j: (i, j)),
        out_shape=jax.ShapeDtypeStruct((n, d), F32),
        compiler_params=pltpu.CompilerParams(
            dimension_semantics=("arbitrary", "arbitrary"),
            vmem_limit_bytes=_vmem_limit(blocks, _nbytes((tm, tn), F32))),
        name="out_proj_residual",
    )(a, w, x, mod)


def _retention_decays(c):
    lg = jnp.log1p(-jnp.exp2(-5.0 - jnp.arange(RET_HEADS, dtype=F32)))
    idx = jnp.arange(c, dtype=F32)
    diff = idx[:, None] - idx[None, :]
    causal = diff >= 0
    dmask = jnp.where(causal[None], jnp.exp(jnp.where(causal, diff, 0.0)[None] * lg[:, None, None]), 0.0)
    qdec = jnp.exp((idx + 1.0)[None, :] * lg[:, None])
    kdec = jnp.exp((c - 1.0 - idx)[None, :] * lg[:, None])
    sdec = jnp.exp(c * lg)
    return dmask, qdec[:, :, None], kdec[:, :, None], sdec[:, None, None]


def _group_norm_gate(o, g, gng):
    mu = jnp.mean(o, axis=-1, keepdims=True)
    dlt = o - mu
    var = jnp.mean(dlt * dlt, axis=-1, keepdims=True)
    return _silu(g) * (dlt * lax.rsqrt(var + EPS) * gng)


_NT = (((1,), (1,)), ((), ()))
_TN = (((0,), (0,)), ((), ()))


def _retention_chunk(q, k, v, s_prev, dmask, qdec, kdec, sdec):
    scores = lax.dot_general(q, k, _NT, preferred_element_type=F32) * dmask
    o = (jnp.dot(scores.astype(BF16), v, preferred_element_type=F32)
         + qdec * jnp.dot(q, s_prev.astype(BF16), preferred_element_type=F32))
    kd = (k.astype(F32) * kdec).astype(BF16)
    s_new = sdec * s_prev + lax.dot_general(kd, v, _TN, preferred_element_type=F32)
    return o, s_new


def _ret_prompt_kernel(q_ref, k_ref, v_ref, g_ref, gng_ref, dmask_ref, qdec_ref, kdec_ref, sdec_ref,
                       o_ref, s_out_ref, s_ref, *, chunk, n_chunks):
    s_ref[...] = jnp.zeros_like(s_ref)
    dmask = dmask_ref[0]
    qdec = qdec_ref[0]
    kdec = kdec_ref[0]
    sdec = sdec_ref[0]
    gng = gng_ref[...]

    def body(c, carry):
        rows = pl.ds(pl.multiple_of(c * chunk, chunk), chunk)
        o, s_new = _retention_chunk(q_ref[rows, :], k_ref[rows, :], v_ref[rows, :], s_ref[...],
                                    dmask, qdec, kdec, sdec)
        s_ref[...] = s_new
        o_ref[rows, :] = _group_norm_gate(o, g_ref[rows, :], gng).astype(o_ref.dtype)
        return carry

    lax.fori_loop(0, n_chunks, body, 0)
    s_out_ref[0, 0] = s_ref[...]


def _ret_prompt_call(qkv, g, gng, lay, chunk=RET_CHUNK):
    n = qkv.shape[0]
    bp, tp, dk, dv = lay["n_prompt"], lay["t_prompt"], lay["dk"], lay["dv"]
    h = RET_HEADS
    dmask, qdec, kdec, sdec = _retention_decays(chunk)
    v_blk0 = 2 * h * dk // dv
    blocks = (2 * _nbytes((tp, dk), BF16) + _nbytes((tp, dv), BF16) + _nbytes((tp, dv), F32)
              + _nbytes((tp, dv), BF16) + 2 * _nbytes((dk, dv), F32) + _nbytes((chunk, chunk), F32))
    return pl.pallas_call(
        functools.partial(_ret_prompt_kernel, chunk=chunk, n_chunks=tp // chunk),
        grid=(bp, h),
        in_specs=[
            pl.BlockSpec((tp, dk), lambda b, hh: (b, hh)),
            pl.BlockSpec((tp, dk), lambda b, hh: (b, h + hh)),
            pl.BlockSpec((tp, dv), lambda b, hh: (b, v_blk0 + hh)),
            pl.BlockSpec((tp, dv), lambda b, hh: (b, hh)),
            pl.BlockSpec((1, dv), lambda b, hh: (0, hh)),
            pl.BlockSpec((1, chunk, chunk), lambda b, hh: (hh, 0, 0)),
            pl.BlockSpec((1, chunk, 1), lambda b, hh: (hh, 0, 0)),
            pl.BlockSpec((1, chunk, 1), lambda b, hh: (hh, 0, 0)),
            pl.BlockSpec((1, 1, 1), lambda b, hh: (hh, 0, 0)),
        ],
        out_specs=[
            pl.BlockSpec((tp, dv), lambda b, hh: (b, hh)),
            pl.BlockSpec((1, 1, dk, dv), lambda b, hh: (b, hh, 0, 0)),
        ],
        out_shape=[jax.ShapeDtypeStruct((n, h * dv), BF16),
                   jax.ShapeDtypeStruct((bp, h, dk, dv), F32)],
        scratch_shapes=[pltpu.VMEM((dk, dv), F32)],
        compiler_params=pltpu.CompilerParams(
            dimension_semantics=("arbitrary", "arbitrary"),
            vmem_limit_bytes=_vmem_limit(blocks, _nbytes((dk, dv), F32))),
        name="retention_prompt",
    )(qkv, qkv, qkv, g, gng.reshape(1, h * dv), dmask, qdec, kdec, sdec)


def _ret_sample_kernel(q_ref, k_ref, v_ref, g_ref, gng_ref, s0_ref, dmask_ref, qdec_ref, kdec_ref, sdec_ref,
                       o_ref, s_out_ref, *, bb):
    dmask = dmask_ref[0]
    qdec = qdec_ref[0]
    kdec = kdec_ref[0]
    sdec = sdec_ref[0]
    gng = gng_ref[...]

    def body(b, carry):
        o, s_new = _retention_chunk(q_ref[b].astype(BF16), k_ref[b].astype(BF16), v_ref[b].astype(BF16),
                                    s0_ref[b, 0], dmask, qdec, kdec, sdec)
        s_out_ref[b, 0] = s_new
        o_ref[b] = _group_norm_gate(o, g_ref[b], gng)
        return carry

    lax.fori_loop(0, bb, body, 0)


def _ret_sample_call(qkv_s, g_s, gng, state, lay, bb=16):
    bs, ts, dk, dv = lay["sub"], lay["t_sample"], lay["dk"], lay["dv"]
    h = RET_HEADS
    dmask, qdec, kdec, sdec = _retention_decays(ts)
    v_blk0 = 2 * h * dk // dv
    blocks = (2 * _nbytes((bb, ts, dk), F32) + 3 * _nbytes((bb, ts, dv), F32) + 2 * _nbytes((bb, dk, dv), F32))
    return pl.pallas_call(
        functools.partial(_ret_sample_kernel, bb=bb),
        grid=(bs // bb, h),
        in_specs=[
            pl.BlockSpec((bb, ts, dk), lambda b, hh: (b, 0, hh)),
            pl.BlockSpec((bb, ts, dk), lambda b, hh: (b, 0, h + hh)),
            pl.BlockSpec((bb, ts, dv), lambda b, hh: (b, 0, v_blk0 + hh)),
            pl.BlockSpec((bb, ts, dv), lambda b, hh: (b, 0, hh)),
            pl.BlockSpec((1, dv), lambda b, hh: (0, hh)),
            pl.BlockSpec((bb, 1, dk, dv), lambda b, hh: (b, hh, 0, 0)),
            pl.BlockSpec((1, ts, ts), lambda b, hh: (hh, 0, 0)),
            pl.BlockSpec((1, ts, 1), lambda b, hh: (hh, 0, 0)),
            pl.BlockSpec((1, ts, 1), lambda b, hh: (hh, 0, 0)),
            pl.BlockSpec((1, 1, 1), lambda b, hh: (hh, 0, 0)),
        ],
        out_specs=[
            pl.BlockSpec((bb, ts, dv), lambda b, hh: (b, 0, hh)),
            pl.BlockSpec((bb, 1, dk, dv), lambda b, hh: (b, hh, 0, 0)),
        ],
        out_shape=[jax.ShapeDtypeStruct((bs, ts, h * dv), F32),
                   jax.ShapeDtypeStruct((bs, h, dk, dv), F32)],
        compiler_params=pltpu.CompilerParams(
            dimension_semantics=("arbitrary", "arbitrary"),
            vmem_limit_bytes=_vmem_limit(blocks)),
        name="retention_sample",
    )(qkv_s, qkv_s, qkv_s, g_s, gng.reshape(1, h * dv), state, dmask, qdec, kdec, sdec)


def _window_sum_rows(win_ref, halo, r0, rows, win, stride):
    if stride % V7X_SUBLANES == 0:
        acc = win_ref[pl.ds(halo + r0, rows), :]
        for k in range(1, win):
            acc = acc + win_ref[pl.ds(halo - k * stride + r0, rows), :]
        return acc
    assert stride == 1 and halo == 16 and win <= 16
    blk = win_ref[pl.ds(r0, rows + 16), :]
    shift = 1
    while shift < win:
        blk = blk + pltpu.roll(blk, shift, 0)
        shift *= 2
    return blk[16:, :]


def _pool_kernel(u_ref, hist_ref, x_ref, gate_ref, w_ref, scale_ref, o_ref, hist_out_ref, win_ref, m_ref,
                 *, tm, sub, stride, tiles_per_seq, hist_len, chunk_rows):
    g = pl.program_id(0)
    i = pl.program_id(1)
    halo = 16 * stride
    tile_in_seq = i % tiles_per_seq

    if tiles_per_seq > 1:
        @pl.when(tile_in_seq != 0)
        def _():
            win_ref[0:halo, :] = win_ref[tm:tm + halo, :]

    @pl.when(tile_in_seq == 0)
    def _():
        win_ref[0:halo, :] = hist_ref[0]

    win_ref[halo:halo + tm, :] = u_ref[...]

    for gi, win in enumerate(POOL_WINDOWS):
        @pl.when(g == gi)
        def _(win=win):
            def body(c, carry):
                r0 = pl.multiple_of(c * chunk_rows, chunk_rows)
                cur = win_ref[pl.ds(halo + r0, chunk_rows), :]
                acc = _window_sum_rows(win_ref, halo, r0, chunk_rows, win, stride)
                t =(tile_in_seq * tm + r0 + lax.broadcasted_iota(jnp.int32, (chunk_rows, 1), 0)) // stride
                cnt = jnp.minimum(win, t + 1 + hist_len).astype(F32)
                m_ref[pl.ds(r0, chunk_rows), :] = (acc / cnt - cur).astype(m_ref.dtype)
                return carry

            lax.fori_loop(0, tm // chunk_rows, body, 0)

    y = jnp.dot(m_ref[...], w_ref[0], preferred_element_type=F32) * scale_ref[...]
    gate = gate_ref[...]
    for s in range(tm // sub):
        rows = slice(s * sub, (s + 1) * sub)
        o_ref[rows, :] = x_ref[rows, :] + gate * y[rows, :]
    hist_out_ref[0] = win_ref[tm + stride:tm + halo, :]


def _pool_call(u, hist, x, mod, w, scale, x_prev, *, row0, n_rows, n_seq, stride, hist_len, mod_seq0, lay, tm=1024):
    n, d = x.shape
    sub = lay["sub"]
    n_g = len(POOL_WINDOWS)
    gc = d // n_g
    n_tiles = n_rows // tm
    tps = n_tiles // n_seq
    blk0 = row0 // tm
    halo = 16 * stride
    chunk_rows = 64
    blocks = (3 * _nbytes((tm, gc), F32) + _nbytes((halo, gc), F32) + _nbytes((sub, gc), F32)
              + _nbytes((gc, gc), BF16) + _nbytes((15 * stride, gc), F32))
    scratch = _nbytes((halo + tm, gc), F32) + _nbytes((tm, gc), BF16)
    gate_col0 = 2 * n_g
    kern = functools.partial(_pool_kernel, tm=tm, sub=sub, stride=stride, tiles_per_seq=tps,
                             hist_len=hist_len, chunk_rows=chunk_rows)
    in_specs = [
        pl.BlockSpec((tm, gc), lambda g, i: (blk0 + i, g)),
        pl.BlockSpec((1, halo, gc), lambda g, i: (i // tps, 0, g)),
        pl.BlockSpec((tm, gc), lambda g, i: (blk0 + i, g)),
        pl.BlockSpec((sub, gc), lambda g, i: (mod_seq0 + i // tps, gate_col0 + g)),
        pl.BlockSpec((1, gc, gc), lambda g, i: (g, 0, 0)),
        pl.BlockSpec((1, gc), lambda g, i: (0, g)),
    ]
    args = [u, hist, x, mod, w, scale.reshape(1, d)]
    aliases = {}
    if x_prev is not None:
        in_specs.append(pl.BlockSpec(memory_space=pl.ANY))
        args.append(x_prev)
        aliases = {len(args) - 1: 0}
        kern_fn = lambda *refs: kern(*refs[:6], *refs[7:])
    else:
        kern_fn = kern
    return pl.pallas_call(
        kern_fn,
        grid=(n_g, n_tiles),
        in_specs=in_specs,
        out_specs=[
            pl.BlockSpec((tm, gc), lambda g, i: (blk0 + i, g)),
            pl.BlockSpec((1, 15 * stride, gc), lambda g, i: (i // tps, 0, g)),
        ],
        out_shape=[jax.ShapeDtypeStruct((n, d), F32),
                   jax.ShapeDtypeStruct((n_seq, 15 * stride, d), F32)],
        scratch_shapes=[pltpu.VMEM((halo + tm, gc), F32), pltpu.VMEM((tm, gc), BF16)],
        input_output_aliases=aliases,
        compiler_params=pltpu.CompilerParams(
            dimension_semantics=("arbitrary", "arbitrary"),
            vmem_limit_bytes=_vmem_limit(blocks, scratch)),
        name="pool_mixer_s%d" % stride,
    )(*args)


def _rope_tables(pos, dk):
    half = dk // 2
    inv = 1.0 / (ROPE_BASE ** (jnp.arange(half, dtype=F32) / half))
    ang = pos[:, None] * inv[None, :]
    return jnp.cos(ang), jnp.sin(ang)


def kernel(x_prompt, x_sample, c_prompt, c_sample, state_ret, state_pool, norm_mix_g, norm_ffn_g, ada_w, ada_b,
           ret_w_in, ret_gn_g, ret_w_out, pool_w, pool_scale, ffn_w_in, ffn_w_out, final_norm_g):
    bp, tp, d = x_prompt.shape
    bs, ts, _ = x_sample.shape
    depth = ada_w.shape[0]
    dk = state_ret.shape[3]
    dv = state_ret.shape[4]
    n_hist = state_pool.shape[2]
    assert bp <= C_PAD and n_hist == max(POOL_WINDOWS) - 1 and depth == 2
    lay = dict(n_prompt=bp, t_prompt=tp, n_prompt_rows=bp * tp, sub=bs, t_sample=ts, dk=dk, dv=dv)
    n_p = bp * tp
    n_s = bs * ts

    x = jnp.concatenate([x_prompt.reshape(n_p, d), x_sample.transpose(1, 0, 2).reshape(n_s, d)], axis=0)
    c_all = jnp.concatenate([c_prompt, jnp.zeros((C_PAD - bp, d), F32), c_sample], axis=0)
    mod = _ada_call(c_all, ada_w, ada_b, bp, bs)

    pos = jnp.concatenate([jnp.tile(jnp.arange(tp, dtype=F32), bp),
                           jnp.repeat(float(PAST_LEN) + jnp.arange(ts, dtype=F32), bs)])
    cos, sin = _rope_tables(pos, dk)

    qkv, g = _ret_proj_call(x, norm_mix_g[0], mod[0], ret_w_in[0].astype(BF16), cos, sin, lay)
    gated, s_ret_p = _ret_prompt_call(qkv, g, ret_gn_g[0], lay)
    qkv_s = qkv[n_p:].reshape(ts, bs, -1).transpose(1, 0, 2).astype(F32)
    g_s = g[n_p:].reshape(ts, bs, -1).transpose(1, 0, 2)
    gated_s, s_ret_s = _ret_sample_call(qkv_s, g_s, ret_gn_g[0], state_ret[0], lay)
    gated = lax.dynamic_update_slice(
        gated, gated_s.transpose(1, 0, 2).reshape(n_s, -1).astype(BF16), (n_p, 0))
    x = _out_proj_call(gated, ret_w_out[0].astype(BF16), x, mod[0], 2, lay, tn=1024)
    hmid = _ffn_in_call(x, norm_ffn_g[0], mod[0], ffn_w_in[0].astype(BF16), lay)
    x = _out_proj_call(hmid, ffn_w_out[0].astype(BF16), x, mod[0], 5, lay)

    u = _norm_call(x, norm_mix_g[1], mod[1], 1, 0, lay)
    pw = pool_w[0].astype(BF16)
    hist_p = jnp.zeros((bp, 16, d), F32)
    hist_s = jnp.concatenate([jnp.zeros((1, bs, d), F32), state_pool[0].transpose(1, 0, 2)], axis=0)
    x1, nh_p = _pool_call(u, hist_p, x, mod[1], pw, pool_scale[0], None, row0=0, n_rows=n_p, n_seq=bp,
                          stride=1, hist_len=0, mod_seq0=0, lay=lay)
    x1, nh_s = _pool_call(u, hist_s.reshape(1, 16 * bs, d), x, mod[1], pw, pool_scale[0], x1, row0=n_p,
                          n_rows=n_s, n_seq=1, stride=bs, hist_len=min(PAST_LEN, n_hist), mod_seq0=bp, lay=lay)
    x = x1
    hmid = _ffn_in_call(x, norm_ffn_g[1], mod[1], ffn_w_in[1].astype(BF16), lay)
    x = _out_proj_call(hmid, ffn_w_out[1].astype(BF16), x, mod[1], 5, lay)

    y = _norm_call(x, final_norm_g, None, 0, 0, lay)
    y_prompt = y[:n_p].reshape(bp, tp, d)
    y_sample = y[n_p:].reshape(ts, bs, d).transpose(1, 0, 2)
    state_pool_sample = nh_s.reshape(n_hist, bs, d).transpose(1, 0, 2)
    return (y_prompt, y_sample, s_ret_p[None], nh_p[None], s_ret_s[None], state_pool_sample[None])
```

```python
import functools

import jax
import jax.numpy as jnp
from jax import lax
from jax.experimental import pallas as pl
from jax.experimental.pallas import tpu as pltpu

F32 = jnp.float32
BF16 = jnp.bfloat16

RET_HEADS = 8
RET_CHUNK = 128
ROPE_BASE = 10000.0
POOL_WINDOWS = (2, 4, 8, 16)
N_ADA = 6
EPS = 1e-6
PAST_LEN = 16384

V7X_VMEM_BYTES = 64 * 1024 * 1024
V7X_LANES = 128
V7X_SUBLANES = 8
NORM_ROWS = 16
C_PAD = 8


def _vmem_limit(block_bytes, scratch_bytes=0):
    need = 2 * block_bytes + scratch_bytes + 12 * 1024 * 1024
    return int(min(need, V7X_VMEM_BYTES - 6 * 1024 * 1024))


def _nbytes(shape, dtype):
    n = 1
    for s in shape:
        n *= s
    return n * jnp.dtype(dtype).itemsize


def _silu(x):
    return x / (1.0 + jnp.exp(-x))


def _ada_kernel(c_ref, w_ref, b_ref, o_ref, *, n_prompt, sub):
    cs = _silu(c_ref[...])
    res = jnp.dot(cs.astype(BF16), w_ref[0].astype(BF16), preferred_element_type=F32) + b_ref[0]
    tn = res.shape[-1]
    for b in range(n_prompt):
        o_ref[0, b * sub:(b + 1) * sub, :] = jnp.broadcast_to(res[b:b + 1, :], (sub, tn))
    o_ref[0, n_prompt * sub:, :] = res[C_PAD:C_PAD + sub, :]


def _ada_call(c_all, ada_w, ada_b, n_prompt, sub, tn=1024):
    depth, d, n6 = ada_w.shape
    rows = c_all.shape[0]
    out_rows = (n_prompt + 1) * sub
    blocks = (_nbytes((rows, d), F32) + _nbytes((d, tn), F32) + _nbytes((1, tn), F32)
              + _nbytes((out_rows, tn), F32))
    return pl.pallas_call(
        functools.partial(_ada_kernel, n_prompt=n_prompt, sub=sub),
        grid=(depth, n6 // tn),
        in_specs=[
            pl.BlockSpec((rows, d), lambda l, j: (0, 0)),
            pl.BlockSpec((1, d, tn), lambda l, j: (l, 0, j)),
            pl.BlockSpec((1, 1, tn), lambda l, j: (l, 0, j)),
        ],
        out_specs=pl.BlockSpec((1, out_rows, tn), lambda l, j: (l, 0, j)),
        out_shape=jax.ShapeDtypeStruct((depth, out_rows, n6), F32),
        compiler_params=pltpu.CompilerParams(
            dimension_semantics=("arbitrary", "arbitrary"),
            vmem_limit_bytes=_vmem_limit(blocks, _nbytes((d, tn), BF16))),
        name="ada_mod",
    )(c_all, ada_w, ada_b.reshape(depth, 1, n6))


def _norm_mod_rows(x_ref, gam_ref, sc_ref, sh_ref, out_ref, *, tm, sub):
    gam = gam_ref[...]
    per_sub = sub // NORM_ROWS

    def body(c, carry):
        r0 = pl.multiple_of(c * NORM_ROWS, NORM_ROWS)
        x = x_ref[pl.ds(r0, NORM_ROWS), :]
        hn = (x * lax.rsqrt(jnp.mean(x * x, axis=-1, keepdims=True) + EPS)) * gam
        if sc_ref is not None:
            m0 = pl.multiple_of((c % per_sub) * NORM_ROWS, NORM_ROWS)
            hn = hn * (1.0 + sc_ref[pl.ds(m0, NORM_ROWS), :]) + sh_ref[pl.ds(m0, NORM_ROWS), :]
        out_ref[pl.ds(r0, NORM_ROWS), :] = hn.astype(out_ref.dtype)
        return carry

    lax.fori_loop(0, tm // NORM_ROWS, body, 0, unroll=4)


def _mod_index(i, n_prompt_tiles, tiles_per_seq, n_prompt):
    return jnp.where(i < n_prompt_tiles, i // tiles_per_seq, n_prompt)


def _tiling(lay, tm):
    npt = lay["n_prompt_rows"] // tm
    midx = functools.partial(_mod_index, n_prompt_tiles=npt, tiles_per_seq=lay["t_prompt"] // tm,
                             n_prompt=lay["n_prompt"])
    return npt, midx


def _norm_kernel(*refs, tm, sub, n_in, n_out, has_mod, npt):
    x_refs = refs[:n_in]
    gam_ref = refs[n_in]
    sc_ref, sh_ref = (refs[n_in + 1], refs[n_in + 2]) if has_mod else (None, None)
    o_refs = refs[len(refs) - n_out:]
    if n_in == 1 and n_out == 1:
        _norm_mod_rows(x_refs[0], gam_ref, sc_ref, sh_ref, o_refs[0], tm=tm, sub=sub)
        return
    i = pl.program_id(0)

    @pl.when(i < npt)
    def _():
        _norm_mod_rows(x_refs[0], gam_ref, sc_ref, sh_ref, o_refs[0], tm=tm, sub=sub)

    @pl.when(i >= npt)
    def _():
        _norm_mod_rows(x_refs[-1], gam_ref, sc_ref, sh_ref, o_refs[-1], tm=tm, sub=sub)


def _norm_call(xs, gam, mod, layer, col_sc, col_sh, lay, out_dtype, split_out=False, tm=512):
    d = xs[0].shape[1]
    sub = lay["sub"]
    n_p, n_s = lay["n_prompt_rows"], lay["n_sample_rows"]
    npt, midx = _tiling(lay, tm)
    prompt_blk = lambda i: (jnp.minimum(i, npt - 1), 0)
    sample_blk = lambda i: (jnp.maximum(i - npt, 0), 0)
    whole_blk = lambda i: (i, 0)
    in_specs = ([pl.BlockSpec((tm, d), whole_blk)] if len(xs) == 1
                else [pl.BlockSpec((tm, d), prompt_blk), pl.BlockSpec((tm, d), sample_blk)])
    in_specs.append(pl.BlockSpec((1, d), lambda i: (0, 0)))
    args = list(xs) + [gam.reshape(1, d)]
    if mod is not None:
        in_specs += [pl.BlockSpec((None, sub, d), lambda i: (layer, midx(i), col_sc)),
                     pl.BlockSpec((None, sub, d), lambda i: (layer, midx(i), col_sh))]
        args += [mod, mod]
    if split_out:
        out_specs = [pl.BlockSpec((tm, d), prompt_blk), pl.BlockSpec((tm, d), sample_blk)]
        out_shape = [jax.ShapeDtypeStruct((n_p, d), out_dtype), jax.ShapeDtypeStruct((n_s, d), out_dtype)]
    else:
        out_specs = pl.BlockSpec((tm, d), whole_blk)
        out_shape = jax.ShapeDtypeStruct((n_p + n_s, d), out_dtype)
    n_out = 2 if split_out else 1
    blocks = (len(xs) + n_out) * _nbytes((tm, d), F32) + 2 * _nbytes((sub, d), F32)
    return pl.pallas_call(
        functools.partial(_norm_kernel, tm=tm, sub=sub, n_in=len(xs), n_out=n_out,
                          has_mod=mod is not None, npt=npt),
        grid=((n_p + n_s) // tm,),
        in_specs=in_specs,
        out_specs=out_specs,
        out_shape=out_shape,
        compiler_params=pltpu.CompilerParams(
            dimension_semantics=("arbitrary",), vmem_limit_bytes=_vmem_limit(blocks)),
        name="row_norm",
    )(*args)


def _ret_proj_kernel(h_ref, w_ref, cos_ref, sin_ref, qkv_ref, g_ref, *, tn, d_qk, d_v, head_dk, k_scale):
    j = pl.program_id(1)
    acc = jnp.dot(h_ref[...], w_ref[...], preferred_element_type=F32)
    n_qk_tiles = 2 * d_qk // tn
    n_v_tiles = d_v // tn
    half = head_dk // 2

    @pl.when(j < n_qk_tiles)
    def _():
        scale = jnp.where(j * tn >= d_qk, k_scale, 1.0).astype(F32)
        cos = cos_ref[...]
        sin = sin_ref[...]
        for hh in range(tn // head_dk):
            x1 = acc[:, hh * head_dk:hh * head_dk + half]
            x2 = acc[:, hh * head_dk + half:(hh + 1) * head_dk]
            qkv_ref[:, hh * head_dk:hh * head_dk + half] = ((x1 * cos - x2 * sin) * scale).astype(qkv_ref.dtype)
            qkv_ref[:, hh * head_dk + half:(hh + 1) * head_dk] = ((x1 * sin + x2 * cos) * scale).astype(qkv_ref.dtype)

    @pl.when(jnp.logical_and(j >= n_qk_tiles, j < n_qk_tiles + n_v_tiles))
    def _():
        qkv_ref[...] = acc.astype(qkv_ref.dtype)

    @pl.when(j >= n_qk_tiles + n_v_tiles)
    def _():
        g_ref[...] = acc


def _ret_proj_call(h, w, cos, sin, lay, tm=1024, tn=1024):
    n, d = h.shape
    d_qk = RET_HEADS * lay["dk"]
    d_v = RET_HEADS * lay["dv"]
    n_a = (2 * d_qk + d_v) // tn
    n_all = w.shape[1] // tn
    half = lay["dk"] // 2
    blocks = (_nbytes((tm, d), BF16) + _nbytes((d, tn), BF16)
              + 2 * _nbytes((tm, half), F32) + _nbytes((tm, tn), BF16) + _nbytes((tm, tn), F32))
    return pl.pallas_call(
        functools.partial(_ret_proj_kernel, tn=tn, d_qk=d_qk, d_v=d_v,
                          head_dk=lay["dk"], k_scale=float(lay["dk"]) ** -0.5),
        grid=(n // tm, n_all),
        in_specs=[
            pl.BlockSpec((tm, d), lambda i, j: (i, 0)),
            pl.BlockSpec((d, tn), lambda i, j: (0, j)),
            pl.BlockSpec((tm, half), lambda i, j: (i, 0)),
            pl.BlockSpec((tm, half), lambda i, j: (i, 0)),
        ],
        out_specs=[
            pl.BlockSpec((tm, tn), lambda i, j: (i, jnp.minimum(j, n_a - 1))),
            pl.BlockSpec((tm, tn), lambda i, j: (i, jnp.maximum(j - n_a, 0))),
        ],
        out_shape=[jax.ShapeDtypeStruct((n, 2 * d_qk + d_v), BF16),
                   jax.ShapeDtypeStruct((n, d_v), F32)],
        compiler_params=pltpu.CompilerParams(
            dimension_semantics=("arbitrary", "arbitrary"),
            vmem_limit_bytes=_vmem_limit(blocks, _nbytes((tm, tn), F32))),
        name="ret_in_proj",
    )(h, w, cos, sin)


def _ffn_in_kernel(x_ref, gam_ref, sc_ref, sh_ref, wg_ref, wu_ref, o_ref, h_ref, *, tm, sub):
    @pl.when(pl.program_id(1) == 0)
    def _():
        _norm_mod_rows(x_ref, gam_ref, sc_ref, sh_ref, h_ref, tm=tm, sub=sub)

    h = h_ref[...]
    gate = jnp.dot(h, wg_ref[...], preferred_element_type=F32)
    up = jnp.dot(h, wu_ref[...], preferred_element_type=F32)
    o_ref[...] = (_silu(gate) * up).astype(o_ref.dtype)


def _ffn_in_call(x, gam, mod, layer, w, lay, tm=1024, tn=512):
    n, d = x.shape
    sub = lay["sub"]
    d_ff = w.shape[1] // 2
    n_j = d_ff // tn
    _, midx = _tiling(lay, tm)
    blocks = (_nbytes((tm, d), F32) + 2 * _nbytes((sub, d), F32) + 2 * _nbytes((d, tn), BF16)
              + _nbytes((tm, tn), BF16))
    return pl.pallas_call(
        functools.partial(_ffn_in_kernel, tm=tm, sub=sub),
        grid=(n // tm, n_j),
        in_specs=[
            pl.BlockSpec((tm, d), lambda i, j: (i, 0)),
            pl.BlockSpec((1, d), lambda i, j: (0, 0)),
            pl.BlockSpec((None, sub, d), lambda i, j: (layer, midx(i), 4)),
            pl.BlockSpec((None, sub, d), lambda i, j: (layer, midx(i), 3)),
            pl.BlockSpec((d, tn), lambda i, j: (0, j)),
            pl.BlockSpec((d, tn), lambda i, j: (0, j + n_j)),
        ],
        out_specs=pl.BlockSpec((tm, tn), lambda i, j: (i, j)),
        out_shape=jax.ShapeDtypeStruct((n, d_ff), BF16),
        scratch_shapes=[pltpu.VMEM((tm, d), BF16)],
        compiler_params=pltpu.CompilerParams(
            dimension_semantics=("arbitrary", "arbitrary"),
            vmem_limit_bytes=_vmem_limit(blocks, _nbytes((tm, d), BF16) + 2 * _nbytes((tm, tn), F32))),
        name="ffn_in_swiglu",
    )(x, gam.reshape(1, d), mod, mod, w, w)


def _out_proj_kernel(a_ref, w_ref, *refs, tm, sub, npt):
    x_refs, gate_ref, o_ref = refs[:-2], refs[-2], refs[-1]
    acc = jnp.dot(a_ref[...], w_ref[...], preferred_element_type=F32)
    gate = gate_ref[...]
    is_prompt = pl.program_id(0) < npt
    for s in range(tm // sub):
        rows = slice(s * sub, (s + 1) * sub)
        x = x_refs[0][rows, :]
        if len(x_refs) == 2:
            x = jnp.where(is_prompt, x, x_refs[1][rows, :])
        o_ref[rows, :] = x + gate * acc[rows, :]


def _out_proj_call(a, w, xs, mod, layer, gate_col, lay, tm=1024, tn=512):
    n, k = a.shape
    d = w.shape[1]
    sub = lay["sub"]
    npt, midx = _tiling(lay, tm)
    n_j = d // tn
    x_specs = ([pl.BlockSpec((tm, tn), lambda i, j: (i, j))] if len(xs) == 1
               else [pl.BlockSpec((tm, tn), lambda i, j: (jnp.minimum(i, npt - 1), j)),
                     pl.BlockSpec((tm, tn), lambda i, j: (jnp.maximum(i - npt, 0), j))])
    blocks = (_nbytes((tm, k), BF16) + _nbytes((k, tn), BF16) + (1 + len(xs)) * _nbytes((tm, tn), F32)
              + _nbytes((sub, tn), F32))
    return pl.pallas_call(
        functools.partial(_out_proj_kernel, tm=tm, sub=sub, npt=npt),
        grid=(n // tm, n_j),
        in_specs=[pl.BlockSpec((tm, k), lambda i, j: (i, 0)),
                  pl.BlockSpec((k, tn), lambda i, j: (0, j))] + x_specs
                 + [pl.BlockSpec((None, sub, tn), lambda i, j: (layer, midx(i), gate_col * n_j + j))],
        out_specs=pl.BlockSpec((tm, tn), lambda i, j: (i, j)),
        out_shape=jax.ShapeDtypeStruct((n, d), F32),
        compiler_params=pltpu.CompilerParams(
            dimension_semantics=("arbitrary", "arbitrary"),
            vmem_limit_bytes=_vmem_limit(blocks, _nbytes((tm, tn), F32))),
        name="out_proj_residual",
    )(a, w, *xs, mod)


def _retention_decays(c):
    lg = jnp.log1p(-jnp.exp2(-5.0 - jnp.arange(RET_HEADS, dtype=F32)))
    idx = jnp.arange(c, dtype=F32)
    diff = idx[:, None] - idx[None, :]
    causal = diff >= 0
    dmask = jnp.where(causal[None], jnp.exp(jnp.where(causal, diff, 0.0)[None] * lg[:, None, None]), 0.0)
    qdec = jnp.exp((idx + 1.0)[None, :] * lg[:, None])
    kdec = jnp.exp((c - 1.0 - idx)[None, :] * lg[:, None])
    sdec = jnp.exp(c * lg)
    return dmask, qdec[:, :, None], kdec[:, :, None], sdec[:, None, None]


def _group_norm_gate(o, g, gng):
    mu = jnp.mean(o, axis=-1, keepdims=True)
    dlt = o - mu
    var = jnp.mean(dlt * dlt, axis=-1, keepdims=True)
    return _silu(g) * (dlt * lax.rsqrt(var + EPS) * gng)


_NT = (((1,), (1,)), ((), ()))
_TN = (((0,), (0,)), ((), ()))


def _retention_chunk(q, k, v, s_prev, dmask, qdec, kdec, sdec):
    scores = lax.dot_general(q, k, _NT, preferred_element_type=F32) * dmask
    o = (jnp.dot(scores.astype(BF16), v, preferred_element_type=F32)
         + qdec * jnp.dot(q, s_prev.astype(BF16), preferred_element_type=F32))
    kd = (k.astype(F32) * kdec).astype(BF16)
    s_new = sdec * s_prev + lax.dot_general(kd, v, _TN, preferred_element_type=F32)
    return o, s_new


def _ret_prompt_kernel(q_ref, k_ref, v_ref, g_ref, gng_ref, dmask_ref, qdec_ref, kdec_ref, sdec_ref,
                       o_ref, s_out_ref, s_ref, *, chunk, n_chunks):
    s_ref[...] = jnp.zeros_like(s_ref)
    dmask = dmask_ref[0]
    qdec = qdec_ref[0]
    kdec = kdec_ref[0]
    sdec = sdec_ref[0]
    gng = gng_ref[...]

    def body(c, carry):
        rows = pl.ds(pl.multiple_of(c * chunk, chunk), chunk)
        o, s_new = _retention_chunk(q_ref[rows, :], k_ref[rows, :], v_ref[rows, :], s_ref[...],
                                    dmask, qdec, kdec, sdec)
        s_ref[...] = s_new
        o_ref[rows, :] = _group_norm_gate(o, g_ref[rows, :], gng).astype(o_ref.dtype)
        return carry

    lax.fori_loop(0, n_chunks, body, 0, unroll=2)
    s_out_ref[0, 0] = s_ref[...]


def _ret_prompt_call(qkv, g, gng, lay, chunk=RET_CHUNK):
    n = qkv.shape[0]
    bp, tp, dk, dv = lay["n_prompt"], lay["t_prompt"], lay["dk"], lay["dv"]
    h = RET_HEADS
    dmask, qdec, kdec, sdec = _retention_decays(chunk)
    v_blk0 = 2 * h * dk // dv
    blocks = (2 * _nbytes((tp, dk), BF16) + _nbytes((tp, dv), BF16) + _nbytes((tp, dv), F32)
              + _nbytes((tp, dv), BF16) + 2 * _nbytes((dk, dv), F32) + _nbytes((chunk, chunk), F32))
    return pl.pallas_call(
        functools.partial(_ret_prompt_kernel, chunk=chunk, n_chunks=tp // chunk),
        grid=(bp, h),
        in_specs=[
            pl.BlockSpec((tp, dk), lambda b, hh: (b, hh)),
            pl.BlockSpec((tp, dk), lambda b, hh: (b, h + hh)),
            pl.BlockSpec((tp, dv), lambda b, hh: (b, v_blk0 + hh)),
            pl.BlockSpec((tp, dv), lambda b, hh: (b, hh)),
            pl.BlockSpec((1, dv), lambda b, hh: (0, hh)),
            pl.BlockSpec((1, chunk, chunk), lambda b, hh: (hh, 0, 0)),
            pl.BlockSpec((1, chunk, 1), lambda b, hh: (hh, 0, 0)),
            pl.BlockSpec((1, chunk, 1), lambda b, hh: (hh, 0, 0)),
            pl.BlockSpec((1, 1, 1), lambda b, hh: (hh, 0, 0)),
        ],
        out_specs=[
            pl.BlockSpec((tp, dv), lambda b, hh: (b, hh)),
            pl.BlockSpec((1, 1, dk, dv), lambda b, hh: (b, hh, 0, 0)),
        ],
        out_shape=[jax.ShapeDtypeStruct((n, h * dv), BF16),
                   jax.ShapeDtypeStruct((bp, h, dk, dv), F32)],
        scratch_shapes=[pltpu.VMEM((dk, dv), F32)],
        compiler_params=pltpu.CompilerParams(
            dimension_semantics=("arbitrary", "arbitrary"),
            vmem_limit_bytes=_vmem_limit(blocks, _nbytes((dk, dv), F32))),
        name="retention_prompt",
    )(qkv, qkv, qkv, g, gng.reshape(1, h * dv), dmask, qdec, kdec, sdec)


def _ret_sample_kernel(q_ref, k_ref, v_ref, g_ref, gng_ref, s0_ref, dmask_ref, qdec_ref, kdec_ref, sdec_ref,
                       o_ref, s_out_ref, *, bb):
    dmask = dmask_ref[0]
    qdec = qdec_ref[0]
    kdec = kdec_ref[0]
    sdec = sdec_ref[0]
    gng = gng_ref[...]

    def body(b, carry):
        o, s_new = _retention_chunk(q_ref[b].astype(BF16), k_ref[b].astype(BF16), v_ref[b].astype(BF16),
                                    s0_ref[b, 0], dmask, qdec, kdec, sdec)
        s_out_ref[b, 0] = s_new
        o_ref[b] = _group_norm_gate(o, g_ref[b], gng)
        return carry

    lax.fori_loop(0, bb, body, 0, unroll=4)


def _ret_sample_call(qkv_s, g_s, gng, state, lay, bb=16):
    bs, ts, dk, dv = lay["sub"], lay["t_sample"], lay["dk"], lay["dv"]
    h = RET_HEADS
    dmask, qdec, kdec, sdec = _retention_decays(ts)
    v_blk0 = 2 * h * dk // dv
    blocks = (2 * _nbytes((bb, ts, dk), F32) + 3 * _nbytes((bb, ts, dv), F32) + 2 * _nbytes((bb, dk, dv), F32))
    return pl.pallas_call(
        functools.partial(_ret_sample_kernel, bb=bb),
        grid=(bs // bb, h),
        in_specs=[
            pl.BlockSpec((bb, ts, dk), lambda b, hh: (b, 0, hh)),
            pl.BlockSpec((bb, ts, dk), lambda b, hh: (b, 0, h + hh)),
            pl.BlockSpec((bb, ts, dv), lambda b, hh: (b, 0, v_blk0 + hh)),
            pl.BlockSpec((bb, ts, dv), lambda b, hh: (b, 0, hh)),
            pl.BlockSpec((1, dv), lambda b, hh: (0, hh)),
            pl.BlockSpec((bb, 1, dk, dv), lambda b, hh: (b, hh, 0, 0)),
            pl.BlockSpec((1, ts, ts), lambda b, hh: (hh, 0, 0)),
            pl.BlockSpec((1, ts, 1), lambda b, hh: (hh, 0, 0)),
            pl.BlockSpec((1, ts, 1), lambda b, hh: (hh, 0, 0)),
            pl.BlockSpec((1, 1, 1), lambda b, hh: (hh, 0, 0)),
        ],
        out_specs=[
            pl.BlockSpec((bb, ts, dv), lambda b, hh: (b, 0, hh)),
            pl.BlockSpec((bb, 1, dk, dv), lambda b, hh: (b, hh, 0, 0)),
        ],
        out_shape=[jax.ShapeDtypeStruct((bs, ts, h * dv), F32),
                   jax.ShapeDtypeStruct((bs, h, dk, dv), F32)],
        compiler_params=pltpu.CompilerParams(
            dimension_semantics=("arbitrary", "arbitrary"),
            vmem_limit_bytes=_vmem_limit(blocks)),
        name="retention_sample",
    )(qkv_s, qkv_s, qkv_s, g_s, gng.reshape(1, h * dv), state, dmask, qdec, kdec, sdec)


def _window_sum_rows(win_ref, halo, r0, rows, win, stride):
    if stride % V7X_SUBLANES == 0:
        acc = win_ref[pl.ds(halo + r0, rows), :]
        for k in range(1, win):
            acc = acc + win_ref[pl.ds(halo - k * stride + r0, rows), :]
        return acc
    assert stride == 1 and halo == 16 and win <= 16
    blk = win_ref[pl.ds(r0, rows + 16), :]
    shift = 1
    while shift < win:
        blk = blk + pltpu.roll(blk, shift, 0)
        shift *= 2
    return blk[16:, :]


def _pool_kernel(u_ref, hist_ref, x_ref, gate_ref, w_ref, scale_ref, *rest,
                 tm, sub, stride, tiles_per_seq, hist_len, chunk_rows):
    o_ref, hist_out_ref, win_ref, m_ref = rest[-4:]
    g = pl.program_id(0)
    i = pl.program_id(1)
    halo = 16 * stride
    tile_in_seq = i % tiles_per_seq

    if tiles_per_seq > 1:
        @pl.when(tile_in_seq != 0)
        def _():
            win_ref[0:halo, :] = win_ref[tm:tm + halo, :]

    @pl.when(tile_in_seq == 0)
    def _():
        win_ref[0:halo, :] = hist_ref[0]

    win_ref[halo:halo + tm, :] = u_ref[...]

    for gi, win in enumerate(POOL_WINDOWS):
        @pl.when(g == gi)
        def _(win=win):
            def body(c, carry):
                r0 = pl.multiple_of(c * chunk_rows, chunk_rows)
                cur = win_ref[pl.ds(halo + r0, chunk_rows), :]
                acc = _window_sum_rows(win_ref, halo, r0, chunk_rows, win, stride)
                t = (tile_in_seq * tm + r0 + lax.broadcasted_iota(jnp.int32, (chunk_rows, 1), 0)) // stride
                cnt = jnp.minimum(win, t + 1 + hist_len).astype(F32)
                m_ref[pl.ds(r0, chunk_rows), :] = (acc / cnt - cur).astype(m_ref.dtype)
                return carry

            lax.fori_loop(0, tm // chunk_rows, body, 0)

    y = jnp.dot(m_ref[...], w_ref[0], preferred_element_type=F32) * scale_ref[...]
    gate = gate_ref[...]
    for s in range(tm // sub):
        rows = slice(s * sub, (s + 1) * sub)
        o_ref[rows, :] = x_ref[rows, :] + gate * y[rows, :]
    hist_out_ref[0] = win_ref[tm + stride:tm + halo, :]


def _pool_call(u, hist, x, mod, layer, w, scale, x_prev, *, row0, n_rows, n_seq, stride, hist_len, mod_seq0,
               lay, tm=1024):
    n, d = x.shape
    sub = lay["sub"]
    n_g = len(POOL_WINDOWS)
    gc = d // n_g
    n_tiles = n_rows // tm
    tps = n_tiles // n_seq
    blk0 = row0 // tm
    halo = 16 * stride
    blocks = (3 * _nbytes((tm, gc), F32) + _nbytes((halo, gc), F32) + _nbytes((sub, gc), F32)
              + _nbytes((gc, gc), BF16) + _nbytes((15 * stride, gc), F32))
    scratch = _nbytes((halo + tm, gc), F32) + _nbytes((tm, gc), BF16)
    gate_col0 = 2 * n_g
    in_specs = [
        pl.BlockSpec((tm, gc), lambda g, i: (blk0 + i, g)),
        pl.BlockSpec((1, halo, gc), lambda g, i: (i // tps, 0, g)),
        pl.BlockSpec((tm, gc), lambda g, i: (blk0 + i, g)),
        pl.BlockSpec((None, sub, gc), lambda g, i: (layer, mod_seq0 + i // tps, gate_col0 + g)),
        pl.BlockSpec((1, gc, gc), lambda g, i: (g, 0, 0)),
        pl.BlockSpec((1, gc), lambda g, i: (0, g)),
    ]
    args = [u, hist, x, mod, w, scale.reshape(1, d)]
    aliases = {}
    if x_prev is not None:
        in_specs.append(pl.BlockSpec(memory_space=pl.ANY))
        args.append(x_prev)
        aliases = {len(args) - 1: 0}
    return pl.pallas_call(
        functools.partial(_pool_kernel, tm=tm, sub=sub, stride=stride, tiles_per_seq=tps,
                          hist_len=hist_len, chunk_rows=64),
        grid=(n_g, n_tiles),
        in_specs=in_specs,
        out_specs=[
            pl.BlockSpec((tm, gc), lambda g, i: (blk0 + i, g)),
            pl.BlockSpec((1, 15 * stride, gc), lambda g, i: (i // tps, 0, g)),
        ],
        out_shape=[jax.ShapeDtypeStruct((n, d), F32),
                   jax.ShapeDtypeStruct((n_seq, 15 * stride, d), F32)],
        scratch_shapes=[pltpu.VMEM((halo + tm, gc), F32), pltpu.VMEM((tm, gc), BF16)],
        input_output_aliases=aliases,
        compiler_params=pltpu.CompilerParams(
            dimension_semantics=("arbitrary", "arbitrary"),
            vmem_limit_bytes=_vmem_limit(blocks, scratch)),
        name="pool_mixer_s%d" % stride,
    )(*args)


def _rope_tables(pos, dk):
    half = dk // 2
    inv = 1.0 / (ROPE_BASE ** (jnp.arange(half, dtype=F32) / half))
    ang = pos[:, None] * inv[None, :]
    return jnp.cos(ang), jnp.sin(ang)


def kernel(x_prompt, x_sample, c_prompt, c_sample, state_ret, state_pool, norm_mix_g, norm_ffn_g, ada_w, ada_b,
           ret_w_in, ret_gn_g, ret_w_out, pool_w, pool_scale, ffn_w_in, ffn_w_out, final_norm_g):
    bp, tp, d = x_prompt.shape
    bs, ts, _ = x_sample.shape
    depth = ada_w.shape[0]
    dk = state_ret.shape[3]
    dv = state_ret.shape[4]
    n_hist = state_pool.shape[2]
    assert bp <= C_PAD and n_hist == max(POOL_WINDOWS) - 1 and depth == 2
    n_p = bp * tp
    n_s = bs * ts
    lay = dict(n_prompt=bp, t_prompt=tp, n_prompt_rows=n_p, n_sample_rows=n_s, sub=bs, t_sample=ts, dk=dk, dv=dv)

    xp = x_prompt.reshape(n_p, d)
    xs = x_sample.transpose(1, 0, 2).reshape(n_s, d)
    c_all = jnp.concatenate([c_prompt, jnp.zeros((C_PAD - bp, d), F32), c_sample], axis=0)
    mod = _ada_call(c_all, ada_w, ada_b, bp, bs)

    cos_p, sin_p = _rope_tables(jnp.arange(tp, dtype=F32), dk)
    cos_s, sin_s = _rope_tables(float(PAST_LEN) + jnp.arange(ts, dtype=F32), dk)
    cos = jnp.concatenate([jnp.tile(cos_p, (bp, 1)), jnp.repeat(cos_s, bs, axis=0)], axis=0)
    sin = jnp.concatenate([jnp.tile(sin_p, (bp, 1)), jnp.repeat(sin_s, bs, axis=0)], axis=0)

    h0 = _norm_call((xp, xs), norm_mix_g[0], mod, 0, 1, 0, lay, BF16)
    qkv, g = _ret_proj_call(h0, ret_w_in[0].astype(BF16), cos, sin, lay)
    gated, s_ret_p = _ret_prompt_call(qkv, g, ret_gn_g[0], lay)
    qkv_s = qkv[n_p:].reshape(ts, bs, -1).transpose(1, 0, 2).astype(F32)
    g_s = g[n_p:].reshape(ts, bs, -1).transpose(1, 0, 2)
    gated_s, s_ret_s = _ret_sample_call(qkv_s, g_s, ret_gn_g[0], state_ret[0], lay)
    gated = lax.dynamic_update_slice(
        gated, gated_s.transpose(1, 0, 2).reshape(n_s, -1).astype(BF16), (n_p, 0))
    x = _out_proj_call(gated, ret_w_out[0].astype(BF16), (xp, xs), mod, 0, 2, lay)
    hmid = _ffn_in_call(x, norm_ffn_g[0], mod, 0, ffn_w_in[0].astype(BF16), lay)
    x = _out_proj_call(hmid, ffn_w_out[0].astype(BF16), (x,), mod, 0, 5, lay)

    u = _norm_call((x,), norm_mix_g[1], mod, 1, 1, 0, lay, F32)
    pw = pool_w[0].astype(BF16)
    hist_p = jnp.zeros((bp, 16, d), F32)
    hist_s = jnp.concatenate([jnp.zeros((1, bs, d), F32), state_pool[0].transpose(1, 0, 2)], axis=0)
    x1, nh_p = _pool_call(u, hist_p, x, mod, 1, pw, pool_scale[0], None, row0=0, n_rows=n_p, n_seq=bp,
                          stride=1, hist_len=0, mod_seq0=0, lay=lay)
    x, nh_s = _pool_call(u, hist_s.reshape(1, 16 * bs, d), x, mod, 1, pw, pool_scale[0], x1, row0=n_p,
                         n_rows=n_s, n_seq=1, stride=bs, hist_len=min(PAST_LEN, n_hist), mod_seq0=bp, lay=lay)
    hmid = _ffn_in_call(x, norm_ffn_g[1], mod, 1, ffn_w_in[1].astype(BF16), lay)
    x = _out_proj_call(hmid, ffn_w_out[1].astype(BF16), (x,), mod, 1, 5, lay)

    y_p, y_s = _norm_call((x,), final_norm_g, None, 0, 0, 0, lay, F32, split_out=True)
    y_prompt = y_p.reshape(bp, tp, d)
    y_sample = y_s.reshape(ts, bs, d).transpose(1, 0, 2)
    state_pool_sample = nh_s.reshape(n_hist, bs, d).transpose(1, 0, 2)
    return (y_prompt, y_sample, s_ret_p[None], nh_p[None], s_ret_s[None], state_pool_sample[None])
```

```python
import functools

import jax
import jax.numpy as jnp
from jax import lax
from jax.experimental import pallas as pl
from jax.experimental.pallas import tpu as pltpu

F32 = jnp.float32
BF16 = jnp.bfloat16

RET_HEADS = 8
RET_CHUNK = 128
ROPE_BASE = 10000.0
POOL_WINDOWS = (2, 4, 8, 16)
N_ADA = 6
EPS = 1e-6
PAST_LEN = 16384

V7X_VMEM_BYTES = 64 * 1024 * 1024
V7X_LANES = 128
V7X_SUBLANES = 8
NORM_ROWS = 16
C_PAD = 8


def _vmem_limit(block_bytes, scratch_bytes=0):
    need = 2 * block_bytes + scratch_bytes + 12 * 1024 * 1024
    return int(min(need, V7X_VMEM_BYTES - 6 * 1024 * 1024))


def _nbytes(shape, dtype):
    n = 1
    for s in shape:
        n *= s
    return n * jnp.dtype(dtype).itemsize


def _silu(x):
    return x / (1.0 + jnp.exp(-x))


def _ada_kernel(c_ref, w_ref, b_ref, o_ref, *, n_prompt, sub):
    cs = _silu(c_ref[...])
    res = jnp.dot(cs.astype(BF16), w_ref[0].astype(BF16), preferred_element_type=F32) + b_ref[0]
    tn = res.shape[-1]
    for b in range(n_prompt):
        o_ref[0, b * sub:(b + 1) * sub, :] = jnp.broadcast_to(res[b:b + 1, :], (sub, tn))
    o_ref[0, n_prompt * sub:, :] = res[C_PAD:C_PAD + sub, :]


def _ada_call(c_all, ada_w, ada_b, n_prompt, sub, tn=1024):
    depth, d, n6 = ada_w.shape
    rows = c_all.shape[0]
    out_rows = (n_prompt + 1) * sub
    blocks = (_nbytes((rows, d), F32) + _nbytes((d, tn), F32) + _nbytes((1, tn), F32)
              + _nbytes((out_rows, tn), F32))
    return pl.pallas_call(
        functools.partial(_ada_kernel, n_prompt=n_prompt, sub=sub),
        grid=(depth, n6 // tn),
        in_specs=[
            pl.BlockSpec((rows, d), lambda l, j: (0, 0)),
            pl.BlockSpec((1, d, tn), lambda l, j: (l, 0, j)),
            pl.BlockSpec((1, 1, tn), lambda l, j: (l, 0, j)),
        ],
        out_specs=pl.BlockSpec((1, out_rows, tn), lambda l, j: (l, 0, j)),
        out_shape=jax.ShapeDtypeStruct((depth, out_rows, n6), F32),
        compiler_params=pltpu.CompilerParams(
            dimension_semantics=("arbitrary", "arbitrary"),
            vmem_limit_bytes=_vmem_limit(blocks, _nbytes((d, tn), BF16))),
        name="ada_mod",
    )(c_all, ada_w, ada_b.reshape(depth, 1, n6))


def _norm_mod_rows(x_ref, gam_ref, sc_ref, sh_ref, out_ref, *, tm, sub):
    gam = gam_ref[...]
    per_sub = sub // NORM_ROWS

    def body(c, carry):
        r0 = pl.multiple_of(c * NORM_ROWS, NORM_ROWS)
        x = x_ref[pl.ds(r0, NORM_ROWS), :]
        hn = (x * lax.rsqrt(jnp.mean(x * x, axis=-1, keepdims=True) + EPS)) * gam
        if sc_ref is not None:
            m0 = pl.multiple_of((c % per_sub) * NORM_ROWS, NORM_ROWS)
            hn = hn * (1.0 + sc_ref[pl.ds(m0, NORM_ROWS), :]) + sh_ref[pl.ds(m0, NORM_ROWS), :]
        out_ref[pl.ds(r0, NORM_ROWS), :] = hn.astype(out_ref.dtype)
        return carry

    lax.fori_loop(0, tm // NORM_ROWS, body, 0, unroll=4)


def _mod_index(i, n_prompt_tiles, tiles_per_seq, n_prompt):
    return jnp.where(i < n_prompt_tiles, i // tiles_per_seq, n_prompt)


def _tiling(lay, tm):
    npt = lay["n_prompt_rows"] // tm
    midx = functools.partial(_mod_index, n_prompt_tiles=npt, tiles_per_seq=lay["t_prompt"] // tm,
                             n_prompt=lay["n_prompt"])
    return npt, midx


def _norm_kernel(*refs, tm, sub, n_in, n_out, has_mod, npt):
    x_refs = refs[:n_in]
    gam_ref = refs[n_in]
    sc_ref, sh_ref = (refs[n_in + 1], refs[n_in + 2]) if has_mod else (None, None)
    o_refs = refs[len(refs) - n_out:]
    if n_in == 1 and n_out == 1:
        _norm_mod_rows(x_refs[0], gam_ref, sc_ref, sh_ref, o_refs[0], tm=tm, sub=sub)
        return
    i = pl.program_id(0)

    @pl.when(i < npt)
    def _():
        _norm_mod_rows(x_refs[0], gam_ref, sc_ref, sh_ref, o_refs[0], tm=tm, sub=sub)

    @pl.when(i >= npt)
    def _():
        _norm_mod_rows(x_refs[-1], gam_ref, sc_ref, sh_ref, o_refs[-1], tm=tm, sub=sub)


def _norm_call(xs, gam, mod, layer, col_sc, col_sh, lay, out_dtype, split_out=False, tm=512):
    d = xs[0].shape[1]
    sub = lay["sub"]
    n_p, n_s = lay["n_prompt_rows"], lay["n_sample_rows"]
    npt, midx = _tiling(lay, tm)
    prompt_blk = lambda i: (jnp.minimum(i, npt - 1), 0)
    sample_blk = lambda i: (jnp.maximum(i - npt, 0), 0)
    whole_blk = lambda i: (i, 0)
    in_specs = ([pl.BlockSpec((tm, d), whole_blk)] if len(xs) == 1
                else [pl.BlockSpec((tm, d), prompt_blk), pl.BlockSpec((tm, d), sample_blk)])
    in_specs.append(pl.BlockSpec((1, d), lambda i: (0, 0)))
    args = list(xs) + [gam.reshape(1, d)]
    if mod is not None:
        in_specs += [pl.BlockSpec((None, sub, d), lambda i: (layer, midx(i), col_sc)),
                     pl.BlockSpec((None, sub, d), lambda i: (layer, midx(i), col_sh))]
        args += [mod, mod]
    if split_out:
        out_specs = [pl.BlockSpec((tm, d), prompt_blk), pl.BlockSpec((tm, d), sample_blk)]
        out_shape = [jax.ShapeDtypeStruct((n_p, d), out_dtype), jax.ShapeDtypeStruct((n_s, d), out_dtype)]
    else:
        out_specs = pl.BlockSpec((tm, d), whole_blk)
        out_shape = jax.ShapeDtypeStruct((n_p + n_s, d), out_dtype)
    n_out = 2 if split_out else 1
    blocks = (len(xs) + n_out) * _nbytes((tm, d), F32) + 2 * _nbytes((sub, d), F32)
    return pl.pallas_call(
        functools.partial(_norm_kernel, tm=tm, sub=sub, n_in=len(xs), n_out=n_out,
                          has_mod=mod is not None, npt=npt),
        grid=((n_p + n_s) // tm,),
        in_specs=in_specs,
        out_specs=out_specs,
        out_shape=out_shape,
        compiler_params=pltpu.CompilerParams(
            dimension_semantics=("arbitrary",), vmem_limit_bytes=_vmem_limit(blocks)),
        name="row_norm",
    )(*args)


def _ret_proj_kernel(h_ref, w_ref, *rest, tn, d_qk, head_dk, k_scale, rope):
    o_ref = rest[-1]
    acc = jnp.dot(h_ref[...], w_ref[...], preferred_element_type=F32)
    if not rope:
        o_ref[...] = acc.astype(o_ref.dtype)
        return
    cos_ref, sin_ref = rest[0], rest[1]
    col0 = pl.program_id(1) * tn
    is_qk = col0 < 2 * d_qk
    scale = jnp.where(jnp.logical_and(is_qk, col0 >= d_qk), k_scale, 1.0).astype(F32)
    cs = jnp.where(is_qk, cos_ref[...], 1.0) * scale
    sn = jnp.where(is_qk, sin_ref[...], 0.0) * scale
    half = head_dk // 2
    for hh in range(tn // head_dk):
        x1 = acc[:, hh * head_dk:hh * head_dk + half]
        x2 = acc[:, hh * head_dk + half:(hh + 1) * head_dk]
        o_ref[:, hh * head_dk:hh * head_dk + half] = (x1 * cs - x2 * sn).astype(o_ref.dtype)
        o_ref[:, hh * head_dk + half:(hh + 1) * head_dk] = (x1 * sn + x2 * cs).astype(o_ref.dtype)


def _ret_proj_call(h, w, col0, n_cols, out_dtype, lay, rope_tables=None, tm=1024, tn=1024):
    n, d = h.shape
    d_qk = RET_HEADS * lay["dk"]
    half = lay["dk"] // 2
    rope = rope_tables is not None
    jblk0 = col0 // tn
    in_specs = [pl.BlockSpec((tm, d), lambda i, j: (i, 0)),
                pl.BlockSpec((d, tn), lambda i, j: (0, jblk0 + j))]
    args = [h, w]
    if rope:
        in_specs += [pl.BlockSpec((tm, half), lambda i, j: (i, 0))] * 2
        args += list(rope_tables)
    blocks = (_nbytes((tm, d), BF16) + _nbytes((d, tn), BF16)
              + 2 * _nbytes((tm, half), F32) + _nbytes((tm, tn), out_dtype))
    return pl.pallas_call(
        functools.partial(_ret_proj_kernel, tn=tn, d_qk=d_qk, head_dk=lay["dk"],
                          k_scale=float(lay["dk"]) ** -0.5, rope=rope),
        grid=(n // tm, n_cols // tn),
        in_specs=in_specs,
        out_specs=pl.BlockSpec((tm, tn), lambda i, j: (i, j)),
        out_shape=jax.ShapeDtypeStruct((n, n_cols), out_dtype),
        compiler_params=pltpu.CompilerParams(
            dimension_semantics=("arbitrary", "arbitrary"),
            vmem_limit_bytes=_vmem_limit(blocks, _nbytes((tm, tn), F32))),
        name="ret_in_proj_qkv" if rope else "ret_in_proj_gate",
    )(*args)


def _ffn_in_kernel(x_ref, gam_ref, sc_ref, sh_ref, wg_ref, wu_ref, o_ref, h_ref, *, tm, sub):
    @pl.when(pl.program_id(1) == 0)
    def _():
        _norm_mod_rows(x_ref, gam_ref, sc_ref, sh_ref, h_ref, tm=tm, sub=sub)

    h = h_ref[...]
    gate = jnp.dot(h, wg_ref[...], preferred_element_type=F32)
    up = jnp.dot(h, wu_ref[...], preferred_element_type=F32)
    o_ref[...] = (_silu(gate) * up).astype(o_ref.dtype)


def _ffn_in_call(x, gam, mod, layer, w, lay, tm=1024, tn=512):
    n, d = x.shape
    sub = lay["sub"]
    d_ff = w.shape[1] // 2
    n_j = d_ff // tn
    _, midx = _tiling(lay, tm)
    blocks = (_nbytes((tm, d), F32) + 2 * _nbytes((sub, d), F32) + 2 * _nbytes((d, tn), BF16)
              + _nbytes((tm, tn), BF16))
    return pl.pallas_call(
        functools.partial(_ffn_in_kernel, tm=tm, sub=sub),
        grid=(n // tm, n_j),
        in_specs=[
            pl.BlockSpec((tm, d), lambda i, j: (i, 0)),
            pl.BlockSpec((1, d), lambda i, j: (0, 0)),
            pl.BlockSpec((None, sub, d), lambda i, j: (layer, midx(i), 4)),
            pl.BlockSpec((None, sub, d), lambda i, j: (layer, midx(i), 3)),
            pl.BlockSpec((d, tn), lambda i, j: (0, j)),
            pl.BlockSpec((d, tn), lambda i, j: (0, j + n_j)),
        ],
        out_specs=pl.BlockSpec((tm, tn), lambda i, j: (i, j)),
        out_shape=jax.ShapeDtypeStruct((n, d_ff), BF16),
        scratch_shapes=[pltpu.VMEM((tm, d), BF16)],
        compiler_params=pltpu.CompilerParams(
            dimension_semantics=("arbitrary", "arbitrary"),
            vmem_limit_bytes=_vmem_limit(blocks, _nbytes((tm, d), BF16) + 2 * _nbytes((tm, tn), F32))),
        name="ffn_in_swiglu",
    )(x, gam.reshape(1, d), mod, mod, w, w)


def _out_proj_kernel(a_ref, w_ref, *refs, tm, sub, npt):
    x_refs, gate_ref, o_ref = refs[:-2], refs[-2], refs[-1]
    acc = jnp.dot(a_ref[...], w_ref[...], preferred_element_type=F32)
    gate = gate_ref[...]
    is_prompt = pl.program_id(0) < npt
    for s in range(tm // sub):
        rows = slice(s * sub, (s + 1) * sub)
        x = x_refs[0][rows, :]
        if len(x_refs) == 2:
            x = jnp.where(is_prompt, x, x_refs[1][rows, :])
        o_ref[rows, :] = x + gate * acc[rows, :]


def _out_proj_call(a, w, xs, mod, layer, gate_col, lay, tm=1024, tn=512):
    n, k = a.shape
    d = w.shape[1]
    sub = lay["sub"]
    npt, midx = _tiling(lay, tm)
    n_j = d // tn
    x_specs = ([pl.BlockSpec((tm, tn), lambda i, j: (i, j))] if len(xs) == 1
               else [pl.BlockSpec((tm, tn), lambda i, j: (jnp.minimum(i, npt - 1), j)),
                     pl.BlockSpec((tm, tn), lambda i, j: (jnp.maximum(i - npt, 0), j))])
    blocks = (_nbytes((tm, k), BF16) + _nbytes((k, tn), BF16) + (1 + len(xs)) * _nbytes((tm, tn), F32)
              + _nbytes((sub, tn), F32))
    return pl.pallas_call(
        functools.partial(_out_proj_kernel, tm=tm, sub=sub, npt=npt),
        grid=(n // tm, n_j),
        in_specs=[pl.BlockSpec((tm, k), lambda i, j: (i, 0)),
                  pl.BlockSpec((k, tn), lambda i, j: (0, j))] + x_specs
                 + [pl.BlockSpec((None, sub, tn), lambda i, j: (layer, midx(i), gate_col * n_j + j))],
        out_specs=pl.BlockSpec((tm, tn), lambda i, j: (i, j)),
        out_shape=jax.ShapeDtypeStruct((n, d), F32),
        compiler_params=pltpu.CompilerParams(
            dimension_semantics=("arbitrary", "arbitrary"),
            vmem_limit_bytes=_vmem_limit(blocks, _nbytes((tm, tn), F32))),
        name="out_proj_residual",
    )(a, w, *xs, mod)


def _retention_decays(c):
    lg = jnp.log1p(-jnp.exp2(-5.0 - jnp.arange(RET_HEADS, dtype=F32)))
    idx = jnp.arange(c, dtype=F32)
    diff = idx[:, None] - idx[None, :]
    causal = diff >= 0
    dmask = jnp.where(causal[None], jnp.exp(jnp.where(causal, diff, 0.0)[None] * lg[:, None, None]), 0.0)
    qdec = jnp.exp((idx + 1.0)[None, :] * lg[:, None])
    kdec = jnp.exp((c - 1.0 - idx)[None, :] * lg[:, None])
    sdec = jnp.exp(c * lg)
    return dmask, qdec[:, :, None], kdec[:, :, None], sdec[:, None, None]


def _group_norm_gate(o, g, gng):
    mu = jnp.mean(o, axis=-1, keepdims=True)
    dlt = o - mu
    var = jnp.mean(dlt * dlt, axis=-1, keepdims=True)
    return _silu(g) * (dlt * lax.rsqrt(var + EPS) * gng)


_NT = (((1,), (1,)), ((), ()))
_TN = (((0,), (0,)), ((), ()))


def _retention_chunk(q, k, v, s_prev, dmask, qdec, kdec, sdec):
    scores = lax.dot_general(q, k, _NT, preferred_element_type=F32) * dmask
    o = (jnp.dot(scores.astype(BF16), v, preferred_element_type=F32)
         + qdec * jnp.dot(q, s_prev.astype(BF16), preferred_element_type=F32))
    kd = (k.astype(F32) * kdec).astype(BF16)
    s_new = sdec * s_prev + lax.dot_general(kd, v, _TN, preferred_element_type=F32)
    return o, s_new


def _ret_prompt_kernel(q_ref, k_ref, v_ref, g_ref, gng_ref, dmask_ref, qdec_ref, kdec_ref, sdec_ref,
                       o_ref, s_out_ref, s_ref, *, chunk, n_chunks):
    s_ref[...] = jnp.zeros_like(s_ref)
    dmask = dmask_ref[0]
    qdec = qdec_ref[0]
    kdec = kdec_ref[0]
    sdec = sdec_ref[0]
    gng = gng_ref[...]

    def body(c, carry):
        rows = pl.ds(pl.multiple_of(c * chunk, chunk), chunk)
        o, s_new = _retention_chunk(q_ref[rows, :], k_ref[rows, :], v_ref[rows, :], s_ref[...],
                                    dmask, qdec, kdec, sdec)
        s_ref[...] = s_new
        o_ref[rows, :] = _group_norm_gate(o, g_ref[rows, :], gng).astype(o_ref.dtype)
        return carry

    lax.fori_loop(0, n_chunks, body, 0, unroll=2)
    s_out_ref[0, 0] = s_ref[...]


def _ret_prompt_call(qkv, g, gng, lay, chunk=RET_CHUNK):
    n = qkv.shape[0]
    bp, tp, dk, dv = lay["n_prompt"], lay["t_prompt"], lay["dk"], lay["dv"]
    h = RET_HEADS
    dmask, qdec, kdec, sdec = _retention_decays(chunk)
    v_blk0 = 2 * h * dk // dv
    blocks = (2 * _nbytes((tp, dk), BF16) + _nbytes((tp, dv), BF16) + _nbytes((tp, dv), F32)
              + _nbytes((tp, dv), BF16) + 2 * _nbytes((dk, dv), F32) + _nbytes((chunk, chunk), F32))
    return pl.pallas_call(
        functools.partial(_ret_prompt_kernel, chunk=chunk, n_chunks=tp // chunk),
        grid=(bp, h),
        in_specs=[
            pl.BlockSpec((tp, dk), lambda b, hh: (b, hh)),
            pl.BlockSpec((tp, dk), lambda b, hh: (b, h + hh)),
            pl.BlockSpec((tp, dv), lambda b, hh: (b, v_blk0 + hh)),
            pl.BlockSpec((tp, dv), lambda b, hh: (b, hh)),
            pl.BlockSpec((1, dv), lambda b, hh: (0, hh)),
            pl.BlockSpec((1, chunk, chunk), lambda b, hh: (hh, 0, 0)),
            pl.BlockSpec((1, chunk, 1), lambda b, hh: (hh, 0, 0)),
            pl.BlockSpec((1, chunk, 1), lambda b, hh: (hh, 0, 0)),
            pl.BlockSpec((1, 1, 1), lambda b, hh: (hh, 0, 0)),
        ],
        out_specs=[
            pl.BlockSpec((tp, dv), lambda b, hh: (b, hh)),
            pl.BlockSpec((1, 1, dk, dv), lambda b, hh: (b, hh, 0, 0)),
        ],
        out_shape=[jax.ShapeDtypeStruct((n, h * dv), BF16),
                   jax.ShapeDtypeStruct((bp, h, dk, dv), F32)],
        scratch_shapes=[pltpu.VMEM((dk, dv), F32)],
        compiler_params=pltpu.CompilerParams(
            dimension_semantics=("arbitrary", "arbitrary"),
            vmem_limit_bytes=_vmem_limit(blocks, _nbytes((dk, dv), F32))),
        name="retention_prompt",
    )(qkv, qkv, qkv, g, gng.reshape(1, h * dv), dmask, qdec, kdec, sdec)


def _ret_sample_kernel(q_ref, k_ref, v_ref, g_ref, gng_ref, s0_ref, dmask_ref, qdec_ref, kdec_ref, sdec_ref,
                       prev_ref, o_ref, s_out_ref, qf_ref, kf_ref, vf_ref, of_ref, *, bb):
    del prev_ref
    dmask = dmask_ref[0]
    qdec = qdec_ref[0]
    kdec = kdec_ref[0]
    sdec = sdec_ref[0]
    gng = gng_ref[...]
    qf_ref[...] = q_ref[...].astype(F32)
    kf_ref[...] = k_ref[...].astype(F32)
    vf_ref[...] = v_ref[...].astype(F32)

    def body(b, carry):
        o, s_new = _retention_chunk(qf_ref[:, b, :].astype(BF16), kf_ref[:, b, :].astype(BF16),
                                    vf_ref[:, b, :].astype(BF16), s0_ref[b, 0], dmask, qdec, kdec, sdec)
        s_out_ref[b, 0] = s_new
        of_ref[:, b, :] = _group_norm_gate(o, g_ref[:, b, :], gng)
        return carry

    lax.fori_loop(0, bb, body, 0, unroll=4)
    o_ref[...] = of_ref[...].astype(o_ref.dtype)


def _ret_sample_call(qkv, g, gng, state, gated, lay, bb=16):
    bs, ts, dk, dv = lay["sub"], lay["t_sample"], lay["dk"], lay["dv"]
    h = RET_HEADS
    n = qkv.shape[0]
    t_blk = lay["n_prompt_rows"] // (bs * ts)
    assert t_blk * bs * ts == lay["n_prompt_rows"]
    dmask, qdec, kdec, sdec = _retention_decays(ts)
    v_blk0 = 2 * h * dk // dv
    blocks = (2 * _nbytes((ts, bb, dk), BF16) + 2 * _nbytes((ts, bb, dv), BF16) + _nbytes((ts, bb, dv), F32)
              + 2 * _nbytes((bb, dk, dv), F32))
    scratch = 2 * _nbytes((ts, bb, dk), F32) + 2 * _nbytes((ts, bb, dv), F32)
    qkv3 = qkv.reshape(n // bs, bs, qkv.shape[1])
    gated3, s_new = pl.pallas_call(
        functools.partial(_ret_sample_kernel, bb=bb),
        grid=(bs // bb, h),
        in_specs=[
            pl.BlockSpec((ts, bb, dk), lambda b, hh: (t_blk, b, hh)),
            pl.BlockSpec((ts, bb, dk), lambda b, hh: (t_blk, b, h + hh)),
            pl.BlockSpec((ts, bb, dv), lambda b, hh: (t_blk, b, v_blk0 + hh)),
            pl.BlockSpec((ts, bb, dv), lambda b, hh: (t_blk, b, hh)),
            pl.BlockSpec((1, dv), lambda b, hh: (0, hh)),
            pl.BlockSpec((bb, 1, dk, dv), lambda b, hh: (b, hh, 0, 0)),
            pl.BlockSpec((1, ts, ts), lambda b, hh: (hh, 0, 0)),
            pl.BlockSpec((1, ts, 1), lambda b, hh: (hh, 0, 0)),
            pl.BlockSpec((1, ts, 1), lambda b, hh: (hh, 0, 0)),
            pl.BlockSpec((1, 1, 1), lambda b, hh: (hh, 0, 0)),
            pl.BlockSpec(memory_space=pl.ANY),
        ],
        out_specs=[
            pl.BlockSpec((ts, bb, dv), lambda b, hh: (t_blk, b, hh)),
            pl.BlockSpec((bb, 1, dk, dv), lambda b, hh: (b, hh, 0, 0)),
        ],
        out_shape=[jax.ShapeDtypeStruct((n // bs, bs, h * dv), gated.dtype),
                   jax.ShapeDtypeStruct((bs, h, dk, dv), F32)],
        scratch_shapes=[pltpu.VMEM((ts, bb, dk), F32), pltpu.VMEM((ts, bb, dk), F32),
                        pltpu.VMEM((ts, bb, dv), F32), pltpu.VMEM((ts, bb, dv), F32)],
        input_output_aliases={10: 0},
        compiler_params=pltpu.CompilerParams(
            dimension_semantics=("arbitrary", "arbitrary"),
            vmem_limit_bytes=_vmem_limit(blocks, scratch)),
        name="retention_sample",
    )(qkv3, qkv3, qkv3, g.reshape(n // bs, bs, h * dv), gng.reshape(1, h * dv), state, dmask, qdec, kdec, sdec,
      gated.reshape(n // bs, bs, h * dv))
    return gated3.reshape(n, h * dv), s_new


def _window_sum_rows(win_ref, halo, r0, rows, win, stride):
    if stride % V7X_SUBLANES == 0:
        acc = win_ref[pl.ds(halo + r0, rows), :]
        for k in range(1, win):
            acc = acc + win_ref[pl.ds(halo - k * stride + r0, rows), :]
        return acc
    assert stride == 1 and halo == 16 and win <= 16
    blk = win_ref[pl.ds(r0, rows + 16), :]
    shift = 1
    while shift < win:
        blk = blk + pltpu.roll(blk, shift, 0)
        shift *= 2
    return blk[16:, :]


def _pool_kernel(u_ref, hist_ref, x_ref, gate_ref, w_ref, scale_ref, *rest,
                 tm, sub, stride, tiles_per_seq, hist_len, chunk_rows):
    o_ref, hist_out_ref, win_ref, m_ref = rest[-4:]
    g = pl.program_id(0)
    i = pl.program_id(1)
    halo = 16 * stride
    tile_in_seq = i % tiles_per_seq

    if tiles_per_seq > 1:
        @pl.when(tile_in_seq != 0)
        def _():
            win_ref[0:halo, :] = win_ref[tm:tm + halo, :]

    @pl.when(tile_in_seq == 0)
    def _():
        win_ref[0:halo, :] = hist_ref[0]

    win_ref[halo:halo + tm, :] = u_ref[...]

    for gi, win in enumerate(POOL_WINDOWS):
        @pl.when(g == gi)
        def _(win=win):
            def body(c, carry):
                r0 = pl.multiple_of(c * chunk_rows, chunk_rows)
                cur = win_ref[pl.ds(halo + r0, chunk_rows), :]
                acc = _window_sum_rows(win_ref, halo, r0, chunk_rows, win, stride)
                t = (tile_in_seq * tm + r0 + lax.broadcasted_iota(jnp.int32, (chunk_rows, 1), 0)) // stride
                cnt = jnp.minimum(win, t + 1 + hist_len).astype(F32)
                m_ref[pl.ds(r0, chunk_rows), :] = (acc / cnt - cur).astype(m_ref.dtype)
                return carry

            lax.fori_loop(0, tm // chunk_rows, body, 0)

    y = jnp.dot(m_ref[...], w_ref[0], preferred_element_type=F32) * scale_ref[...]
    gate = gate_ref[...]
    for s in range(tm // sub):
        rows = slice(s * sub, (s + 1) * sub)
        o_ref[rows, :] = x_ref[rows, :] + gate * y[rows, :]
    hist_out_ref[0] = win_ref[tm + stride:tm + halo, :]


def _pool_call(u, hist, x, mod, layer, w, scale, x_prev, *, row0, n_rows, n_seq, stride, hist_len, mod_seq0,
               lay, tm=1024):
    n, d = x.shape
    sub = lay["sub"]
    n_g = len(POOL_WINDOWS)
    gc = d // n_g
    n_tiles = n_rows // tm
    tps = n_tiles // n_seq
    blk0 = row0 // tm
    halo = 16 * stride
    blocks = (3 * _nbytes((tm, gc), F32) + _nbytes((halo, gc), F32) + _nbytes((sub, gc), F32)
              + _nbytes((gc, gc), BF16) + _nbytes((15 * stride, gc), F32))
    scratch = _nbytes((halo + tm, gc), F32) + _nbytes((tm, gc), BF16)
    gate_col0 = 2 * n_g
    in_specs = [
        pl.BlockSpec((tm, gc), lambda g, i: (blk0 + i, g)),
        pl.BlockSpec((1, halo, gc), lambda g, i: (i // tps, 0, g)),
        pl.BlockSpec((tm, gc), lambda g, i: (blk0 + i, g)),
        pl.BlockSpec((None, sub, gc), lambda g, i: (layer, mod_seq0 + i // tps, gate_col0 + g)),
        pl.BlockSpec((1, gc, gc), lambda g, i: (g, 0, 0)),
        pl.BlockSpec((1, gc), lambda g, i: (0, g)),
    ]
    args = [u, hist, x, mod, w, scale.reshape(1, d)]
    aliases = {}
    if x_prev is not None:
        in_specs.append(pl.BlockSpec(memory_space=pl.ANY))
        args.append(x_prev)
        aliases = {len(args) - 1: 0}
    return pl.pallas_call(
        functools.partial(_pool_kernel, tm=tm, sub=sub, stride=stride, tiles_per_seq=tps,
                          hist_len=hist_len, chunk_rows=64),
        grid=(n_g, n_tiles),
        in_specs=in_specs,
        out_specs=[
            pl.BlockSpec((tm, gc), lambda g, i: (blk0 + i, g)),
            pl.BlockSpec((1, 15 * stride, gc), lambda g, i: (i // tps, 0, g)),
        ],
        out_shape=[jax.ShapeDtypeStruct((n, d), F32),
                   jax.ShapeDtypeStruct((n_seq, 15 * stride, d), F32)],
        scratch_shapes=[pltpu.VMEM((halo + tm, gc), F32), pltpu.VMEM((tm, gc), BF16)],
        input_output_aliases=aliases,
        compiler_params=pltpu.CompilerParams(
            dimension_semantics=("arbitrary", "arbitrary"),
            vmem_limit_bytes=_vmem_limit(blocks, scratch)),
        name="pool_mixer_s%d" % stride,
    )(*args)


def _rope_tables(pos, dk):
    half = dk // 2
    inv = 1.0 / (ROPE_BASE ** (jnp.arange(half, dtype=F32) / half))
    ang = pos[:, None] * inv[None, :]
    return jnp.cos(ang), jnp.sin(ang)


def kernel(x_prompt, x_sample, c_prompt, c_sample, state_ret, state_pool, norm_mix_g, norm_ffn_g, ada_w, ada_b,
           ret_w_in, ret_gn_g, ret_w_out, pool_w, pool_scale, ffn_w_in, ffn_w_out, final_norm_g):
    bp, tp, d = x_prompt.shape
    bs, ts, _ = x_sample.shape
    depth = ada_w.shape[0]
    dk = state_ret.shape[3]
    dv = state_ret.shape[4]
    n_hist = state_pool.shape[2]
    assert bp <= C_PAD and n_hist == max(POOL_WINDOWS) - 1 and depth == 2
    n_p = bp * tp
    n_s = bs * ts
    lay = dict(n_prompt=bp, t_prompt=tp, n_prompt_rows=n_p, n_sample_rows=n_s, sub=bs, t_sample=ts, dk=dk, dv=dv)

    xp = x_prompt.reshape(n_p, d)
    xs = x_sample.transpose(1, 0, 2).reshape(n_s, d)
    c_all = jnp.concatenate([c_prompt, jnp.zeros((C_PAD - bp, d), F32), c_sample], axis=0)
    mod = _ada_call(c_all, ada_w, ada_b, bp, bs)

    cos_p, sin_p = _rope_tables(jnp.arange(tp, dtype=F32), dk)
    cos_s, sin_s = _rope_tables(float(PAST_LEN) + jnp.arange(ts, dtype=F32), dk)
    cos = jnp.concatenate([jnp.tile(cos_p, (bp, 1)), jnp.repeat(cos_s, bs, axis=0)], axis=0)
    sin = jnp.concatenate([jnp.tile(sin_p, (bp, 1)), jnp.repeat(sin_s, bs, axis=0)], axis=0)

    h0 = _norm_call((xp, xs), norm_mix_g[0], mod, 0, 1, 0, lay, BF16)
    w_in = ret_w_in[0].astype(BF16)
    n_qkv = RET_HEADS * (2 * dk + dv)
    qkv = _ret_proj_call(h0, w_in, 0, n_qkv, BF16, lay, rope_tables=(cos, sin))
    g = _ret_proj_call(h0, w_in, n_qkv, RET_HEADS * dv, F32, lay)
    gated, s_ret_p = _ret_prompt_call(qkv, g, ret_gn_g[0], lay)
    gated, s_ret_s = _ret_sample_call(qkv, g, ret_gn_g[0], state_ret[0], gated, lay)
    x =_out_proj_call(gated, ret_w_out[0].astype(BF16), (xp, xs), mod, 0, 2, lay)
    hmid = _ffn_in_call(x, norm_ffn_g[0], mod, 0, ffn_w_in[0].astype(BF16), lay)
    x = _out_proj_call(hmid, ffn_w_out[0].astype(BF16), (x,), mod, 0, 5, lay)

    u = _norm_call((x,), norm_mix_g[1], mod, 1, 1, 0, lay, F32)
    pw = pool_w[0].astype(BF16)
    hist_p = jnp.zeros((bp, 16, d), F32)
    hist_s = jnp.concatenate([jnp.zeros((1, bs, d), F32), state_pool[0].transpose(1, 0, 2)], axis=0)
    x1, nh_p = _pool_call(u, hist_p, x, mod, 1, pw, pool_scale[0], None, row0=0, n_rows=n_p, n_seq=bp,
                          stride=1, hist_len=0, mod_seq0=0, lay=lay)
    x, nh_s = _pool_call(u, hist_s.reshape(1, 16 * bs, d), x, mod, 1, pw, pool_scale[0], x1, row0=n_p,
                         n_rows=n_s, n_seq=1, stride=bs, hist_len=min(PAST_LEN, n_hist), mod_seq0=bp, lay=lay)
    hmid = _ffn_in_call(x, norm_ffn_g[1], mod, 1, ffn_w_in[1].astype(BF16), lay)
    x = _out_proj_call(hmid, ffn_w_out[1].astype(BF16), (x,), mod, 1, 5, lay)

    y_p, y_s = _norm_call((x,), final_norm_g, None, 0, 0, 0, lay, F32, split_out=True)
    y_prompt = y_p.reshape(bp, tp, d)
    y_sample = y_s.reshape(ts, bs, d).transpose(1, 0, 2)
    state_pool_sample = nh_s.reshape(n_hist, bs, d).transpose(1, 0, 2)
    return (y_prompt, y_sample, s_ret_p[None], nh_p[None], s_ret_s[None], state_pool_sample[None])
```

```python
import functools

import jax
import jax.numpy as jnp
from jax import lax
from jax.experimental import pallas as pl
from jax.experimental.pallas import tpu as pltpu

F32 = jnp.float32
BF16 = jnp.bfloat16

RET_HEADS = 8
RET_CHUNK = 128
ROPE_BASE = 10000.0
POOL_WINDOWS = (2, 4, 8, 16)
N_ADA = 6
EPS = 1e-6
PAST_LEN = 16384

V7X_VMEM_BYTES = 64 * 1024 * 1024
V7X_LANES = 128
V7X_SUBLANES = 8
NORM_ROWS = 16
C_PAD = 8


def _vmem_limit(block_bytes, scratch_bytes=0):
    need = 2 * block_bytes + scratch_bytes + 12 * 1024 * 1024
    return int(min(need, V7X_VMEM_BYTES - 6 * 1024 * 1024))


def _nbytes(shape, dtype):
    n = 1
    for s in shape:
        n *= s
    return n * jnp.dtype(dtype).itemsize


def _silu(x):
    return x / (1.0 + jnp.exp(-x))


def _ada_kernel(c_ref, w_ref, b_ref, o_ref, *, n_prompt, sub):
    cs = _silu(c_ref[...])
    res = jnp.dot(cs.astype(BF16), w_ref[0].astype(BF16), preferred_element_type=F32) + b_ref[0]
    tn = res.shape[-1]
    for b in range(n_prompt):
        o_ref[0, b * sub:(b + 1) * sub, :] = jnp.broadcast_to(res[b:b + 1, :], (sub, tn))
    o_ref[0, n_prompt * sub:, :] = res[C_PAD:C_PAD + sub, :]


def _ada_call(c_all, ada_w, ada_b, n_prompt, sub, tn=1024):
    depth, d, n6 = ada_w.shape
    rows = c_all.shape[0]
    out_rows = (n_prompt + 1) * sub
    blocks = (_nbytes((rows, d), F32) + _nbytes((d, tn), F32) + _nbytes((1, tn), F32)
              + _nbytes((out_rows, tn), F32))
    return pl.pallas_call(
        functools.partial(_ada_kernel, n_prompt=n_prompt, sub=sub),
        grid=(depth, n6 // tn),
        in_specs=[
            pl.BlockSpec((rows, d), lambda l, j: (0, 0)),
            pl.BlockSpec((1, d, tn), lambda l, j: (l, 0, j)),
            pl.BlockSpec((1, 1, tn), lambda l, j: (l, 0, j)),
        ],
        out_specs=pl.BlockSpec((1, out_rows, tn), lambda l, j: (l, 0, j)),
        out_shape=jax.ShapeDtypeStruct((depth, out_rows, n6), F32),
        compiler_params=pltpu.CompilerParams(
            dimension_semantics=("arbitrary", "arbitrary"),
            vmem_limit_bytes=_vmem_limit(blocks, _nbytes((d, tn), BF16))),
        name="ada_mod",
    )(c_all, ada_w, ada_b.reshape(depth, 1, n6))


def _norm_mod_rows(x_ref, gam_ref, sc_ref, sh_ref, out_ref, *, tm, sub):
    gam = gam_ref[...]
    per_sub = sub // NORM_ROWS

    def body(c, carry):
        r0 = pl.multiple_of(c * NORM_ROWS, NORM_ROWS)
        x = x_ref[pl.ds(r0, NORM_ROWS), :]
        hn = (x * lax.rsqrt(jnp.mean(x * x, axis=-1, keepdims=True) + EPS)) * gam
        if sc_ref is not None:
            m0 = pl.multiple_of((c % per_sub) * NORM_ROWS, NORM_ROWS)
            hn = hn * (1.0 + sc_ref[pl.ds(m0, NORM_ROWS), :]) + sh_ref[pl.ds(m0, NORM_ROWS), :]
        out_ref[pl.ds(r0, NORM_ROWS), :] = hn.astype(out_ref.dtype)
        return carry

    lax.fori_loop(0, tm // NORM_ROWS, body, 0, unroll=4)


def _mod_index(i, n_prompt_tiles, tiles_per_seq, n_prompt):
    return jnp.where(i < n_prompt_tiles, i // tiles_per_seq, n_prompt)


def _tiling(lay, tm):
    npt = lay["n_prompt_rows"] // tm
    midx = functools.partial(_mod_index, n_prompt_tiles=npt, tiles_per_seq=lay["t_prompt"] // tm,
                             n_prompt=lay["n_prompt"])
    return npt, midx


def _norm_kernel(*refs, tm, sub, n_in, n_out, has_mod, npt):
    x_refs = refs[:n_in]
    gam_ref = refs[n_in]
    sc_ref, sh_ref = (refs[n_in + 1], refs[n_in + 2]) if has_mod else (None, None)
    o_refs = refs[len(refs) - n_out:]
    if n_in == 1 and n_out == 1:
        _norm_mod_rows(x_refs[0], gam_ref, sc_ref, sh_ref, o_refs[0], tm=tm, sub=sub)
        return
    i = pl.program_id(0)

    @pl.when(i < npt)
    def _():
        _norm_mod_rows(x_refs[0], gam_ref, sc_ref, sh_ref, o_refs[0], tm=tm, sub=sub)

    @pl.when(i >= npt)
    def _():
        _norm_mod_rows(x_refs[-1], gam_ref, sc_ref, sh_ref, o_refs[-1], tm=tm, sub=sub)


def _norm_call(xs, gam, mod, layer, col_sc, col_sh, lay, out_dtype, split_out=False, tm=512):
    d = xs[0].shape[1]
    sub = lay["sub"]
    n_p, n_s = lay["n_prompt_rows"], lay["n_sample_rows"]
    npt, midx = _tiling(lay, tm)
    prompt_blk = lambda i: (jnp.minimum(i, npt - 1), 0)
    sample_blk = lambda i: (jnp.maximum(i - npt, 0), 0)
    whole_blk = lambda i: (i, 0)
    in_specs = ([pl.BlockSpec((tm, d), whole_blk)] if len(xs) == 1
                else [pl.BlockSpec((tm, d), prompt_blk), pl.BlockSpec((tm, d), sample_blk)])
    in_specs.append(pl.BlockSpec((1, d), lambda i: (0, 0)))
    args = list(xs) + [gam.reshape(1, d)]
    if mod is not None:
        in_specs += [pl.BlockSpec((None, sub, d), lambda i: (layer, midx(i), col_sc)),
                     pl.BlockSpec((None, sub, d), lambda i: (layer, midx(i), col_sh))]
        args += [mod, mod]
    if split_out:
        out_specs = [pl.BlockSpec((tm, d), prompt_blk), pl.BlockSpec((tm, d), sample_blk)]
        out_shape = [jax.ShapeDtypeStruct((n_p, d), out_dtype), jax.ShapeDtypeStruct((n_s, d), out_dtype)]
    else:
        out_specs = pl.BlockSpec((tm, d), whole_blk)
        out_shape = jax.ShapeDtypeStruct((n_p + n_s, d), out_dtype)
    n_out = 2 if split_out else 1
    blocks = (len(xs) + n_out) * _nbytes((tm, d), F32) + 2 * _nbytes((sub, d), F32)
    return pl.pallas_call(
        functools.partial(_norm_kernel, tm=tm, sub=sub, n_in=len(xs), n_out=n_out,
                          has_mod=mod is not None, npt=npt),
        grid=((n_p + n_s) // tm,),
        in_specs=in_specs,
        out_specs=out_specs,
        out_shape=out_shape,
        compiler_params=pltpu.CompilerParams(
            dimension_semantics=("arbitrary",), vmem_limit_bytes=_vmem_limit(blocks)),
        name="row_norm",
    )(*args)


def _load_weight(w_ref, copy_ref):
    w = w_ref[...]
    if copy_ref is not None:
        w = w.astype(BF16)
        copy_ref[...] = w
    return w


def _first_tile_then_rest(run, n_tiles, tn, tn_first, w_f32):
    out0, *w_bf = run(0, 1, tn_first, w_f32, True, None)
    return run(1, n_tiles - 1, tn, w_bf, False, out0)[0]


def _ret_proj_kernel(h_ref, w_ref, *rest, tn, d_qk, head_dk, k_scale, rope, cast_w, has_prev):
    n_in = (2 if rope else 0) + (1 if has_prev else 0)
    o_ref = rest[n_in]
    w = _load_weight(w_ref, rest[n_in + 1] if cast_w else None)
    acc = jnp.dot(h_ref[...], w, preferred_element_type=F32)
    if not rope:
        o_ref[...] = acc.astype(o_ref.dtype)
        return
    cos_ref, sin_ref = rest[0], rest[1]
    col0 = pl.program_id(1) * tn
    is_qk = col0 < 2 * d_qk
    scale = jnp.where(jnp.logical_and(is_qk, col0 >= d_qk), k_scale, 1.0).astype(F32)
    cs = jnp.where(is_qk, cos_ref[...], 1.0) * scale
    sn = jnp.where(is_qk, sin_ref[...], 0.0) * scale
    half = head_dk // 2
    for hh in range(tn // head_dk):
        x1 = acc[:, hh * head_dk:hh * head_dk + half]
        x2 = acc[:, hh * head_dk + half:(hh + 1) * head_dk]
        o_ref[:, hh * head_dk:hh * head_dk + half] = (x1 * cs - x2 * sn).astype(o_ref.dtype)
        o_ref[:, hh * head_dk + half:(hh + 1) * head_dk] = (x1 * sn + x2 * cs).astype(o_ref.dtype)


def _ret_proj_call(h, w3, w_layer, col0, n_cols, out_dtype, lay, rope_tables=None, tm=1024, tn=1024, tn_first=512):
    n, d = h.shape
    d_qk = RET_HEADS * lay["dk"]
    half = lay["dk"] // 2
    rope = rope_tables is not None

    def run(i0, n_i, tn_, weights, cast_w, prev):
        jblk0 = col0 // tn_
        w_spec = (pl.BlockSpec((None, d, tn_), lambda i, j: (w_layer, 0, jblk0 + j)) if cast_w
                  else pl.BlockSpec((d, tn_), lambda i, j: (0, j)))
        in_specs = [pl.BlockSpec((tm, d), lambda i, j: (i + i0, 0)), w_spec]
        args = [h, weights if cast_w else weights[0]]
        if rope:
            in_specs += [pl.BlockSpec((tm, half), lambda i, j: (i + i0, 0))] * 2
            args += list(rope_tables)
        aliases = {}
        if prev is not None:
            in_specs.append(pl.BlockSpec(memory_space=pl.ANY))
            args.append(prev)
            aliases = {len(args) - 1: 0}
        out_specs = [pl.BlockSpec((tm, tn_), lambda i, j: (i + i0, j))]
        out_shape = [jax.ShapeDtypeStruct((n, n_cols), out_dtype)]
        w_bytes = _nbytes((d, tn_), BF16)
        if cast_w:
            out_specs.append(pl.BlockSpec((d, tn_), lambda i, j: (0, j)))
            out_shape.append(jax.ShapeDtypeStruct((d, n_cols), BF16))
            w_bytes = _nbytes((d, tn_), F32) + 2 * _nbytes((d, tn_), BF16)
        blocks = (_nbytes((tm, d), BF16) + w_bytes + 2 * _nbytes((tm, half), F32) + _nbytes((tm, tn_), out_dtype))
        return pl.pallas_call(
            functools.partial(_ret_proj_kernel, tn=tn_, d_qk=d_qk, head_dk=lay["dk"],
                              k_scale=float(lay["dk"]) ** -0.5, rope=rope, cast_w=cast_w,
                              has_prev=prev is not None),
            grid=(n_i, n_cols // tn_),
            in_specs=in_specs,
            out_specs=out_specs,
            out_shape=out_shape,
            input_output_aliases=aliases,
            compiler_params=pltpu.CompilerParams(
                dimension_semantics=("arbitrary", "arbitrary"),
                vmem_limit_bytes=_vmem_limit(blocks, _nbytes((tm, tn_), F32))),
            name=("ret_in_proj_qkv" if rope else "ret_in_proj_gate") + ("_first" if cast_w else ""),
        )(*args)

    return _first_tile_then_rest(run, n // tm, tn, tn_first, w3)


def _ffn_in_kernel(x_ref, gam_ref, sc_ref, sh_ref, wg_ref, wu_ref, *rest, tm, sub, cast_w, has_prev):
    outs = rest[1:] if has_prev else rest
    o_ref, h_ref = outs[0], outs[-1]

    @pl.when(pl.program_id(1) == 0)
    def _():
        _norm_mod_rows(x_ref, gam_ref, sc_ref, sh_ref, h_ref, tm=tm, sub=sub)

    h = h_ref[...]
    gate = jnp.dot(h, _load_weight(wg_ref, outs[1] if cast_w else None), preferred_element_type=F32)
    up = jnp.dot(h, _load_weight(wu_ref, outs[2] if cast_w else None), preferred_element_type=F32)
    o_ref[...] = (_silu(gate) * up).astype(o_ref.dtype)


def _ffn_in_call(x, gam, mod, layer, w3, lay, tm=1024, tn=512, tn_first=256):
    n, d = x.shape
    sub = lay["sub"]
    d_ff = w3.shape[2] // 2
    _, midx = _tiling(lay, tm)

    def run(i0, n_i, tn_, weights, cast_w, prev):
        n_j = d_ff // tn_
        if cast_w:
            w_specs = [pl.BlockSpec((None, d, tn_), lambda i, j: (layer, 0, j)),
                       pl.BlockSpec((None, d, tn_), lambda i, j: (layer, 0, j + n_j))]
            w_args = [weights, weights]
            w_bytes = 2 * _nbytes((d, tn_), F32) + 4 * _nbytes((d, tn_), BF16)
        else:
            w_specs = [pl.BlockSpec((d, tn_), lambda i, j: (0, j))] * 2
            w_args = list(weights)
            w_bytes = 2 * _nbytes((d, tn_), BF16)
        in_specs = [
            pl.BlockSpec((tm, d), lambda i, j: (i + i0, 0)),
            pl.BlockSpec((1, d), lambda i, j: (0, 0)),
            pl.BlockSpec((None, sub, d), lambda i, j: (layer, midx(i + i0), 4)),
            pl.BlockSpec((None, sub, d), lambda i, j: (layer, midx(i + i0), 3)),
        ] + w_specs
        args = [x, gam.reshape(1, d), mod, mod] + w_args
        aliases = {}
        if prev is not None:
            in_specs.append(pl.BlockSpec(memory_space=pl.ANY))
            args.append(prev)
            aliases = {len(args) - 1: 0}
        out_specs = [pl.BlockSpec((tm, tn_), lambda i, j: (i + i0, j))]
        out_shape = [jax.ShapeDtypeStruct((n, d_ff), BF16)]
        if cast_w:
            out_specs += [pl.BlockSpec((d, tn_), lambda i, j: (0, j))] * 2
            out_shape += [jax.ShapeDtypeStruct((d, d_ff), BF16)] * 2
        blocks = _nbytes((tm, d), F32) + 2 * _nbytes((sub, d), F32) + w_bytes + _nbytes((tm, tn_), BF16)
        return pl.pallas_call(
            functools.partial(_ffn_in_kernel, tm=tm, sub=sub, cast_w=cast_w, has_prev=prev is not None),
            grid=(n_i, n_j),
            in_specs=in_specs,
            out_specs=out_specs,
            out_shape=out_shape,
            scratch_shapes=[pltpu.VMEM((tm, d), BF16)],
            input_output_aliases=aliases,
            compiler_params=pltpu.CompilerParams(
                dimension_semantics=("arbitrary", "arbitrary"),
                vmem_limit_bytes=_vmem_limit(blocks, _nbytes((tm, d), BF16) + 2 * _nbytes((tm, tn_), F32))),
            name="ffn_in_swiglu" + ("_first" if cast_w else ""),
        )(*args)

    return _first_tile_then_rest(run, n // tm, tn, tn_first, w3)


def _out_proj_kernel(a_ref, w_ref, *refs, tm, sub, npt, i0, n_x, cast_w, has_prev):
    x_refs, gate_ref = refs[:n_x], refs[n_x]
    outs = refs[n_x + 1 + (1 if has_prev else 0):]
    o_ref = outs[0]
    w = _load_weight(w_ref, outs[1] if cast_w else None)
    acc = jnp.dot(a_ref[...], w, preferred_element_type=F32)
    gate = gate_ref[...]
    is_prompt = pl.program_id(0) + i0 < npt
    for s in range(tm // sub):
        rows = slice(s * sub, (s + 1) * sub)
        x = x_refs[0][rows, :]
        if n_x == 2:
            x = jnp.where(is_prompt, x, x_refs[1][rows, :])
        o_ref[rows, :] = x + gate * acc[rows, :]


def _out_proj_call(a, w3, w_layer, xs, mod, layer, gate_col, lay, tm=1024, tn=512, tn_first=256):
    n, k = a.shape
    d = w3.shape[2]
    sub = lay["sub"]
    npt, midx = _tiling(lay, tm)

    def run(i0, n_i, tn_, weights, cast_w, prev):
        n_j = d // tn_
        w_spec = (pl.BlockSpec((None, k, tn_), lambda i, j: (w_layer, 0, j)) if cast_w
                  else pl.BlockSpec((k, tn_), lambda i, j: (0, j)))
        x_specs = ([pl.BlockSpec((tm, tn_), lambda i, j: (i + i0, j))] if len(xs) == 1
                   else [pl.BlockSpec((tm, tn_), lambda i, j: (jnp.minimum(i + i0, npt - 1), j)),
                         pl.BlockSpec((tm, tn_), lambda i, j: (jnp.maximum(i + i0 - npt, 0), j))])
        in_specs = ([pl.BlockSpec((tm, k), lambda i, j: (i + i0, 0)), w_spec] + x_specs
                    + [pl.BlockSpec((None, sub, tn_), lambda i, j: (layer, midx(i + i0), gate_col * n_j + j))])
        args = [a, weights if cast_w else weights[0], *xs, mod]
        aliases = {}
        if prev is not None:
            in_specs.append(pl.BlockSpec(memory_space=pl.ANY))
            args.append(prev)
            aliases = {len(args) - 1: 0}
        out_specs = [pl.BlockSpec((tm, tn_), lambda i, j: (i + i0, j))]
        out_shape = [jax.ShapeDtypeStruct((n, d), F32)]
        w_bytes = _nbytes((k, tn_), BF16)
        if cast_w:
            out_specs.append(pl.BlockSpec((k, tn_), lambda i, j: (0, j)))
            out_shape.append(jax.ShapeDtypeStruct((k, d), BF16))
            w_bytes = _nbytes((k, tn_), F32) + 2 * _nbytes((k, tn_), BF16)
        blocks = (_nbytes((tm, k), BF16) + w_bytes + (1 + len(xs)) * _nbytes((tm, tn_), F32)
                  + _nbytes((sub, tn_), F32))
        return pl.pallas_call(
            functools.partial(_out_proj_kernel, tm=tm, sub=sub, npt=npt, i0=i0, n_x=len(xs), cast_w=cast_w,
                              has_prev=prev is not None),
            grid=(n_i, n_j),
            in_specs=in_specs,
            out_specs=out_specs,
            out_shape=out_shape,
            input_output_aliases=aliases,
            compiler_params=pltpu.CompilerParams(
                dimension_semantics=("arbitrary", "arbitrary"),
                vmem_limit_bytes=_vmem_limit(blocks, _nbytes((tm, tn_), F32))),
            name="out_proj_residual" + ("_first" if cast_w else ""),
        )(*args)

    return _first_tile_then_rest(run, n // tm, tn, tn_first, w3)


def _retention_decays(c):
    lg = jnp.log1p(-jnp.exp2(-5.0 - jnp.arange(RET_HEADS, dtype=F32)))
    idx = jnp.arange(c, dtype=F32)
    diff = idx[:, None] - idx[None, :]
    causal = diff >= 0
    dmask = jnp.where(causal[None], jnp.exp(jnp.where(causal, diff, 0.0)[None] * lg[:, None, None]), 0.0)
    qdec = jnp.exp((idx + 1.0)[None, :] * lg[:, None])
    kdec = jnp.exp((c - 1.0 - idx)[None, :] * lg[:, None])
    sdec = jnp.exp(c * lg)
    return dmask, qdec[:, :, None], kdec[:, :, None], sdec[:, None, None]


def _group_norm_gate(o, g, gng):
    mu = jnp.mean(o, axis=-1, keepdims=True)
    dlt = o - mu
    var = jnp.mean(dlt * dlt, axis=-1, keepdims=True)
    return _silu(g) * (dlt * lax.rsqrt(var + EPS) * gng)


_NT = (((1,), (1,)), ((), ()))
_TN = (((0,), (0,)), ((), ()))


def _retention_chunk(q, k, v, s_prev, dmask, qdec, kdec, sdec):
    scores = lax.dot_general(q, k, _NT, preferred_element_type=F32) * dmask
    o = (jnp.dot(scores.astype(BF16), v, preferred_element_type=F32)
         + qdec * jnp.dot(q, s_prev.astype(BF16), preferred_element_type=F32))
    kd = (k.astype(F32) * kdec).astype(BF16)
    s_new = sdec * s_prev + lax.dot_general(kd, v, _TN, preferred_element_type=F32)
    return o, s_new


def _ret_prompt_kernel(q_ref, k_ref, v_ref, g_ref, gng_ref, dmask_ref, qdec_ref, kdec_ref, sdec_ref,
                       o_ref, s_out_ref, s_ref, *, chunk, n_chunks):
    s_ref[...] = jnp.zeros_like(s_ref)
    dmask = dmask_ref[0]
    qdec = qdec_ref[0]
    kdec = kdec_ref[0]
    sdec = sdec_ref[0]
    gng = gng_ref[...]

    def body(c, carry):
        rows = pl.ds(pl.multiple_of(c * chunk, chunk), chunk)
        o, s_new = _retention_chunk(q_ref[rows, :], k_ref[rows, :], v_ref[rows, :], s_ref[...],
                                    dmask, qdec, kdec, sdec)
        s_ref[...] = s_new
        o_ref[rows, :] = _group_norm_gate(o, g_ref[rows, :], gng).astype(o_ref.dtype)
        return carry

    lax.fori_loop(0, n_chunks, body, 0, unroll=2)
    s_out_ref[0, 0] = s_ref[...]


def _ret_prompt_call(qkv, g, gng, lay, chunk=RET_CHUNK):
    n = qkv.shape[0]
    bp, tp, dk, dv = lay["n_prompt"], lay["t_prompt"], lay["dk"], lay["dv"]
    h = RET_HEADS
    dmask, qdec, kdec, sdec = _retention_decays(chunk)
    v_blk0 = 2 * h * dk // dv
    blocks = (2 * _nbytes((tp, dk), BF16) + _nbytes((tp, dv), BF16) + _nbytes((tp, dv), F32)
              + _nbytes((tp, dv), BF16) + 2 * _nbytes((dk, dv), F32) + _nbytes((chunk, chunk), F32))
    return pl.pallas_call(
        functools.partial(_ret_prompt_kernel, chunk=chunk, n_chunks=tp // chunk),
        grid=(bp, h),
        in_specs=[
            pl.BlockSpec((tp, dk), lambda b, hh: (b, hh)),
            pl.BlockSpec((tp, dk), lambda b, hh: (b, h + hh)),
            pl.BlockSpec((tp, dv), lambda b, hh: (b, v_blk0 + hh)),
            pl.BlockSpec((tp, dv), lambda b, hh: (b, hh)),
            pl.BlockSpec((1, dv), lambda b, hh: (0, hh)),
            pl.BlockSpec((1, chunk, chunk), lambda b, hh: (hh, 0, 0)),
            pl.BlockSpec((1, chunk, 1), lambda b, hh: (hh, 0, 0)),
            pl.BlockSpec((1, chunk, 1), lambda b, hh: (hh, 0, 0)),
            pl.BlockSpec((1, 1, 1), lambda b, hh: (hh, 0, 0)),
        ],
        out_specs=[
            pl.BlockSpec((tp, dv), lambda b, hh: (b, hh)),
            pl.BlockSpec((1, 1, dk, dv), lambda b, hh: (b, hh, 0, 0)),
        ],
        out_shape=[jax.ShapeDtypeStruct((n, h * dv), BF16),
                   jax.ShapeDtypeStruct((bp, h, dk, dv), F32)],
        scratch_shapes=[pltpu.VMEM((dk, dv), F32)],
        compiler_params=pltpu.CompilerParams(
            dimension_semantics=("arbitrary", "arbitrary"),
            vmem_limit_bytes=_vmem_limit(blocks, _nbytes((dk, dv), F32))),
        name="retention_prompt",
    )(qkv, qkv, qkv, g, gng.reshape(1, h * dv), dmask, qdec, kdec, sdec)


def _ret_sample_kernel(q_ref, k_ref, v_ref, g_ref, gng_ref, s0_ref, dmask_ref, qdec_ref, kdec_ref, sdec_ref,
                       prev_ref, o_ref, s_out_ref, qf_ref, kf_ref, vf_ref, of_ref, *, bb):
    del prev_ref
    dmask = dmask_ref[0]
    qdec = qdec_ref[0]
    kdec = kdec_ref[0]
    sdec = sdec_ref[0]
    gng = gng_ref[...]
    qf_ref[...] = q_ref[...].astype(F32)
    kf_ref[...] = k_ref[...].astype(F32)
    vf_ref[...] = v_ref[...].astype(F32)

    def body(b, carry):
        o, s_new = _retention_chunk(qf_ref[:, b, :].astype(BF16), kf_ref[:, b, :].astype(BF16),
                                    vf_ref[:, b, :].astype(BF16), s0_ref[b, 0], dmask, qdec, kdec, sdec)
        s_out_ref[b, 0] = s_new
        of_ref[:, b, :] = _group_norm_gate(o, g_ref[:, b, :], gng)
        return carry

    lax.fori_loop(0, bb, body, 0, unroll=4)
    o_ref[...] = of_ref[...].astype(o_ref.dtype)


def _ret_sample_call(qkv, g, gng, state, gated, lay, bb=16):
    bs, ts, dk, dv = lay["sub"], lay["t_sample"], lay["dk"], lay["dv"]
    h = RET_HEADS
    n = qkv.shape[0]
    t_blk = lay["n_prompt_rows"] // (bs * ts)
    assert t_blk * bs * ts == lay["n_prompt_rows"]
    dmask, qdec, kdec, sdec = _retention_decays(ts)
    v_blk0 = 2 * h * dk // dv
    blocks = (2 * _nbytes((ts, bb, dk), BF16) + 2 * _nbytes((ts, bb, dv), BF16) + _nbytes((ts, bb, dv), F32)
              + 2 * _nbytes((bb, dk, dv), F32))
    scratch = 2 * _nbytes((ts, bb, dk), F32) + 2 * _nbytes((ts, bb, dv), F32)
    qkv3 = qkv.reshape(n // bs, bs, qkv.shape[1])
    gated3, s_new = pl.pallas_call(
        functools.partial(_ret_sample_kernel, bb=bb),
        grid=(bs // bb, h),
        in_specs=[
            pl.BlockSpec((ts, bb, dk), lambda b, hh: (t_blk, b, hh)),
            pl.BlockSpec((ts, bb, dk), lambda b, hh: (t_blk, b, h + hh)),
            pl.BlockSpec((ts, bb, dv), lambda b, hh: (t_blk, b, v_blk0 + hh)),
            pl.BlockSpec((ts, bb, dv), lambda b, hh: (t_blk, b, hh)),
            pl.BlockSpec((1, dv), lambda b, hh: (0, hh)),
            pl.BlockSpec((bb, 1, dk, dv), lambda b, hh: (b, hh, 0, 0)),
            pl.BlockSpec((1, ts, ts), lambda b, hh: (hh, 0, 0)),
            pl.BlockSpec((1, ts, 1), lambda b, hh: (hh, 0, 0)),
            pl.BlockSpec((1, ts, 1), lambda b, hh: (hh, 0, 0)),
            pl.BlockSpec((1, 1, 1), lambda b, hh: (hh, 0, 0)),
            pl.BlockSpec(memory_space=pl.ANY),
        ],
        out_specs=[
            pl.BlockSpec((ts, bb, dv), lambda b, hh: (t_blk, b, hh)),
            pl.BlockSpec((bb, 1, dk, dv), lambda b, hh: (b, hh, 0, 0)),
        ],
        out_shape=[jax.ShapeDtypeStruct((n // bs, bs, h * dv), gated.dtype),
                   jax.ShapeDtypeStruct((bs, h, dk, dv), F32)],
        scratch_shapes=[pltpu.VMEM((ts, bb, dk), F32), pltpu.VMEM((ts, bb, dk), F32),
                        pltpu.VMEM((ts, bb, dv), F32), pltpu.VMEM((ts, bb, dv), F32)],
        input_output_aliases={10: 0},
        compiler_params=pltpu.CompilerParams(
            dimension_semantics=("arbitrary", "arbitrary"),
            vmem_limit_bytes=_vmem_limit(blocks, scratch)),
        name="retention_sample",
    )(qkv3, qkv3, qkv3, g.reshape(n // bs, bs, h * dv), gng.reshape(1, h * dv), state, dmask, qdec, kdec, sdec,
      gated.reshape(n // bs, bs, h * dv))
    return gated3.reshape(n, h * dv), s_new


def _window_sum_rows(win_ref, halo, r0, rows, win, stride):
    if stride % V7X_SUBLANES == 0:
        acc = win_ref[pl.ds(halo + r0, rows), :]
        for k in range(1, win):
            acc = acc + win_ref[pl.ds(halo - k * stride + r0, rows), :]
        return acc
    assert stride == 1 and halo == 16 and win <= 16
    blk = win_ref[pl.ds(r0, rows + 16), :]
    shift = 1
    while shift < win:
        blk = blk + pltpu.roll(blk, shift, 0)
        shift *= 2
    return blk[16:, :]


def _pool_kernel(u_ref, hist_ref, x_ref, gate_ref, w_ref, scale_ref, *rest,
                 tm, sub, stride, tiles_per_seq, hist_len, chunk_rows):
    o_ref, hist_out_ref, win_ref, m_ref = rest[-4:]
    g = pl.program_id(0)
    i = pl.program_id(1)
    halo = 16 * stride
    tile_in_seq = i % tiles_per_seq

    if tiles_per_seq > 1:
        @pl.when(tile_in_seq != 0)
        def _():
            win_ref[0:halo, :] = win_ref[tm:tm + halo, :]

    @pl.when(tile_in_seq == 0)
    def _():
        win_ref[0:halo, :] = hist_ref[0]

    win_ref[halo:halo + tm, :] = u_ref[...]

    for gi, win in enumerate(POOL_WINDOWS):
        @pl.when(g == gi)
        def _(win=win):
            def body(c, carry):
                r0 = pl.multiple_of(c * chunk_rows, chunk_rows)
                cur = win_ref[pl.ds(halo + r0, chunk_rows), :]
                acc = _window_sum_rows(win_ref, halo, r0, chunk_rows, win, stride)
                t = (tile_in_seq * tm + r0 + lax.broadcasted_iota(jnp.int32, (chunk_rows, 1), 0)) // stride
                cnt = jnp.minimum(win, t + 1 + hist_len).astype(F32)
                m_ref[pl.ds(r0, chunk_rows), :] = (acc / cnt - cur).astype(m_ref.dtype)
                return carry

            lax.fori_loop(0, tm // chunk_rows, body, 0)

    y = jnp.dot(m_ref[...], w_ref[0], preferred_element_type=F32) * scale_ref[...]
    gate = gate_ref[...]
    for s in range(tm // sub):
        rows = slice(s * sub, (s + 1) * sub)
        o_ref[rows, :] = x_ref[rows, :] + gate * y[rows, :]
    hist_out_ref[0] = win_ref[tm + stride:tm + halo, :]


def _pool_call(u, hist, x, mod, layer, w, scale, x_prev, *, row0, n_rows, n_seq, stride, hist_len, mod_seq0,
               lay, tm=1024):
    n, d = x.shape
    sub = lay["sub"]
    n_g = len(POOL_WINDOWS)
    gc = d // n_g
    n_tiles = n_rows // tm
    tps = n_tiles // n_seq
    blk0 = row0 // tm
    halo = 16 * stride
    blocks = (3 * _nbytes((tm, gc), F32) + _nbytes((halo, gc), F32) + _nbytes((sub, gc), F32)
              + _nbytes((gc, gc), BF16) + _nbytes((15 * stride, gc), F32))
    scratch = _nbytes((halo + tm, gc), F32) + _nbytes((tm, gc), BF16)
    gate_col0 = 2 * n_g
    in_specs = [
        pl.BlockSpec((tm, gc), lambda g, i: (blk0 + i, g)),
        pl.BlockSpec((1, halo, gc), lambda g, i: (i // tps, 0, g)),
        pl.BlockSpec((tm, gc), lambda g, i: (blk0 + i, g)),
        pl.BlockSpec((None, sub, gc), lambda g, i: (layer, mod_seq0 + i // tps, gate_col0 + g)),
        pl.BlockSpec((1, gc, gc), lambda g, i: (g, 0, 0)),
        pl.BlockSpec((1, gc), lambda g, i: (0, g)),
    ]
    args = [u, hist, x, mod, w, scale.reshape(1, d)]
    aliases = {}
    if x_prev is not None:
        in_specs.append(pl.BlockSpec(memory_space=pl.ANY))
        args.append(x_prev)
        aliases = {len(args) - 1: 0}
    return pl.pallas_call(
        functools.partial(_pool_kernel, tm=tm, sub=sub, stride=stride, tiles_per_seq=tps,
                          hist_len=hist_len, chunk_rows=64),
        grid=(n_g, n_tiles),
        in_specs=in_specs,
        out_specs=[
            pl.BlockSpec((tm, gc), lambda g, i: (blk0 + i, g)),
            pl.BlockSpec((1, 15 * stride, gc), lambda g, i: (i // tps, 0, g)),
        ],
        out_shape=[jax.ShapeDtypeStruct((n, d), F32),
                   jax.ShapeDtypeStruct((n_seq, 15 * stride, d), F32)],
        scratch_shapes=[pltpu.VMEM((halo + tm, gc), F32), pltpu.VMEM((tm, gc), BF16)],
        input_output_aliases=aliases,
        compiler_params=pltpu.CompilerParams(
            dimension_semantics=("arbitrary", "arbitrary"),
            vmem_limit_bytes=_vmem_limit(blocks, scratch)),
        name="pool_mixer_s%d" % stride,
    )(*args)


def _rope_tables(pos, dk):
    half = dk // 2
    inv = 1.0 / (ROPE_BASE ** (jnp.arange(half, dtype=F32) / half))
    ang = pos[:, None] * inv[None, :]
    return jnp.cos(ang), jnp.sin(ang)


def kernel(x_prompt, x_sample, c_prompt, c_sample, state_ret, state_pool, norm_mix_g, norm_ffn_g, ada_w, ada_b,
           ret_w_in, ret_gn_g, ret_w_out, pool_w, pool_scale, ffn_w_in, ffn_w_out, final_norm_g):
    bp, tp, d = x_prompt.shape
    bs, ts, _ = x_sample.shape
    depth = ada_w.shape[0]
    dk = state_ret.shape[3]
    dv = state_ret.shape[4]
    n_hist = state_pool.shape[2]
    assert bp <= C_PAD and n_hist == max(POOL_WINDOWS) - 1 and depth == 2
    n_p = bp * tp
    n_s = bs * ts
    lay = dict(n_prompt=bp, t_prompt=tp, n_prompt_rows=n_p, n_sample_rows=n_s, sub=bs, t_sample=ts, dk=dk, dv=dv)

    xp = x_prompt.reshape(n_p, d)
    xs = x_sample.transpose(1, 0, 2).reshape(n_s, d)
    c_all = jnp.concatenate([c_prompt, jnp.zeros((C_PAD - bp, d), F32), c_sample], axis=0)
    mod = _ada_call(c_all, ada_w, ada_b, bp, bs)

    cos_p, sin_p = _rope_tables(jnp.arange(tp, dtype=F32), dk)
    cos_s, sin_s = _rope_tables(float(PAST_LEN) + jnp.arange(ts, dtype=F32), dk)
    cos = jnp.concatenate([jnp.tile(cos_p, (bp, 1)), jnp.repeat(cos_s, bs, axis=0)], axis=0)
    sin = jnp.concatenate([jnp.tile(sin_p, (bp, 1)), jnp.repeat(sin_s, bs, axis=0)], axis=0)

    h0 = _norm_call((xp, xs), norm_mix_g[0], mod, 0, 1, 0, lay, BF16)
    n_qkv = RET_HEADS * (2 * dk + dv)
    qkv = _ret_proj_call(h0, ret_w_in, 0, 0, n_qkv, BF16, lay, rope_tables=(cos, sin))
    g = _ret_proj_call(h0, ret_w_in, 0, n_qkv, RET_HEADS * dv, F32, lay)
    gated, s_ret_p = _ret_prompt_call(qkv, g, ret_gn_g[0], lay)
    gated, s_ret_s = _ret_sample_call(qkv, g, ret_gn_g[0], state_ret[0], gated, lay)
    x = _out_proj_call(gated, ret_w_out, 0, (xp, xs), mod, 0, 2, lay)
    hmid = _ffn_in_call(x, norm_ffn_g[0], mod, 0, ffn_w_in, lay)
    x = _out_proj_call(hmid, ffn_w_out, 0, (x,), mod, 0, 5, lay)

    u = _norm_call((x,), norm_mix_g[1], mod, 1, 1, 0, lay, F32)
    pw = pool_w[0].astype(BF16)
    hist_p = jnp.zeros((bp, 16, d), F32)
    hist_s = jnp.concatenate([jnp.zeros((1, bs, d), F32), state_pool[0].transpose(1, 0, 2)], axis=0)
    x1, nh_p = _pool_call(u, hist_p, x, mod, 1, pw, pool_scale[0], None, row0=0, n_rows=n_p, n_seq=bp,
                          stride=1, hist_len=0, mod_seq0=0, lay=lay)
    x, nh_s = _pool_call(u, hist_s.reshape(1, 16 * bs, d), x, mod, 1, pw, pool_scale[0], x1, row0=n_p,
                         n_rows=n_s, n_seq=1, stride=bs, hist_len=min(PAST_LEN, n_hist), mod_seq0=bp, lay=lay)
    hmid = _ffn_in_call(x, norm_ffn_g[1], mod, 1, ffn_w_in, lay)
    x = _out_proj_call(hmid, ffn_w_out, 1, (x,), mod, 1, 5, lay)

    y_p, y_s = _norm_call((x,), final_norm_g, None, 0, 0, 0, lay, F32, split_out=True)
    y_prompt = y_p.reshape(bp, tp, d)
    y_sample = y_s.reshape(ts, bs, d).transpose(1, 0, 2)
    state_pool_sample = nh_s.reshape(n_hist, bs, d).transpose(1, 0, 2)
    return (y_prompt, y_sample, s_ret_p[None], nh_p[None], s_ret_s[None], state_pool_sample[None])
```

```python
import functools

import jax
import jax.numpy as jnp
from jax import lax
from jax.experimental import pallas as pl
from jax.experimental.pallas import tpu as pltpu

F32 = jnp.float32
BF16 = jnp.bfloat16

RET_HEADS = 8
RET_CHUNK = 128
ROPE_BASE = 10000.0
POOL_WINDOWS = (2, 4, 8, 16)
N_ADA = 6
EPS = 1e-6
PAST_LEN = 16384

V7X_VMEM_BYTES = 64 * 1024 * 1024
V7X_LANES = 128
V7X_SUBLANES = 8
NORM_ROWS = 16
C_PAD = 8


def _vmem_limit(block_bytes, scratch_bytes=0):
    need = 2 * block_bytes + scratch_bytes + 12 * 1024 * 1024
    return int(min(need, V7X_VMEM_BYTES - 6 * 1024 * 1024))


def _nbytes(shape, dtype):
    n = 1
    for s in shape:
        n *= s
    return n * jnp.dtype(dtype).itemsize


def _silu(x):
    return x / (1.0 + jnp.exp(-x))


def _ada_kernel(c_ref, w_ref, b_ref, o_ref, *, n_prompt, sub):
    cs = _silu(c_ref[...])
    res = jnp.dot(cs.astype(BF16), w_ref[0].astype(BF16), preferred_element_type=F32) + b_ref[0]
    tn = res.shape[-1]
    for b in range(n_prompt):
        o_ref[0, b * sub:(b + 1) * sub, :] = jnp.broadcast_to(res[b:b + 1, :], (sub, tn))
    o_ref[0, n_prompt * sub:, :] = res[C_PAD:C_PAD + sub, :]


def _ada_call(c_all, ada_w, ada_b, n_prompt, sub, tn=1024):
    depth, d, n6 = ada_w.shape
    rows = c_all.shape[0]
    out_rows = (n_prompt + 1) * sub
    blocks = (_nbytes((rows, d), F32) + _nbytes((d, tn), F32) + _nbytes((1, tn), F32)
              + _nbytes((out_rows, tn), F32))
    return pl.pallas_call(
        functools.partial(_ada_kernel, n_prompt=n_prompt, sub=sub),
        grid=(depth, n6 // tn),
        in_specs=[
            pl.BlockSpec((rows, d), lambda l, j: (0, 0)),
            pl.BlockSpec((1, d, tn), lambda l, j: (l, 0, j)),
            pl.BlockSpec((1, 1, tn), lambda l, j: (l, 0, j)),
        ],
        out_specs=pl.BlockSpec((1, out_rows, tn), lambda l, j: (l, 0, j)),
        out_shape=jax.ShapeDtypeStruct((depth, out_rows, n6), F32),
        compiler_params=pltpu.CompilerParams(
            dimension_semantics=("arbitrary", "arbitrary"),
            vmem_limit_bytes=_vmem_limit(blocks, _nbytes((d, tn), BF16))),
        name="ada_mod",
    )(c_all, ada_w, ada_b.reshape(depth, 1, n6))


def _norm_mod_rows(x_ref, gam_ref, sc_ref, sh_ref, out_ref, *, tm, sub):
    gam = gam_ref[...]
    per_sub = sub // NORM_ROWS

    def body(c, carry):
        r0 = pl.multiple_of(c * NORM_ROWS, NORM_ROWS)
        x = x_ref[pl.ds(r0, NORM_ROWS), :]
        hn = (x * lax.rsqrt(jnp.mean(x * x, axis=-1, keepdims=True) + EPS)) * gam
        if sc_ref is not None:
            m0 = pl.multiple_of((c % per_sub) * NORM_ROWS, NORM_ROWS)
            hn = hn * (1.0 + sc_ref[pl.ds(m0, NORM_ROWS), :]) + sh_ref[pl.ds(m0, NORM_ROWS), :]
        out_ref[pl.ds(r0, NORM_ROWS), :] = hn.astype(out_ref.dtype)
        return carry

    lax.fori_loop(0, tm // NORM_ROWS, body, 0, unroll=4)


def _mod_index(i, n_prompt_tiles, tiles_per_seq, n_prompt):
    return jnp.where(i < n_prompt_tiles, i // tiles_per_seq, n_prompt)


def _tiling(lay, tm):
    npt = lay["n_prompt_rows"] // tm
    midx = functools.partial(_mod_index, n_prompt_tiles=npt, tiles_per_seq=lay["t_prompt"] // tm,
                             n_prompt=lay["n_prompt"])
    return npt, midx


def _norm_kernel(*refs, tm, sub, n_in, n_out, has_mod, npt):
    x_refs = refs[:n_in]
    gam_ref = refs[n_in]
    sc_ref, sh_ref = (refs[n_in + 1], refs[n_in + 2]) if has_mod else (None, None)
    o_refs = refs[len(refs) - n_out:]
    if n_in == 1 and n_out == 1:
        _norm_mod_rows(x_refs[0], gam_ref, sc_ref, sh_ref, o_refs[0], tm=tm, sub=sub)
        return
    i = pl.program_id(0)

    @pl.when(i < npt)
    def _():
        _norm_mod_rows(x_refs[0], gam_ref, sc_ref, sh_ref, o_refs[0], tm=tm, sub=sub)

    @pl.when(i >= npt)
    def _():
        _norm_mod_rows(x_refs[-1], gam_ref, sc_ref, sh_ref, o_refs[-1], tm=tm, sub=sub)


def _norm_call(xs, gam, mod, layer, col_sc, col_sh, lay, out_dtype, split_out=False, tm=512):
    d = xs[0].shape[1]
    sub = lay["sub"]
    n_p, n_s = lay["n_prompt_rows"], lay["n_sample_rows"]
    npt, midx = _tiling(lay, tm)
    prompt_blk = lambda i: (jnp.minimum(i, npt - 1), 0)
    sample_blk = lambda i: (jnp.maximum(i - npt, 0), 0)
    whole_blk = lambda i: (i, 0)
    in_specs = ([pl.BlockSpec((tm, d), whole_blk)] if len(xs) == 1
                else [pl.BlockSpec((tm, d), prompt_blk), pl.BlockSpec((tm, d), sample_blk)])
    in_specs.append(pl.BlockSpec((1, d), lambda i: (0, 0)))
    args = list(xs) + [gam.reshape(1, d)]
    if mod is not None:
        in_specs += [pl.BlockSpec((None, sub, d), lambda i: (layer, midx(i), col_sc)),
                     pl.BlockSpec((None, sub, d), lambda i: (layer, midx(i), col_sh))]
        args += [mod, mod]
    if split_out:
        out_specs = [pl.BlockSpec((tm, d), prompt_blk), pl.BlockSpec((tm, d), sample_blk)]
        out_shape = [jax.ShapeDtypeStruct((n_p, d), out_dtype), jax.ShapeDtypeStruct((n_s, d), out_dtype)]
    else:
        out_specs = pl.BlockSpec((tm, d), whole_blk)
        out_shape = jax.ShapeDtypeStruct((n_p + n_s, d), out_dtype)
    n_out = 2 if split_out else 1
    blocks = (len(xs) + n_out) * _nbytes((tm, d), F32) + 2 * _nbytes((sub, d), F32)
    return pl.pallas_call(
        functools.partial(_norm_kernel, tm=tm, sub=sub, n_in=len(xs), n_out=n_out,
                          has_mod=mod is not None, npt=npt),
        grid=((n_p + n_s) // tm,),
        in_specs=in_specs,
        out_specs=out_specs,
        out_shape=out_shape,
        compiler_params=pltpu.CompilerParams(
            dimension_semantics=("arbitrary",), vmem_limit_bytes=_vmem_limit(blocks)),
        name="row_norm",
    )(*args)


def _load_weight(w_ref, copy_ref):
    w = w_ref[...]
    if copy_ref is not None:
        w = w.astype(BF16)
        copy_ref[...] = w
    return w


class _SideCast:
    def __init__(self, src3, layer, block, n_i, n_j):
        _, r, c = src3.shape
        br, bc = block
        ncb = c // bc
        n_blocks = (r // br) * ncb
        assert n_blocks <= n_i * n_j and r % br == 0 and c % bc == 0

        def blk(i, j):
            t = jnp.minimum(i * n_j + j, n_blocks - 1)
            return t // ncb, t % ncb

        self.arg = src3
        self.in_spec = pl.BlockSpec((None, br, bc), lambda i, j: (layer, *blk(i, j)))
        self.out_spec = pl.BlockSpec((br, bc), blk)
        self.out_shape = jax.ShapeDtypeStruct((r, c), BF16)
        self.block_bytes = _nbytes(block, F32) + _nbytes(block, BF16)


def _ret_proj_kernel(h_ref, w_ref, *rest, tn, d_qk, head_dk, k_scale, rope, cast_w, has_prev, has_side):
    n_in = (2 if rope else 0) + (1 if has_prev else 0) + (1 if has_side else 0)
    o_ref = rest[n_in]
    if has_side:
        rest[-1][...] = rest[n_in - 1][...].astype(BF16)
    w = _load_weight(w_ref, rest[n_in + 1] if cast_w else None)
    acc = jnp.dot(h_ref[...], w, preferred_element_type=F32)
    if not rope:
        o_ref[...] = acc.astype(o_ref.dtype)
        return
    cos_ref, sin_ref = rest[0], rest[1]
    col0 = pl.program_id(1) * tn
    is_qk = col0 < 2 * d_qk
    scale = jnp.where(jnp.logical_and(is_qk, col0 >= d_qk), k_scale, 1.0).astype(F32)
    cs = jnp.where(is_qk, cos_ref[...], 1.0) * scale
    sn = jnp.where(is_qk, sin_ref[...], 0.0) * scale
    half = head_dk // 2
    for hh in range(tn // head_dk):
        x1 = acc[:, hh * head_dk:hh * head_dk + half]
        x2 = acc[:, hh * head_dk + half:(hh + 1) * head_dk]
        o_ref[:, hh * head_dk:hh * head_dk + half] = (x1 * cs - x2 * sn).astype(o_ref.dtype)
        o_ref[:, hh * head_dk + half:(hh + 1) * head_dk] = (x1 * sn + x2 * cs).astype(o_ref.dtype)


def _ret_proj_call(h, w3, w_layer, col0, n_cols, out_dtype, lay, side, rope_tables=None,
                   tm=1024, tn=1024, tn_first=512):
    n, d = h.shape
    d_qk = RET_HEADS * lay["dk"]
    half = lay["dk"] // 2
    rope = rope_tables is not None

    def run(i0, n_i, tn_, weights, cast_w, prev, side_job):
        jblk0 = col0 // tn_
        w_spec = (pl.BlockSpec((None, d, tn_), lambda i, j: (w_layer, 0, jblk0 + j)) if cast_w
                  else pl.BlockSpec((d, tn_), lambda i, j: (0, j)))
        in_specs = [pl.BlockSpec((tm, d), lambda i, j: (i + i0, 0)), w_spec]
        args = [h, weights]
        if rope:
            in_specs += [pl.BlockSpec((tm, half), lambda i, j: (i + i0, 0))] * 2
            args += list(rope_tables)
        aliases = {}
        if prev is not None:
            in_specs.append(pl.BlockSpec(memory_space=pl.ANY))
            args.append(prev)
            aliases = {len(args) - 1: 0}
        out_specs = [pl.BlockSpec((tm, tn_), lambda i, j: (i + i0, j))]
        out_shape = [jax.ShapeDtypeStruct((n, n_cols), out_dtype)]
        w_bytes = _nbytes((d, tn_), BF16)
        if cast_w:
            out_specs.append(pl.BlockSpec((d, tn_), lambda i, j: (0, j)))
            out_shape.append(jax.ShapeDtypeStruct((d, n_cols), BF16))
            w_bytes = _nbytes((d, tn_), F32) + 2 * _nbytes((d, tn_), BF16)
        if side_job is not None:
            sc = _SideCast(*side_job, n_i, n_cols // tn_)
            in_specs.append(sc.in_spec)
            args.append(sc.arg)
            out_specs.append(sc.out_spec)
            out_shape.append(sc.out_shape)
            w_bytes += sc.block_bytes
        blocks = (_nbytes((tm, d), BF16) + w_bytes + 2 * _nbytes((tm, half), F32) + _nbytes((tm, tn_), out_dtype))
        return pl.pallas_call(
            functools.partial(_ret_proj_kernel, tn=tn_, d_qk=d_qk, head_dk=lay["dk"],
                              k_scale=float(lay["dk"]) ** -0.5, rope=rope, cast_w=cast_w,
                              has_prev=prev is not None, has_side=side_job is not None),
            grid=(n_i, n_cols // tn_),
            in_specs=in_specs,
            out_specs=out_specs,
            out_shape=out_shape,
            input_output_aliases=aliases,
            compiler_params=pltpu.CompilerParams(
                dimension_semantics=("arbitrary", "arbitrary"),
                vmem_limit_bytes=_vmem_limit(blocks, _nbytes((tm, tn_), F32))),
            name=("ret_in_proj_qkv" if rope else "ret_in_proj_gate") + ("_first" if cast_w else ""),
        )(*args)

    out0, w_bf = run(0, 1, tn_first, w3, True, None, None)
    return run(1, n // tm - 1, tn, w_bf, False, out0, side)


def _ffn_in_kernel(x_ref, gam_ref, sc_ref, sh_ref, wg_ref, wu_ref, *rest, tm, sub, has_side):
    if has_side:
        side_in, o_ref, side_out, h_ref = rest
        side_out[...] = side_in[...].astype(BF16)
    else:
        o_ref, h_ref = rest

    @pl.when(pl.program_id(1) == 0)
    def _():
        _norm_mod_rows(x_ref, gam_ref, sc_ref, sh_ref, h_ref, tm=tm, sub=sub)

    h = h_ref[...]
    gate = jnp.dot(h, wg_ref[...], preferred_element_type=F32)
    up = jnp.dot(h, wu_ref[...], preferred_element_type=F32)
    o_ref[...] = (_silu(gate) * up).astype(o_ref.dtype)


def _ffn_in_call(x, gam, mod, layer, w, lay, side=None, tm=1024, tn=512):
    n, d = x.shape
    sub = lay["sub"]
    d_ff = w.shape[1] // 2
    n_j = d_ff // tn
    _, midx = _tiling(lay, tm)
    in_specs = [
        pl.BlockSpec((tm, d), lambda i, j: (i, 0)),
        pl.BlockSpec((1, d), lambda i, j: (0, 0)),
        pl.BlockSpec((None, sub, d), lambda i, j: (layer, midx(i), 4)),
        pl.BlockSpec((None, sub, d), lambda i, j: (layer, midx(i), 3)),
        pl.BlockSpec((d, tn), lambda i, j: (0, j)),
        pl.BlockSpec((d, tn), lambda i, j: (0, j + n_j)),
    ]
    args = [x, gam.reshape(1, d), mod, mod, w, w]
    out_specs = [pl.BlockSpec((tm, tn), lambda i, j: (i, j))]
    out_shape = [jax.ShapeDtypeStruct((n, d_ff), BF16)]
    blocks = (_nbytes((tm, d), F32) + 2 * _nbytes((sub, d), F32) + 2 * _nbytes((d, tn), BF16)
              + _nbytes((tm, tn), BF16))
    if side is not None:
        sc = _SideCast(*side, n // tm, n_j)
        in_specs.append(sc.in_spec)
        args.append(sc.arg)
        out_specs.append(sc.out_spec)
        out_shape.append(sc.out_shape)
        blocks += sc.block_bytes
    res = pl.pallas_call(
        functools.partial(_ffn_in_kernel, tm=tm, sub=sub, has_side=side is not None),
        grid=(n // tm, n_j),
        in_specs=in_specs,
        out_specs=out_specs,
        out_shape=out_shape,
        scratch_shapes=[pltpu.VMEM((tm, d), BF16)],
        compiler_params=pltpu.CompilerParams(
            dimension_semantics=("arbitrary", "arbitrary"),
            vmem_limit_bytes=_vmem_limit(blocks, _nbytes((tm, d), BF16) + 2 * _nbytes((tm, tn), F32))),
        name="ffn_in_swiglu",
    )(*args)
    return res[0], (res[1] if side is not None else None)


def _out_proj_kernel(a_ref, w_ref, *refs, tm, sub, npt, n_x, has_side, emit_ssq):
    x_refs, gate_ref = refs[:n_x], refs[n_x]
    rest = list(refs[n_x + 1:])
    ssq_ref = rest.pop() if emit_ssq else None
    if has_side:
        side_in, o_ref, side_out = rest
        side_out[...] = side_in[...].astype(BF16)
    else:
        (o_ref,) = rest
    acc = jnp.dot(a_ref[...], w_ref[...], preferred_element_type=F32)
    gate = gate_ref[...]
    is_prompt = pl.program_id(0) < npt
    for s in range(tm // sub):
        rows = slice(s * sub, (s + 1) * sub)
        x = x_refs[0][rows, :]
        if n_x == 2:
            x = jnp.where(is_prompt, x, x_refs[1][rows, :])
        x_new = x + gate * acc[rows, :]
        o_ref[rows, :] = x_new
        if emit_ssq:
            ssq_ref[rows, :] = jnp.sum(x_new * x_new, axis=-1, keepdims=True)


def _out_proj_call(a, w, xs, mod, layer, gate_col, lay, side=None, emit_ssq=False, tm=1024, tn=512):
    n, k = a.shape
    d = w.shape[1]
    sub = lay["sub"]
    npt, midx = _tiling(lay, tm)
    n_j = d // tn
    x_specs = ([pl.BlockSpec((tm, tn), lambda i, j: (i, j))] if len(xs) == 1
               else [pl.BlockSpec((tm, tn), lambda i, j: (jnp.minimum(i, npt - 1), j)),
                     pl.BlockSpec((tm, tn), lambda i, j: (jnp.maximum(i - npt, 0), j))])
    in_specs = ([pl.BlockSpec((tm, k), lambda i, j: (i, 0)), pl.BlockSpec((k, tn), lambda i, j: (0, j))] + x_specs
                + [pl.BlockSpec((None, sub, tn), lambda i, j: (layer, midx(i), gate_col * n_j + j))])
    args = [a, w, *xs, mod]
    out_specs = [pl.BlockSpec((tm, tn), lambda i, j: (i, j))]
    out_shape = [jax.ShapeDtypeStruct((n, d), F32)]
    blocks = (_nbytes((tm, k), BF16) + _nbytes((k, tn), BF16) + (1 + len(xs)) * _nbytes((tm, tn), F32)
              + _nbytes((sub, tn), F32))
    if side is not None:
        sc = _SideCast(*side, n // tm, n_j)
        in_specs.append(sc.in_spec)
        args.append(sc.arg)
        out_specs.append(sc.out_spec)
        out_shape.append(sc.out_shape)
        blocks += sc.block_bytes
    if emit_ssq:
        out_specs.append(pl.BlockSpec((None, tm, 1), lambda i, j: (j, i, 0)))
        out_shape.append(jax.ShapeDtypeStruct((n_j, n, 1), F32))
        blocks += _nbytes((tm, V7X_LANES), F32)
    res = pl.pallas_call(
        functools.partial(_out_proj_kernel, tm=tm, sub=sub, npt=npt, n_x=len(xs), has_side=side is not None,
                          emit_ssq=emit_ssq),
        grid=(n // tm, n_j),
        in_specs=in_specs,
        out_specs=out_specs,
        out_shape=out_shape,
        compiler_params=pltpu.CompilerParams(
            dimension_semantics=("arbitrary", "arbitrary"),
            vmem_limit_bytes=_vmem_limit(blocks, _nbytes((tm, tn), F32))),
        name="out_proj_residual",
    )(*args)
    return res[0], (res[1] if side is not None else None), (res[-1] if emit_ssq else None)


def _retention_decays(c):
    lg = jnp.log1p(-jnp.exp2(-5.0 - jnp.arange(RET_HEADS, dtype=F32)))
    idx = jnp.arange(c, dtype=F32)
    diff = idx[:, None] - idx[None, :]
    causal = diff >= 0
    dmask = jnp.where(causal[None], jnp.exp(jnp.where(causal, diff, 0.0)[None] * lg[:, None, None]), 0.0)
    qdec = jnp.exp((idx + 1.0)[None, :] * lg[:, None])
    kdec = jnp.exp((c - 1.0 - idx)[None, :] * lg[:, None])
    sdec = jnp.exp(c * lg)
    return dmask, qdec[:, :, None], kdec[:, :, None], sdec[:, None, None]


def _group_norm_gate(o, g, gng):
    mu = jnp.mean(o, axis=-1, keepdims=True)
    dlt = o - mu
    var = jnp.mean(dlt * dlt, axis=-1, keepdims=True)
    return _silu(g) * (dlt * lax.rsqrt(var + EPS) * gng)


_NT = (((1,), (1,)), ((), ()))
_TN = (((0,), (0,)), ((), ()))


def _retention_chunk(q, k, v, s_prev, dmask, qdec, kdec, sdec):
    scores = lax.dot_general(q, k, _NT, preferred_element_type=F32) * dmask
    o = (jnp.dot(scores.astype(BF16), v, preferred_element_type=F32)
         + qdec * jnp.dot(q, s_prev.astype(BF16), preferred_element_type=F32))
    kd = (k.astype(F32) * kdec).astype(BF16)
    s_new = sdec * s_prev + lax.dot_general(kd, v, _TN, preferred_element_type=F32)
    return o, s_new


def _ret_prompt_kernel(q_ref, k_ref, v_ref, g_ref, gng_ref, dmask_ref, qdec_ref, kdec_ref, sdec_ref,
                       o_ref, s_out_ref, s_ref, *, chunk, n_chunks):
    s_ref[...] = jnp.zeros_like(s_ref)
    dmask = dmask_ref[0]
    qdec = qdec_ref[0]
    kdec = kdec_ref[0]
    sdec = sdec_ref[0]
    gng = gng_ref[...]

    def body(c, carry):
        rows = pl.ds(pl.multiple_of(c * chunk, chunk), chunk)
        o, s_new = _retention_chunk(q_ref[rows, :], k_ref[rows, :], v_ref[rows, :], s_ref[...],
                                    dmask, qdec, kdec, sdec)
        s_ref[...] = s_new
        o_ref[rows, :] = _group_norm_gate(o, g_ref[rows, :], gng).astype(o_ref.dtype)
        return carry

    lax.fori_loop(0, n_chunks, body, 0, unroll=2)
    s_out_ref[0, 0] = s_ref[...]


def _ret_prompt_call(qkv, g, gng, lay, chunk=2 * RET_CHUNK):
    n = qkv.shape[0]
    bp, tp, dk, dv = lay["n_prompt"], lay["t_prompt"], lay["dk"], lay["dv"]
    h = RET_HEADS
    dmask, qdec, kdec, sdec = _retention_decays(chunk)
    v_blk0 = 2 * h * dk // dv
    blocks = (2 * _nbytes((tp, dk), BF16) + _nbytes((tp, dv), BF16) + _nbytes((tp, dv), F32)
              + _nbytes((tp, dv), BF16) + 2 * _nbytes((dk, dv), F32) + _nbytes((chunk, chunk), F32))
    return pl.pallas_call(
        functools.partial(_ret_prompt_kernel, chunk=chunk, n_chunks=tp // chunk),
        grid=(bp, h),
        in_specs=[
            pl.BlockSpec((tp, dk), lambda b, hh: (b, hh)),
            pl.BlockSpec((tp, dk), lambda b, hh: (b, h + hh)),
            pl.BlockSpec((tp, dv), lambda b, hh: (b, v_blk0 + hh)),
            pl.BlockSpec((tp, dv), lambda b, hh: (b, hh)),
            pl.BlockSpec((1, dv), lambda b, hh: (0, hh)),
            pl.BlockSpec((1, chunk, chunk), lambda b, hh: (hh, 0, 0)),
            pl.BlockSpec((1, chunk, 1), lambda b, hh: (hh, 0, 0)),
            pl.BlockSpec((1, chunk, 1), lambda b, hh: (hh, 0, 0)),
            pl.BlockSpec((1, 1, 1), lambda b, hh: (hh, 0, 0)),
        ],
        out_specs=[
            pl.BlockSpec((tp, dv), lambda b, hh: (b, hh)),
            pl.BlockSpec((1, 1, dk, dv), lambda b, hh: (b, hh, 0, 0)),
        ],
        out_shape=[jax.ShapeDtypeStruct((n, h * dv), BF16),
                   jax.ShapeDtypeStruct((bp, h, dk, dv), F32)],
        scratch_shapes=[pltpu.VMEM((dk, dv), F32)],
        compiler_params=pltpu.CompilerParams(
            dimension_semantics=("arbitrary", "arbitrary"),
            vmem_limit_bytes=_vmem_limit(blocks, _nbytes((dk, dv), F32))),
        name="retention_prompt",
    )(qkv, qkv, qkv, g, gng.reshape(1, h * dv), dmask, qdec, kdec, sdec)


def _ret_sample_kernel(q_ref, k_ref, v_ref, g_ref, gng_ref, s0_ref, dmask_ref, qdec_ref, kdec_ref, sdec_ref,
                       prev_ref, o_ref, s_out_ref, qf_ref, kf_ref, vf_ref, of_ref, *, bb):
    del prev_ref
    dmask = dmask_ref[0]
    qdec = qdec_ref[0]
    kdec = kdec_ref[0]
    sdec = sdec_ref[0]
    gng = gng_ref[...]
    qf_ref[...] = q_ref[...].astype(F32)
    kf_ref[...] = k_ref[...].astype(F32)
    vf_ref[...] = v_ref[...].astype(F32)

    def body(b, carry):
        o, s_new = _retention_chunk(qf_ref[:, b, :].astype(BF16), kf_ref[:, b, :].astype(BF16),
                                    vf_ref[:, b, :].astype(BF16), s0_ref[b, 0], dmask, qdec, kdec, sdec)
        s_out_ref[b, 0] = s_new
        of_ref[:, b, :] = _group_norm_gate(o, g_ref[:, b, :], gng)
        return carry

    lax.fori_loop(0, bb, body, 0, unroll=4)
    o_ref[...] = of_ref[...].astype(o_ref.dtype)


def _ret_sample_call(qkv, g, gng, state, gated, lay, bb=16):
    bs, ts, dk, dv = lay["sub"], lay["t_sample"], lay["dk"], lay["dv"]
    h = RET_HEADS
    n = qkv.shape[0]
    t_blk = lay["n_prompt_rows"] // (bs * ts)
    assert t_blk * bs * ts == lay["n_prompt_rows"]
    dmask, qdec, kdec, sdec = _retention_decays(ts)
    v_blk0 = 2 * h * dk // dv
    blocks = (2 * _nbytes((ts, bb, dk), BF16) + 2 * _nbytes((ts, bb, dv), BF16) + _nbytes((ts, bb, dv), F32)
              + 2 * _nbytes((bb, dk, dv), F32))
    scratch = 2 * _nbytes((ts, bb, dk), F32) + 2 * _nbytes((ts, bb, dv), F32)
    qkv3 = qkv.reshape(n // bs, bs, qkv.shape[1])
    gated3, s_new = pl.pallas_call(
        functools.partial(_ret_sample_kernel, bb=bb),
        grid=(bs // bb, h),
        in_specs=[
            pl.BlockSpec((ts, bb, dk), lambda b, hh: (t_blk, b, hh)),
            pl.BlockSpec((ts, bb, dk), lambda b, hh: (t_blk, b, h + hh)),
            pl.BlockSpec((ts, bb, dv), lambda b, hh: (t_blk, b, v_blk0 + hh)),
            pl.BlockSpec((ts, bb, dv), lambda b, hh: (t_blk, b, hh)),
            pl.BlockSpec((1, dv), lambda b, hh: (0, hh)),
            pl.BlockSpec((bb, 1, dk, dv), lambda b, hh: (b, hh, 0, 0)),
            pl.BlockSpec((1, ts, ts), lambda b, hh: (hh, 0, 0)),
            pl.BlockSpec((1, ts, 1), lambda b, hh: (hh, 0, 0)),
            pl.BlockSpec((1, ts, 1), lambda b, hh: (hh, 0, 0)),
            pl.BlockSpec((1, 1, 1), lambda b, hh: (hh, 0, 0)),
            pl.BlockSpec(memory_space=pl.ANY),
        ],
        out_specs=[
            pl.BlockSpec((ts, bb, dv), lambda b, hh: (t_blk, b, hh)),
            pl.BlockSpec((bb, 1, dk, dv), lambda b, hh: (b, hh, 0, 0)),
        ],
        out_shape=[jax.ShapeDtypeStruct((n // bs, bs, h * dv), gated.dtype),
                   jax.ShapeDtypeStruct((bs, h, dk, dv), F32)],
        scratch_shapes=[pltpu.VMEM((ts, bb, dk), F32), pltpu.VMEM((ts, bb, dk), F32),
                        pltpu.VMEM((ts, bb, dv), F32), pltpu.VMEM((ts, bb, dv), F32)],
        input_output_aliases={10: 0},
        compiler_params=pltpu.CompilerParams(
            dimension_semantics=("arbitrary", "arbitrary"),
            vmem_limit_bytes=_vmem_limit(blocks, scratch)),
        name="retention_sample",
    )(qkv3, qkv3, qkv3, g.reshape(n // bs, bs, h * dv), gng.reshape(1, h * dv), state, dmask, qdec, kdec, sdec,
      gated.reshape(n // bs, bs, h * dv))
    return gated3.reshape(n, h * dv), s_new


def _window_sum_rows(win_ref, halo, r0, rows, win, stride):
    if stride % V7X_SUBLANES == 0:
        acc = win_ref[pl.ds(halo + r0, rows), :]
        for k in range(1, win):
            acc = acc + win_ref[pl.ds(halo - k * stride + r0, rows), :]
        return acc
    assert stride == 1 and halo == 16 and win <= 16
    blk = win_ref[pl.ds(r0, rows + 16), :]
    shift = 1
    while shift < win:
        blk = blk + pltpu.roll(blk, shift, 0)
        shift *= 2
    return blk[16:, :]


def _pool_kernel(ssq_ref, gam_ref, sc_ref, sh_ref, hist_ref, x_ref, gate_ref, w_ref, scale_ref, *rest,
                 tm, sub, stride, tiles_per_seq, hist_len, chunk_rows, d_model):
    o_ref, hist_out_ref, win_ref, m_ref = rest[-4:]
    g = pl.program_id(0)
    i = pl.program_id(1)
    halo = 16 * stride
    tile_in_seq = i % tiles_per_seq

    if tiles_per_seq > 1:
        @pl.when(tile_in_seq != 0)
        def _():
            win_ref[0:halo, :] = win_ref[tm:tm + halo, :]

    @pl.when(tile_in_seq == 0)
    def _():
        win_ref[0:halo, :] = hist_ref[0]

    gam = gam_ref[...]
    for s in range(tm // sub):
        rows = slice(s * sub, (s + 1) * sub)
        ssq = ssq_ref[0, rows, :]
        for c in range(1, ssq_ref.shape[0]):
            ssq = ssq + ssq_ref[c, rows, :]
        rs = lax.rsqrt(ssq / d_model + EPS)
        win_ref[halo + s * sub:halo + (s + 1) * sub, :] = (
            ((x_ref[rows, :] * rs) * gam) * (1.0 + sc_ref[...]) + sh_ref[...])

    for gi, win in enumerate(POOL_WINDOWS):
        @pl.when(g == gi)
        def _(win=win):
            def body(c, carry):
                r0 = pl.multiple_of(c * chunk_rows, chunk_rows)
                cur = win_ref[pl.ds(halo + r0, chunk_rows), :]
                acc = _window_sum_rows(win_ref, halo, r0, chunk_rows, win, stride)
                t = (tile_in_seq * tm + r0 + lax.broadcasted_iota(jnp.int32, (chunk_rows, 1), 0)) // stride
                cnt = jnp.minimum(win, t + 1 + hist_len).astype(F32)
                m_ref[pl.ds(r0, chunk_rows), :] = (acc / cnt - cur).astype(m_ref.dtype)
                return carry

            lax.fori_loop(0, tm // chunk_rows, body, 0)

    y = jnp.dot(m_ref[...], w_ref[0], preferred_element_type=F32) * scale_ref[...]
    gate = gate_ref[...]
    for s in range(tm // sub):
        rows = slice(s * sub, (s + 1) * sub)
        o_ref[rows, :] = x_ref[rows, :] + gate * y[rows, :]
    hist_out_ref[0] = win_ref[tm + stride:tm + halo, :]


def _pool_call(ssq, gam, hist, x, mod, layer, w, scale, x_prev, *, row0, n_rows, n_seq, stride, hist_len, mod_seq0,
               lay, tm=1024):
    n, d = x.shape
    sub = lay["sub"]
    n_g = len(POOL_WINDOWS)
    gc = d // n_g
    n_tiles = n_rows // tm
    tps = n_tiles // n_seq
    blk0 = row0 // tm
    halo = 16 * stride
    n_parts = ssq.shape[0]
    blocks = (2 * _nbytes((tm, gc), F32) + _nbytes((halo, gc), F32) + 3 * _nbytes((sub, gc), F32)
              + _nbytes((gc, gc), BF16) + _nbytes((15 * stride, gc), F32) + n_parts * _nbytes((tm, V7X_LANES), F32))
    scratch = _nbytes((halo + tm, gc), F32) + _nbytes((tm, gc), BF16)
    sh_col0, sc_col0, gate_col0 = 0, n_g, 2 * n_g
    mod_blk = lambda col0: pl.BlockSpec((None, sub, gc), lambda g, i: (layer, mod_seq0 + i // tps, col0 + g))
    in_specs = [
        pl.BlockSpec((n_parts, tm, 1), lambda g, i: (0, blk0 + i, 0)),
        pl.BlockSpec((1, gc), lambda g, i: (0, g)),
        mod_blk(sc_col0),
        mod_blk(sh_col0),
        pl.BlockSpec((1, halo, gc), lambda g, i: (i // tps, 0, g)),
        pl.BlockSpec((tm, gc), lambda g, i: (blk0 + i, g)),
        mod_blk(gate_col0),
        pl.BlockSpec((1, gc, gc), lambda g, i: (g, 0, 0)),
        pl.BlockSpec((1, gc), lambda g, i: (0, g)),
    ]
    args = [ssq, gam.reshape(1, d), mod, mod, hist, x, mod, w, scale.reshape(1, d)]
    aliases = {}
    if x_prev is not None:
        in_specs.append(pl.BlockSpec(memory_space=pl.ANY))
        args.append(x_prev)
        aliases = {len(args) - 1: 0}
    return pl.pallas_call(
        functools.partial(_pool_kernel, tm=tm, sub=sub, stride=stride, tiles_per_seq=tps,
                          hist_len=hist_len, chunk_rows=64, d_model=d),
        grid=(n_g, n_tiles),
        in_specs=in_specs,
        out_specs=[
            pl.BlockSpec((tm, gc), lambda g, i: (blk0 + i, g)),
            pl.BlockSpec((1, 15 * stride, gc), lambda g, i: (i // tps, 0, g)),
        ],
        out_shape=[jax.ShapeDtypeStruct((n, d), F32),
                   jax.ShapeDtypeStruct((n_seq, 15 * stride, d), F32)],
        scratch_shapes=[pltpu.VMEM((halo + tm, gc), F32), pltpu.VMEM((tm, gc), BF16)],
        input_output_aliases=aliases,
        compiler_params=pltpu.CompilerParams(
            dimension_semantics=("arbitrary", "arbitrary"),
            vmem_limit_bytes=_vmem_limit(blocks, scratch)),
        name="pool_mixer_s%d" % stride,
    )(*args)


def _rope_tables(pos, dk):
    half = dk // 2
    inv = 1.0 / (ROPE_BASE ** (jnp.arange(half, dtype=F32) / half))
    ang = pos[:, None] * inv[None, :]
    return jnp.cos(ang), jnp.sin(ang)


def kernel(x_prompt, x_sample, c_prompt, c_sample, state_ret, state_pool, norm_mix_g, norm_ffn_g, ada_w, ada_b,
           ret_w_in, ret_gn_g, ret_w_out, pool_w, pool_scale, ffn_w_in, ffn_w_out, final_norm_g):
    bp, tp, d = x_prompt.shape
    bs, ts, _ = x_sample.shape
    depth = ada_w.shape[0]
    dk = state_ret.shape[3]
    dv = state_ret.shape[4]
    n_hist = state_pool.shape[2]
    assert bp <= C_PAD and n_hist == max(POOL_WINDOWS) - 1 and depth == 2
    n_p = bp * tp
    n_s = bs * ts
    lay = dict(n_prompt=bp, t_prompt=tp, n_prompt_rows=n_p, n_sample_rows=n_s, sub=bs, t_sample=ts, dk=dk, dv=dv)

    xp = x_prompt.reshape(n_p, d)
    xs = x_sample.transpose(1, 0, 2).reshape(n_s, d)
    c_all = jnp.concatenate([c_prompt, jnp.zeros((C_PAD - bp, d), F32), c_sample], axis=0)
    mod = _ada_call(c_all, ada_w, ada_b, bp, bs)

    cos_p, sin_p = _rope_tables(jnp.arange(tp, dtype=F32), dk)
    cos_s, sin_s = _rope_tables(float(PAST_LEN) + jnp.arange(ts, dtype=F32), dk)
    cos = jnp.concatenate([jnp.tile(cos_p, (bp, 1)), jnp.repeat(cos_s, bs, axis=0)], axis=0)
    sin = jnp.concatenate([jnp.tile(sin_p, (bp, 1)), jnp.repeat(sin_s, bs, axis=0)], axis=0)

    h0 = _norm_call((xp, xs), norm_mix_g[0], mod, 0, 1, 0, lay, BF16)
    n_qkv = RET_HEADS * (2 * dk + dv)
    qkv, w_ret_out = _ret_proj_call(h0, ret_w_in, 0, 0, n_qkv, BF16, lay, (ret_w_out, 0, (256, d)),
                                    rope_tables=(cos, sin))
    g, w_ffn_in0 = _ret_proj_call(h0, ret_w_in, 0, n_qkv, RET_HEADS * dv, F32, lay, (ffn_w_in, 0, (d, 512)))
    gated, s_ret_p = _ret_prompt_call(qkv, g, ret_gn_g[0], lay)
    gated, s_ret_s = _ret_sample_call(qkv, g, ret_gn_g[0], state_ret[0], gated, lay)
    x, w_ffn_out0, _ = _out_proj_call(gated, w_ret_out, (xp, xs), mod, 0, 2, lay, side=(ffn_w_out, 0, (256, d)))
    hmid, w_ffn_in1 = _ffn_in_call(x, norm_ffn_g[0], mod, 0, w_ffn_in0, lay, side=(ffn_w_in, 1, (d, 256)))
    x, w_ffn_out1, ssq = _out_proj_call(hmid, w_ffn_out0, (x,), mod, 0, 5, lay, side=(ffn_w_out, 1, (256, d)),
                                        emit_ssq=True)

    pw = pool_w[0].astype(BF16)
    hist_p = jnp.zeros((bp, 16, d), F32)
    hist_s = jnp.concatenate([jnp.zeros((1, bs, d), F32), state_pool[0].transpose(1, 0, 2)], axis=0)
    x1, nh_p = _pool_call(ssq, norm_mix_g[1], hist_p, x, mod, 1, pw, pool_scale[0], None, row0=0, n_rows=n_p,
                          n_seq=bp, stride=1, hist_len=0, mod_seq0=0, lay=lay)
    x, nh_s = _pool_call(ssq, norm_mix_g[1], hist_s.reshape(1, 16 * bs, d), x, mod, 1, pw, pool_scale[0], x1,
                         row0=n_p, n_rows=n_s, n_seq=1, stride=bs, hist_len=min(PAST_LEN, n_hist), mod_seq0=bp,
                         lay=lay)
    hmid, _ = _ffn_in_call(x, norm_ffn_g[1], mod, 1, w_ffn_in1, lay)
    x, _, _ = _out_proj_call(hmid, w_ffn_out1, (x,), mod, 1, 5, lay)

    y_p, y_s = _norm_call((x,), final_norm_g, None, 0, 0, 0, lay, F32, split_out=True)
    y_prompt = y_p.reshape(bp, tp, d)
    y_sample = y_s.reshape(ts, bs, d).transpose(1, 0, 2)
    state_pool_sample = nh_s.reshape(n_hist, bs, d).transpose(1, 0, 2)
    return (y_prompt, y_sample, s_ret_p[None], nh_p[None], s_ret_s[None], state_pool_sample[None])
```

```python
import functools

import jax
import jax.numpy as jnp
from jax import lax
from jax.experimental import pallas as pl
from jax.experimental.pallas import tpu as pltpu

F32 = jnp.float32
BF16 = jnp.bfloat16

RET_HEADS = 8
RET_CHUNK = 128
ROPE_BASE = 10000.0
POOL_WINDOWS = (2, 4, 8, 16)
N_ADA = 6
EPS = 1e-6
PAST_LEN = 16384

V7X_VMEM_BYTES = 64 * 1024 * 1024
V7X_LANES = 128
V7X_SUBLANES = 8
NORM_ROWS = 16
C_PAD = 8


def _vmem_limit(block_bytes, scratch_bytes=0):
    need = 2 * block_bytes + scratch_bytes + 12 * 1024 * 1024
    return int(min(need, V7X_VMEM_BYTES - 6 * 1024 * 1024))


def _nbytes(shape, dtype):
    n = 1
    for s in shape:
        n *= s
    return n * jnp.dtype(dtype).itemsize


def _silu(x):
    return x / (1.0 + jnp.exp(-x))


def _ada_kernel(c_ref, w_ref, b_ref, o_ref, *, n_prompt, sub):
    cs = _silu(c_ref[...])
    res = jnp.dot(cs.astype(BF16), w_ref[0].astype(BF16), preferred_element_type=F32) + b_ref[0]
    tn = res.shape[-1]
    for b in range(n_prompt):
        o_ref[0, b * sub:(b + 1) * sub, :] = jnp.broadcast_to(res[b:b + 1, :], (sub, tn))
    o_ref[0, n_prompt * sub:, :] = res[C_PAD:C_PAD + sub, :]


def _ada_call(c_all, ada_w, ada_b, n_prompt, sub, tn=1024):
    depth, d, n6 = ada_w.shape
    rows = c_all.shape[0]
    out_rows = (n_prompt + 1) * sub
    blocks = (_nbytes((rows, d), F32) + _nbytes((d, tn), F32) + _nbytes((1, tn), F32)
              + _nbytes((out_rows, tn), F32))
    return pl.pallas_call(
        functools.partial(_ada_kernel, n_prompt=n_prompt, sub=sub),
        grid=(depth, n6 // tn),
        in_specs=[
            pl.BlockSpec((rows, d), lambda l, j: (0, 0)),
            pl.BlockSpec((1, d, tn), lambda l, j: (l, 0, j)),
            pl.BlockSpec((1, 1, tn), lambda l, j: (l, 0, j)),
        ],
        out_specs=pl.BlockSpec((1, out_rows, tn), lambda l, j: (l, 0, j)),
        out_shape=jax.ShapeDtypeStruct((depth, out_rows, n6), F32),
        compiler_params=pltpu.CompilerParams(
            dimension_semantics=("arbitrary", "arbitrary"),
            vmem_limit_bytes=_vmem_limit(blocks, _nbytes((d, tn), BF16))),
        name="ada_mod",
    )(c_all, ada_w, ada_b.reshape(depth, 1, n6))


def _norm_mod_rows(x_ref, gam_ref, sc_ref, sh_ref, out_ref, *, tm, sub):
    gam = gam_ref[...]
    per_sub = sub // NORM_ROWS

    def body(c, carry):
        r0 = pl.multiple_of(c * NORM_ROWS, NORM_ROWS)
        x = x_ref[pl.ds(r0, NORM_ROWS), :]
        hn = (x * lax.rsqrt(jnp.mean(x * x, axis=-1, keepdims=True) + EPS)) * gam
        if sc_ref is not None:
            m0 = pl.multiple_of((c % per_sub) * NORM_ROWS, NORM_ROWS)
            hn = hn * (1.0 + sc_ref[pl.ds(m0, NORM_ROWS), :]) + sh_ref[pl.ds(m0, NORM_ROWS), :]
        out_ref[pl.ds(r0, NORM_ROWS), :] = hn.astype(out_ref.dtype)
        return carry

    lax.fori_loop(0, tm // NORM_ROWS, body, 0, unroll=4)


def _mod_index(i, n_prompt_tiles, tiles_per_seq, n_prompt):
    return jnp.where(i < n_prompt_tiles, i // tiles_per_seq, n_prompt)


def _tiling(lay, tm):
    npt = lay["n_prompt_rows"] // tm
    midx = functools.partial(_mod_index, n_prompt_tiles=npt, tiles_per_seq=lay["t_prompt"] // tm,
                             n_prompt=lay["n_prompt"])
    return npt, midx


def _norm_kernel(*refs, tm, sub, n_in, n_out, has_mod, npt):
    x_refs = refs[:n_in]
    gam_ref = refs[n_in]
    sc_ref, sh_ref = (refs[n_in + 1], refs[n_in + 2]) if has_mod else (None, None)
    o_refs = refs[len(refs) - n_out:]
    if n_in == 1 and n_out == 1:
        _norm_mod_rows(x_refs[0], gam_ref, sc_ref, sh_ref, o_refs[0], tm=tm, sub=sub)
        return
    i = pl.program_id(0)

    @pl.when(i < npt)
    def _():
        _norm_mod_rows(x_refs[0], gam_ref, sc_ref, sh_ref, o_refs[0], tm=tm, sub=sub)

    @pl.when(i >= npt)
    def _():
        _norm_mod_rows(x_refs[-1], gam_ref, sc_ref, sh_ref, o_refs[-1], tm=tm, sub=sub)


def _norm_call(xs, gam, mod, layer, col_sc, col_sh, lay, out_dtype, split_out=False, tm=512):
    d = xs[0].shape[1]
    sub = lay["sub"]
    n_p, n_s = lay["n_prompt_rows"], lay["n_sample_rows"]
    npt, midx = _tiling(lay, tm)
    prompt_blk = lambda i: (jnp.minimum(i, npt - 1), 0)
    sample_blk = lambda i: (jnp.maximum(i - npt, 0), 0)
    whole_blk = lambda i: (i, 0)
    in_specs = ([pl.BlockSpec((tm, d), whole_blk)] if len(xs) == 1
                else [pl.BlockSpec((tm, d), prompt_blk), pl.BlockSpec((tm, d), sample_blk)])
    in_specs.append(pl.BlockSpec((1, d), lambda i: (0, 0)))
    args = list(xs) + [gam.reshape(1, d)]
    if mod is not None:
        in_specs += [pl.BlockSpec((None, sub, d), lambda i: (layer, midx(i), col_sc)),
                     pl.BlockSpec((None, sub, d), lambda i: (layer, midx(i), col_sh))]
        args += [mod, mod]
    if split_out:
        out_specs = [pl.BlockSpec((tm, d), prompt_blk), pl.BlockSpec((tm, d), sample_blk)]
        out_shape = [jax.ShapeDtypeStruct((n_p, d), out_dtype), jax.ShapeDtypeStruct((n_s, d), out_dtype)]
    else:
        out_specs = pl.BlockSpec((tm, d), whole_blk)
        out_shape = jax.ShapeDtypeStruct((n_p + n_s, d), out_dtype)
    n_out = 2 if split_out else 1
    blocks = (len(xs) + n_out) * _nbytes((tm, d), F32) + 2 * _nbytes((sub, d), F32)
    return pl.pallas_call(
        functools.partial(_norm_kernel, tm=tm, sub=sub, n_in=len(xs), n_out=n_out,
                          has_mod=mod is not None, npt=npt),
        grid=((n_p + n_s) // tm,),
        in_specs=in_specs,
        out_specs=out_specs,
        out_shape=out_shape,
        compiler_params=pltpu.CompilerParams(
            dimension_semantics=("arbitrary",), vmem_limit_bytes=_vmem_limit(blocks)),
        name="row_norm",
    )(*args)


def _load_weight(w_ref, copy_ref):
    w = w_ref[...]
    if copy_ref is not None:
        w = w.astype(BF16)
        copy_ref[...] = w
    return w


class _SideCast:
    def __init__(self, src3, layer, block, n_i, n_j):
        _, r, c = src3.shape
        br, bc = block
        ncb = c // bc
        n_blocks = (r // br) * ncb
        assert n_blocks <= n_i * n_j and r % br == 0 and c % bc == 0

        def blk(i, j):
            t = jnp.minimum(i * n_j + j, n_blocks - 1)
            return t // ncb, t % ncb

        self.arg = src3
        self.in_spec = pl.BlockSpec((None, br, bc), lambda i, j: (layer, *blk(i, j)))
        self.out_spec = pl.BlockSpec((br, bc), blk)
        self.out_shape = jax.ShapeDtypeStruct((r, c), BF16)
        self.block_bytes = _nbytes(block, F32) + _nbytes(block, BF16)


def _ret_proj_kernel(h_ref, w_ref, cos_ref, sin_ref, *rest, tn, d_qk, head_dk, k_scale, cast_w, has_prev, n_side):
    n_in = (1 if has_prev else 0) + n_side
    o_ref = rest[n_in]
    for s in range(n_side):
        rest[len(rest) - n_side + s][...] = rest[n_in - n_side + s][...].astype(BF16)
    w = _load_weight(w_ref, rest[n_in + 1] if cast_w else None)
    acc = jnp.dot(h_ref[...], w, preferred_element_type=F32)
    col0 = pl.program_id(1) * tn
    is_qk = col0 < 2 * d_qk
    scale = jnp.where(jnp.logical_and(is_qk, col0 >= d_qk), k_scale, 1.0).astype(F32)
    cs = jnp.where(is_qk, cos_ref[...], 1.0) * scale
    sn = jnp.where(is_qk, sin_ref[...], 0.0) * scale
    half = head_dk // 2
    for hh in range(tn // head_dk):
        x1 = acc[:, hh * head_dk:hh * head_dk + half]
        x2 = acc[:, hh * head_dk + half:(hh + 1) * head_dk]
        o_ref[:, hh * head_dk:hh * head_dk + half] = (x1 * cs - x2 * sn).astype(o_ref.dtype)
        o_ref[:, hh * head_dk + half:(hh + 1) * head_dk] = (x1 * sn + x2 * cs).astype(o_ref.dtype)


def _ret_proj_call(h, w3, w_layer, lay, sides, rope_tables, tm=1024, tn=1024, tn_first=512):
    n, d = h.shape
    n_cols = w3.shape[2]
    d_qk = RET_HEADS * lay["dk"]
    half = lay["dk"] // 2

    def run(i0, n_i, tn_, weights, cast_w, prev, side_jobs):
        w_spec = (pl.BlockSpec((None, d, tn_), lambda i, j: (w_layer, 0, j)) if cast_w
                  else pl.BlockSpec((d, tn_), lambda i, j: (0, j)))
        in_specs = [pl.BlockSpec((tm, d), lambda i, j: (i + i0, 0)), w_spec,
                    pl.BlockSpec((tm, half), lambda i, j: (i + i0, 0)),
                    pl.BlockSpec((tm, half), lambda i, j: (i + i0, 0))]
        args = [h, weights, *rope_tables]
        aliases = {}
        if prev is not None:
            in_specs.append(pl.BlockSpec(memory_space=pl.ANY))
            args.append(prev)
            aliases = {len(args) - 1: 0}
        out_specs = [pl.BlockSpec((tm, tn_), lambda i, j: (i + i0, j))]
        out_shape = [jax.ShapeDtypeStruct((n, n_cols), BF16)]
        w_bytes = _nbytes((d, tn_), BF16)
        if cast_w:
            out_specs.append(pl.BlockSpec((d, tn_), lambda i, j: (0, j)))
            out_shape.append(jax.ShapeDtypeStruct((d, n_cols), BF16))
            w_bytes = _nbytes((d, tn_), F32) + 2 * _nbytes((d, tn_), BF16)
        casts = [_SideCast(*job, n_i, n_cols // tn_) for job in side_jobs]
        in_specs += [sc.in_spec for sc in casts]
        args += [sc.arg for sc in casts]
        out_specs += [sc.out_spec for sc in casts]
        out_shape += [sc.out_shape for sc in casts]
        w_bytes += sum(sc.block_bytes for sc in casts)
        blocks = _nbytes((tm, d), BF16) + w_bytes + 2 * _nbytes((tm, half), F32) + _nbytes((tm, tn_), BF16)
        return pl.pallas_call(
            functools.partial(_ret_proj_kernel, tn=tn_, d_qk=d_qk, head_dk=lay["dk"],
                              k_scale=float(lay["dk"]) ** -0.5, cast_w=cast_w,
                              has_prev=prev is not None, n_side=len(casts)),
            grid=(n_i, n_cols // tn_),
            in_specs=in_specs,
            out_specs=out_specs,
            out_shape=out_shape,
            input_output_aliases=aliases,
            compiler_params=pltpu.CompilerParams(
                dimension_semantics=("arbitrary", "arbitrary"),
                vmem_limit_bytes=_vmem_limit(blocks, _nbytes((tm, tn_), F32))),
            name="ret_in_proj" + ("_first" if cast_w else ""),
        )(*args)

    out0, w_bf = run(0, 1, tn_first, w3, True, None, [])
    out, *side_w = run(1, n // tm - 1, tn, w_bf, False, out0, sides)
    return out, side_w


def _ffn_in_kernel(x_ref, gam_ref, sc_ref, sh_ref, wg_ref, wu_ref, *rest, tm, sub, has_side):
    if has_side:
        side_in, o_ref, side_out, h_ref = rest
        side_out[...] = side_in[...].astype(BF16)
    else:
        o_ref, h_ref = rest

    @pl.when(pl.program_id(1) == 0)
    def _():
        _norm_mod_rows(x_ref, gam_ref, sc_ref, sh_ref, h_ref, tm=tm, sub=sub)

    h = h_ref[...]
    gate = jnp.dot(h, wg_ref[...], preferred_element_type=F32)
    up = jnp.dot(h, wu_ref[...], preferred_element_type=F32)
    o_ref[...] = (_silu(gate) * up).astype(o_ref.dtype)


def _ffn_in_call(x, gam, mod, layer, w, lay, side=None, tm=1024, tn=512):
    n, d = x.shape
    sub = lay["sub"]
    d_ff = w.shape[1] // 2
    n_j = d_ff // tn
    _, midx = _tiling(lay, tm)
    in_specs = [
        pl.BlockSpec((tm, d), lambda i, j: (i, 0)),
        pl.BlockSpec((1, d), lambda i, j: (0, 0)),
        pl.BlockSpec((None, sub, d), lambda i, j: (layer, midx(i), 4)),
        pl.BlockSpec((None, sub, d), lambda i, j: (layer, midx(i), 3)),
        pl.BlockSpec((d, tn), lambda i, j: (0, j)),
        pl.BlockSpec((d, tn), lambda i, j: (0, j + n_j)),
    ]
    args = [x, gam.reshape(1, d), mod, mod, w, w]
    out_specs = [pl.BlockSpec((tm, tn), lambda i, j: (i, j))]
    out_shape = [jax.ShapeDtypeStruct((n, d_ff), BF16)]
    blocks = (_nbytes((tm, d), F32) + 2 * _nbytes((sub, d), F32) + 2 * _nbytes((d, tn), BF16)
              + _nbytes((tm, tn), BF16))
    if side is not None:
        sc = _SideCast(*side, n // tm, n_j)
        in_specs.append(sc.in_spec)
        args.append(sc.arg)
        out_specs.append(sc.out_spec)
        out_shape.append(sc.out_shape)
        blocks += sc.block_bytes
    res = pl.pallas_call(
        functools.partial(_ffn_in_kernel, tm=tm, sub=sub, has_side=side is not None),
        grid=(n // tm, n_j),
        in_specs=in_specs,
        out_specs=out_specs,
        out_shape=out_shape,
        scratch_shapes=[pltpu.VMEM((tm, d), BF16)],
        compiler_params=pltpu.CompilerParams(
            dimension_semantics=("arbitrary", "arbitrary"),
            vmem_limit_bytes=_vmem_limit(blocks, _nbytes((tm, d), BF16) + 2 * _nbytes((tm, tn), F32))),
        name="ffn_in_swiglu",
    )(*args)
    return res[0], (res[1] if side is not None else None)


def _out_proj_kernel(a_ref, w_ref, *refs, tm, sub, npt, n_x, has_side, emit_ssq):
    x_refs, gate_ref = refs[:n_x], refs[n_x]
    rest = list(refs[n_x + 1:])
    ssq_ref = rest.pop() if emit_ssq else None
    if has_side:
        side_in, o_ref, side_out = rest
        side_out[...] = side_in[...].astype(BF16)
    else:
        (o_ref,) = rest
    acc = jnp.dot(a_ref[...], w_ref[...], preferred_element_type=F32)
    gate = gate_ref[...]
    is_prompt = pl.program_id(0) < npt
    for s in range(tm // sub):
        rows = slice(s * sub, (s + 1) * sub)
        x = x_refs[0][rows, :]
        if n_x == 2:
            x = jnp.where(is_prompt, x, x_refs[1][rows, :])
        x_new = x + gate * acc[rows, :]
        o_ref[rows, :] = x_new
        if emit_ssq:
            ssq_ref[rows, :] = jnp.sum(x_new * x_new, axis=-1, keepdims=True)


def _out_proj_call(a, w, xs, mod, layer, gate_col, lay, side=None, emit_ssq=False, tm=1024, tn=512):
    n, k = a.shape
    d = w.shape[1]
    sub = lay["sub"]
    npt, midx = _tiling(lay, tm)
    n_j = d // tn
    x_specs = ([pl.BlockSpec((tm, tn), lambda i, j: (i, j))] if len(xs) == 1
               else [pl.BlockSpec((tm, tn), lambda i, j: (jnp.minimum(i, npt - 1), j)),
                     pl.BlockSpec((tm, tn), lambda i, j: (jnp.maximum(i - npt, 0), j))])
    in_specs = ([pl.BlockSpec((tm, k), lambda i, j: (i, 0)), pl.BlockSpec((k, tn), lambda i, j: (0, j))] + x_specs
                + [pl.BlockSpec((None, sub, tn), lambda i, j: (layer, midx(i), gate_col * n_j + j))])
    args = [a, w, *xs, mod]
    out_specs = [pl.BlockSpec((tm, tn), lambda i, j: (i, j))]
    out_shape = [jax.ShapeDtypeStruct((n, d), F32)]
    blocks = (_nbytes((tm, k), BF16) + _nbytes((k, tn), BF16) + (1 + len(xs)) * _nbytes((tm, tn), F32)
              + _nbytes((sub, tn), F32))
    if side is not None:
        sc = _SideCast(*side, n // tm, n_j)
        in_specs.append(sc.in_spec)
        args.append(sc.arg)
        out_specs.append(sc.out_spec)
        out_shape.append(sc.out_shape)
        blocks += sc.block_bytes
    if emit_ssq:
        out_specs.append(pl.BlockSpec((None, tm, 1), lambda i, j: (j, i, 0)))
        out_shape.append(jax.ShapeDtypeStruct((n_j, n, 1), F32))
        blocks += _nbytes((tm, V7X_LANES), F32)
    res = pl.pallas_call(
        functools.partial(_out_proj_kernel, tm=tm, sub=sub, npt=npt, n_x=len(xs), has_side=side is not None,
                          emit_ssq=emit_ssq),
        grid=(n // tm, n_j),
        in_specs=in_specs,
        out_specs=out_specs,
        out_shape=out_shape,
        compiler_params=pltpu.CompilerParams(
            dimension_semantics=("arbitrary", "arbitrary"),
            vmem_limit_bytes=_vmem_limit(blocks, _nbytes((tm, tn), F32))),
        name="out_proj_residual",
    )(*args)
    return res[0], (res[1] if side is not None else None), (res[-1] if emit_ssq else None)


def _retention_decays(c):
    lg = jnp.log1p(-jnp.exp2(-5.0 - jnp.arange(RET_HEADS, dtype=F32)))
    idx = jnp.arange(c, dtype=F32)
    diff = idx[:, None] - idx[None, :]
    causal = diff >= 0
    dmask = jnp.where(causal[None], jnp.exp(jnp.where(causal, diff, 0.0)[None] * lg[:, None, None]), 0.0)
    qdec = jnp.exp((idx + 1.0)[None, :] * lg[:, None])
    kdec = jnp.exp((c - 1.0 - idx)[None, :] * lg[:, None])
    sdec = jnp.exp(c * lg)
    return dmask, qdec[:, :, None], kdec[:, :, None], sdec[:, None, None]


def _group_norm_gate(o, g, gng):
    mu = jnp.mean(o, axis=-1, keepdims=True)
    dlt = o - mu
    var = jnp.mean(dlt * dlt, axis=-1, keepdims=True)
    return _silu(g) * (dlt * lax.rsqrt(var + EPS) * gng)


_NT = (((1,), (1,)), ((), ()))
_TN = (((0,), (0,)), ((), ()))


def _retention_chunk(q, k, v, s_prev, dmask, qdec, kdec, sdec):
    scores = lax.dot_general(q, k, _NT, preferred_element_type=F32) * dmask
    o = (jnp.dot(scores.astype(BF16), v, preferred_element_type=F32)
         + qdec * jnp.dot(q, s_prev.astype(BF16), preferred_element_type=F32))
    kd = (k.astype(F32) * kdec).astype(BF16)
    s_new = sdec * s_prev + lax.dot_general(kd, v, _TN, preferred_element_type=F32)
    return o, s_new


def _ret_prompt_kernel(q_ref, k_ref, v_ref, g_ref, gng_ref, dmask_ref, qdec_ref, kdec_ref, sdec_ref,
                       o_ref, s_out_ref, s_ref, *, chunk, n_chunks):
    s_ref[...] = jnp.zeros_like(s_ref)
    dmask = dmask_ref[0]
    qdec = qdec_ref[0]
    kdec = kdec_ref[0]
    sdec = sdec_ref[0]
    gng = gng_ref[...]

    def body(c, carry):
        rows = pl.ds(pl.multiple_of(c * chunk, chunk), chunk)
        o, s_new = _retention_chunk(q_ref[rows, :], k_ref[rows, :], v_ref[rows, :], s_ref[...],
                                    dmask, qdec, kdec, sdec)
        s_ref[...] = s_new
        o_ref[rows, :] = _group_norm_gate(o, g_ref[rows, :].astype(F32), gng).astype(o_ref.dtype)
        return carry

    lax.fori_loop(0, n_chunks, body, 0, unroll=2)
    s_out_ref[0, 0] = s_ref[...]


def _ret_prompt_call(qkvg, gng, lay, chunk=2 * RET_CHUNK):
    n = qkvg.shape[0]
    bp, tp, dk, dv = lay["n_prompt"], lay["t_prompt"], lay["dk"], lay["dv"]
    h = RET_HEADS
    dmask, qdec, kdec, sdec = _retention_decays(chunk)
    v_blk0 = 2 * h * dk // dv
    g_blk0 = v_blk0 + h
    blocks = (2 * _nbytes((tp, dk), BF16) + 3 * _nbytes((tp, dv), BF16)
              + 2 * _nbytes((dk, dv), F32) + _nbytes((chunk, chunk), F32))
    return pl.pallas_call(
        functools.partial(_ret_prompt_kernel, chunk=chunk, n_chunks=tp // chunk),
        grid=(bp, h),
        in_specs=[
            pl.BlockSpec((tp, dk), lambda b, hh: (b, hh)),
            pl.BlockSpec((tp, dk), lambda b, hh: (b, h + hh)),
            pl.BlockSpec((tp, dv), lambda b, hh: (b, v_blk0 + hh)),
            pl.BlockSpec((tp, dv), lambda b, hh: (b, g_blk0 + hh)),
            pl.BlockSpec((1, dv), lambda b, hh: (0, hh)),
            pl.BlockSpec((1, chunk, chunk), lambda b, hh: (hh, 0, 0)),
            pl.BlockSpec((1, chunk, 1), lambda b, hh: (hh, 0, 0)),
            pl.BlockSpec((1, chunk, 1), lambda b, hh: (hh, 0, 0)),
            pl.BlockSpec((1, 1, 1), lambda b, hh: (hh, 0, 0)),
        ],
        out_specs=[
            pl.BlockSpec((tp, dv), lambda b, hh: (b, hh)),
            pl.BlockSpec((1, 1, dk, dv), lambda b, hh: (b, hh, 0, 0)),
        ],
        out_shape=[jax.ShapeDtypeStruct((n, h * dv), BF16),
                   jax.ShapeDtypeStruct((bp, h, dk, dv), F32)],
        scratch_shapes=[pltpu.VMEM((dk, dv), F32)],
        compiler_params=pltpu.CompilerParams(
            dimension_semantics=("arbitrary", "arbitrary"),
            vmem_limit_bytes=_vmem_limit(blocks, _nbytes((dk, dv), F32))),
        name="retention_prompt",
    )(qkvg, qkvg, qkvg, qkvg, gng.reshape(1, h * dv), dmask, qdec, kdec, sdec)


def _ret_sample_kernel(q_ref, k_ref, v_ref, g_ref, gng_ref, s0_ref, dmask_ref, qdec_ref, kdec_ref, sdec_ref,
                       prev_ref, o_ref, s_out_ref, qf_ref, kf_ref, vf_ref, gf_ref, of_ref, *, bb):
    del prev_ref
    dmask = dmask_ref[0]
    qdec = qdec_ref[0]
    kdec = kdec_ref[0]
    sdec = sdec_ref[0]
    gng = gng_ref[...]
    qf_ref[...] = q_ref[...].astype(F32)
    kf_ref[...] = k_ref[...].astype(F32)
    vf_ref[...] = v_ref[...].astype(F32)
    gf_ref[...] = g_ref[...].astype(F32)

    def body(b, carry):
        o, s_new = _retention_chunk(qf_ref[:, b, :].astype(BF16), kf_ref[:, b, :].astype(BF16),
                                    vf_ref[:, b, :].astype(BF16), s0_ref[b, 0], dmask, qdec, kdec, sdec)
        s_out_ref[b, 0] = s_new
        of_ref[:, b, :] = _group_norm_gate(o, gf_ref[:, b, :], gng)
        return carry

    lax.fori_loop(0, bb, body, 0, unroll=4)
    o_ref[...] = of_ref[...].astype(o_ref.dtype)


def _ret_sample_call(qkvg, gng, state, gated, lay, bb=16):
    bs, ts, dk, dv = lay["sub"], lay["t_sample"], lay["dk"], lay["dv"]
    h = RET_HEADS
    n = qkvg.shape[0]
    t_blk = lay["n_prompt_rows"] // (bs * ts)
    assert t_blk * bs * ts == lay["n_prompt_rows"]
    dmask, qdec, kdec, sdec = _retention_decays(ts)
    v_blk0 = 2 * h * dk // dv
    g_blk0 = v_blk0 + h
    blocks = (2 * _nbytes((ts, bb, dk), BF16) + 3 * _nbytes((ts, bb, dv), BF16) + 2 * _nbytes((bb, dk, dv), F32))
    scratch = 2 * _nbytes((ts, bb, dk), F32) + 3 * _nbytes((ts, bb, dv), F32)
    qkv3 = qkvg.reshape(n // bs, bs, qkvg.shape[1])
    gated3, s_new = pl.pallas_call(
        functools.partial(_ret_sample_kernel, bb=bb),
        grid=(bs // bb, h),
        in_specs=[
            pl.BlockSpec((ts, bb, dk), lambda b, hh: (t_blk, b, hh)),
            pl.BlockSpec((ts, bb, dk), lambda b, hh: (t_blk, b, h + hh)),
            pl.BlockSpec((ts, bb, dv), lambda b, hh: (t_blk, b, v_blk0 + hh)),
            pl.BlockSpec((ts, bb, dv), lambda b, hh: (t_blk, b, g_blk0 + hh)),
            pl.BlockSpec((1, dv), lambda b, hh: (0, hh)),
            pl.BlockSpec((bb, 1, dk, dv), lambda b, hh: (b, hh, 0, 0)),
            pl.BlockSpec((1, ts, ts), lambda b, hh: (hh, 0, 0)),
            pl.BlockSpec((1, ts, 1), lambda b, hh: (hh, 0, 0)),
            pl.BlockSpec((1, ts, 1), lambda b, hh: (hh, 0, 0)),
            pl.BlockSpec((1, 1, 1), lambda b, hh: (hh, 0, 0)),
            pl.BlockSpec(memory_space=pl.ANY),
        ],
        out_specs=[
            pl.BlockSpec((ts, bb, dv), lambda b, hh: (t_blk, b, hh)),
            pl.BlockSpec((bb, 1, dk, dv), lambda b, hh: (b, hh, 0, 0)),
        ],
        out_shape=[jax.ShapeDtypeStruct((n // bs, bs, h * dv), gated.dtype),
                   jax.ShapeDtypeStruct((bs, h, dk, dv), F32)],
        scratch_shapes=[pltpu.VMEM((ts, bb, dk), F32), pltpu.VMEM((ts, bb, dk), F32),
                        pltpu.VMEM((ts, bb, dv), F32), pltpu.VMEM((ts, bb, dv), F32),
                        pltpu.VMEM((ts, bb, dv), F32)],
        input_output_aliases={10: 0},
        compiler_params=pltpu.CompilerParams(
            dimension_semantics=("arbitrary", "arbitrary"),
            vmem_limit_bytes=_vmem_limit(blocks, scratch)),
        name="retention_sample",
    )(qkv3, qkv3, qkv3, qkv3, gng.reshape(1, h * dv), state, dmask, qdec, kdec, sdec,
      gated.reshape(n // bs, bs, h * dv))
    return gated3.reshape(n, h * dv), s_new


def _window_sum_rows(win_ref, halo, r0, rows, win, stride):
    if stride % V7X_SUBLANES == 0:
        acc = win_ref[pl.ds(halo + r0, rows), :]
        for k in range(1, win):
            acc = acc + win_ref[pl.ds(halo - k * stride + r0, rows), :]
        return acc
    assert stride == 1 and halo == 16 and win <= 16
    blk = win_ref[pl.ds(r0, rows + 16), :]
    shift = 1
    while shift < win:
        blk = blk + pltpu.roll(blk, shift, 0)
        shift *= 2
    return blk[16:, :]


def _pool_kernel(ssq_ref, gam_ref, sc_ref, sh_ref, hist_ref, x_ref, gate_ref, w_ref, scale_ref, *rest,
                 tm, sub, stride, tiles_per_seq, hist_len, chunk_rows, d_model):
    o_ref, hist_out_ref, win_ref, m_ref = rest[-4:]
    g = pl.program_id(0)
    i = pl.program_id(1)
    halo = 16 * stride
    tile_in_seq = i % tiles_per_seq

    if tiles_per_seq > 1:
        @pl.when(tile_in_seq != 0)
        def _():
            win_ref[0:halo, :] = win_ref[tm:tm + halo, :]

    @pl.when(tile_in_seq == 0)
    def _():
        win_ref[0:halo, :] = hist_ref[0]

    gam = gam_ref[...]
    for s in range(tm // sub):
        rows = slice(s * sub, (s + 1) * sub)
        ssq = ssq_ref[0, rows, :]
        for c in range(1, ssq_ref.shape[0]):
            ssq = ssq + ssq_ref[c, rows, :]
        rs = lax.rsqrt(ssq / d_model + EPS)
        win_ref[halo + s * sub:halo + (s + 1) * sub, :] = (
            ((x_ref[rows, :] * rs) * gam) * (1.0 + sc_ref[...]) + sh_ref[...])

    for gi, win in enumerate(POOL_WINDOWS):
        @pl.when(g == gi)
        def _(win=win):
            def body(c, carry):
                r0 = pl.multiple_of(c * chunk_rows, chunk_rows)
                cur = win_ref[pl.ds(halo + r0, chunk_rows), :]
                acc = _window_sum_rows(win_ref, halo, r0, chunk_rows, win, stride)
                t = (tile_in_seq * tm + r0 + lax.broadcasted_iota(jnp.int32, (chunk_rows, 1), 0)) // stride
                cnt = jnp.minimum(win, t + 1 + hist_len).astype(F32)
                m_ref[pl.ds(r0, chunk_rows), :] = (acc / cnt - cur).astype(m_ref.dtype)
                return carry

            lax.fori_loop(0, tm // chunk_rows, body, 0)

    y = jnp.dot(m_ref[...], w_ref[0], preferred_element_type=F32) * scale_ref[...]
    gate = gate_ref[...]
    for s in range(tm // sub):
        rows = slice(s * sub, (s + 1) * sub)
        o_ref[rows, :] = x_ref[rows, :] + gate * y[rows, :]
    hist_out_ref[0] = win_ref[tm + stride:tm + halo, :]


def _pool_call(ssq, gam, hist, x, mod, layer, w, scale, x_prev, *, row0, n_rows, n_seq, stride, hist_len, mod_seq0,
               lay, tm=1024):
    n, d = x.shape
    sub = lay["sub"]
    n_g = len(POOL_WINDOWS)
    gc = d // n_g
    n_tiles = n_rows // tm
    tps = n_tiles // n_seq
    blk0 = row0 // tm
    halo = 16 * stride
    n_parts = ssq.shape[0]
    blocks = (2 * _nbytes((tm, gc), F32) + _nbytes((halo, gc), F32) + 3 * _nbytes((sub, gc), F32)
              + _nbytes((gc, gc), BF16) + _nbytes((15 * stride, gc), F32) + n_parts * _nbytes((tm, V7X_LANES), F32))
    scratch = _nbytes((halo + tm, gc), F32) + _nbytes((tm, gc), BF16)
    sh_col0, sc_col0, gate_col0 = 0, n_g, 2 * n_g
    mod_blk = lambda col0: pl.BlockSpec((None, sub, gc), lambda g, i: (layer, mod_seq0 + i // tps, col0 + g))
    in_specs = [
        pl.BlockSpec((n_parts, tm, 1), lambda g, i: (0, blk0 + i, 0)),
        pl.BlockSpec((1, gc), lambda g, i: (0, g)),
        mod_blk(sc_col0),
        mod_blk(sh_col0),
        pl.BlockSpec((1, halo, gc), lambda g, i: (i // tps, 0, g)),
        pl.BlockSpec((tm, gc), lambda g, i: (blk0 + i, g)),
        mod_blk(gate_col0),
        pl.BlockSpec((1, gc, gc), lambda g, i: (g, 0, 0)),
        pl.BlockSpec((1, gc), lambda g, i: (0, g)),
    ]
    args = [ssq, gam.reshape(1, d), mod, mod, hist, x, mod, w, scale.reshape(1, d)]
    aliases = {}
    if x_prev is not None:
        in_specs.append(pl.BlockSpec(memory_space=pl.ANY))
        args.append(x_prev)
        aliases = {len(args) - 1: 0}
    return pl.pallas_call(
        functools.partial(_pool_kernel, tm=tm, sub=sub, stride=stride, tiles_per_seq=tps,
                          hist_len=hist_len, chunk_rows=64, d_model=d),
        grid=(n_g, n_tiles),
        in_specs=in_specs,
        out_specs=[
            pl.BlockSpec((tm, gc), lambda g, i: (blk0 + i, g)),
            pl.BlockSpec((1, 15 * stride, gc), lambda g, i: (i // tps, 0, g)),
        ],
        out_shape=[jax.ShapeDtypeStruct((n, d), F32),
                   jax.ShapeDtypeStruct((n_seq, 15 * stride, d), F32)],
        scratch_shapes=[pltpu.VMEM((halo + tm, gc), F32), pltpu.VMEM((tm, gc), BF16)],
        input_output_aliases=aliases,
        compiler_params=pltpu.CompilerParams(
            dimension_semantics=("arbitrary", "arbitrary"),
            vmem_limit_bytes=_vmem_limit(blocks, scratch)),
        name="pool_mixer_s%d" % stride,
    )(*args)


def _rope_tables(pos, dk):
    half = dk // 2
    inv = 1.0 / (ROPE_BASE ** (jnp.arange(half, dtype=F32) / half))
    ang = pos[:, None] * inv[None, :]
    return jnp.cos(ang), jnp.sin(ang)


def kernel(x_prompt, x_sample, c_prompt, c_sample, state_ret, state_pool, norm_mix_g, norm_ffn_g, ada_w, ada_b,
           ret_w_in, ret_gn_g, ret_w_out, pool_w, pool_scale, ffn_w_in, ffn_w_out, final_norm_g):
    bp, tp, d = x_prompt.shape
    bs, ts, _ = x_sample.shape
    depth = ada_w.shape[0]
    dk = state_ret.shape[3]
    dv = state_ret.shape[4]
    n_hist = state_pool.shape[2]
    assert bp <= C_PAD and n_hist == max(POOL_WINDOWS) - 1 and depth == 2
    n_p = bp * tp
    n_s = bs * ts
    lay = dict(n_prompt=bp, t_prompt=tp, n_prompt_rows=n_p, n_sample_rows=n_s, sub=bs, t_sample=ts, dk=dk, dv=dv)

    xp = x_prompt.reshape(n_p, d)
    xs = x_sample.transpose(1, 0, 2).reshape(n_s, d)
    c_all = jnp.concatenate([c_prompt, jnp.zeros((C_PAD - bp, d), F32), c_sample], axis=0)
    mod = _ada_call(c_all, ada_w, ada_b, bp, bs)

    cos_p, sin_p = _rope_tables(jnp.arange(tp, dtype=F32), dk)
    cos_s, sin_s = _rope_tables(float(PAST_LEN) + jnp.arange(ts, dtype=F32), dk)
    cos = jnp.concatenate([jnp.tile(cos_p, (bp, 1)), jnp.repeat(cos_s, bs, axis=0)], axis=0)
    sin = jnp.concatenate([jnp.tile(sin_p, (bp, 1)), jnp.repeat(sin_s, bs, axis=0)], axis=0)

    h0 = _norm_call((xp, xs), norm_mix_g[0], mod, 0, 1, 0, lay, BF16)
    qkvg, (w_ret_out, w_ffn_in0) = _ret_proj_call(
        h0, ret_w_in, 0, lay, [(ret_w_out, 0, (256, d)), (ffn_w_in, 0, (d, 256))], (cos, sin))
    gated, s_ret_p = _ret_prompt_call(qkvg, ret_gn_g[0], lay)
    gated, s_ret_s = _ret_sample_call(qkvg, ret_gn_g[0], state_ret[0], gated, lay)
    x, w_ffn_out0, _ = _out_proj_call(gated, w_ret_out, (xp, xs), mod, 0, 2, lay, side=(ffn_w_out, 0, (256, d)))
    hmid, w_ffn_in1 = _ffn_in_call(x, norm_ffn_g[0], mod, 0, w_ffn_in0, lay, side=(ffn_w_in, 1, (d, 256)))
    x, w_ffn_out1, ssq = _out_proj_call(hmid, w_ffn_out0, (x,), mod, 0, 5, lay, side=(ffn_w_out, 1, (256, d)),
                                        emit_ssq=True)

    pw = pool_w[0].astype(BF16)
    hist_p = jnp.zeros((bp, 16, d), F32)
    hist_s = jnp.concatenate([jnp.zeros((1, bs, d), F32), state_pool[0].transpose(1, 0, 2)], axis=0)
    x1, nh_p = _pool_call(ssq, norm_mix_g[1], hist_p, x, mod, 1, pw, pool_scale[0], None, row0=0, n_rows=n_p,
                          n_seq=bp, stride=1, hist_len=0, mod_seq0=0, lay=lay)
    x, nh_s = _pool_call(ssq, norm_mix_g[1], hist_s.reshape(1, 16 * bs, d), x, mod, 1, pw, pool_scale[0], x1,
                         row0=n_p, n_rows=n_s, n_seq=1, stride=bs, hist_len=min(PAST_LEN, n_hist), mod_seq0=bp,
                         lay=lay)
    hmid, _ = _ffn_in_call(x, norm_ffn_g[1], mod, 1, w_ffn_in1, lay)
    x, _, _ = _out_proj_call(hmid, w_ffn_out1, (x,), mod, 1, 5, lay)

    y_p, y_s = _norm_call((x,), final_norm_g, None, 0, 0, 0, lay, F32, split_out=True)
    y_prompt = y_p.reshape(bp, tp, d)
    y_sample = y_s.reshape(ts, bs, d).transpose(1, 0, 2)
    state_pool_sample = nh_s.reshape(n_hist, bs, d).transpose(1, 0, 2)
    return (y_prompt, y_sample, s_ret_p[None], nh_p[None], s_ret_s[None], state_pool_sample[None])
```

```python
import functools

import jax
import jax.numpy as jnp
from jax import lax
from jax.experimental import pallas as pl
from jax.experimental.pallas import tpu as pltpu

F32 = jnp.float32
BF16 = jnp.bfloat16

RET_HEADS = 8
RET_CHUNK = 128
ROPE_BASE = 10000.0
POOL_WINDOWS = (2, 4, 8, 16)
N_ADA = 6
EPS = 1e-6
PAST_LEN = 16384

V7X_VMEM_BYTES = 64 * 1024 * 1024
V7X_LANES = 128
V7X_SUBLANES = 8
NORM_ROWS = 16
C_PAD = 8


def _vmem_limit(block_bytes, scratch_bytes=0):
    need = 2 * block_bytes + scratch_bytes + 12 * 1024 * 1024
    return int(min(need, V7X_VMEM_BYTES - 6 * 1024 * 1024))


def _nbytes(shape, dtype):
    n = 1
    for s in shape:
        n *= s
    return n * jnp.dtype(dtype).itemsize


def _silu(x):
    return x / (1.0 + jnp.exp(-x))


def _ada_kernel(c_ref, w_ref, b_ref, o_ref, *, n_prompt, sub):
    cs = _silu(c_ref[...])
    res = jnp.dot(cs.astype(BF16), w_ref[0].astype(BF16), preferred_element_type=F32) + b_ref[0]
    tn = res.shape[-1]
    for b in range(n_prompt):
        o_ref[0, b * sub:(b + 1) * sub, :] = jnp.broadcast_to(res[b:b + 1, :], (sub, tn))
    o_ref[0, n_prompt * sub:, :] = res[C_PAD:C_PAD + sub, :]


def _ada_call(c_all, ada_w, ada_b, n_prompt, sub, tn=1024):
    depth, d, n6 = ada_w.shape
    rows = c_all.shape[0]
    out_rows = (n_prompt + 1) * sub
    blocks = (_nbytes((rows, d), F32) + _nbytes((d, tn), F32) + _nbytes((1, tn), F32)
              + _nbytes((out_rows, tn), F32))
    return pl.pallas_call(
        functools.partial(_ada_kernel, n_prompt=n_prompt, sub=sub),
        grid=(depth, n6 // tn),
        in_specs=[
            pl.BlockSpec((rows, d), lambda l, j: (0, 0)),
            pl.BlockSpec((1, d, tn), lambda l, j: (l, 0, j)),
            pl.BlockSpec((1, 1, tn), lambda l, j: (l, 0, j)),
        ],
        out_specs=pl.BlockSpec((1, out_rows, tn), lambda l, j: (l, 0, j)),
        out_shape=jax.ShapeDtypeStruct((depth, out_rows, n6), F32),
        compiler_params=pltpu.CompilerParams(
            dimension_semantics=("arbitrary", "arbitrary"),
            vmem_limit_bytes=_vmem_limit(blocks, _nbytes((d, tn), BF16))),
        name="ada_mod",
    )(c_all, ada_w, ada_b.reshape(depth, 1, n6))


def _norm_mod_rows(x_ref, gam_ref, sc_ref, sh_ref, out_ref, *, tm, sub, row0=0, unroll=4):
    gam = gam_ref[...]
    per_sub = sub // NORM_ROWS

    def body(c, carry):
        r0 = pl.multiple_of(row0 + c * NORM_ROWS, NORM_ROWS)
        x = x_ref[pl.ds(r0, NORM_ROWS), :]
        hn = (x * lax.rsqrt(jnp.mean(x * x, axis=-1, keepdims=True) + EPS)) * gam
        if sc_ref is not None:
            m0 = pl.multiple_of((c % per_sub) * NORM_ROWS, NORM_ROWS)
            hn = hn * (1.0 + sc_ref[pl.ds(m0, NORM_ROWS), :]) + sh_ref[pl.ds(m0, NORM_ROWS), :]
        out_ref[pl.ds(r0, NORM_ROWS), :] = hn.astype(out_ref.dtype)
        return carry

    lax.fori_loop(0, tm // NORM_ROWS, body, 0, unroll=unroll)


def _mod_index(i, n_prompt_tiles, tiles_per_seq, n_prompt):
    return jnp.where(i < n_prompt_tiles, i // tiles_per_seq, n_prompt)


def _tiling(lay, tm):
    npt = lay["n_prompt_rows"] // tm
    midx = functools.partial(_mod_index, n_prompt_tiles=npt, tiles_per_seq=lay["t_prompt"] // tm,
                             n_prompt=lay["n_prompt"])
    return npt, midx


def _norm_kernel(*refs, tm, sub, n_in, n_out, has_mod, npt):
    x_refs = refs[:n_in]
    gam_ref = refs[n_in]
    sc_ref, sh_ref = (refs[n_in + 1], refs[n_in + 2]) if has_mod else (None, None)
    o_refs = refs[len(refs) - n_out:]
    if n_in == 1 and n_out == 1:
        _norm_mod_rows(x_refs[0], gam_ref, sc_ref, sh_ref, o_refs[0], tm=tm, sub=sub)
        return
    i = pl.program_id(0)

    @pl.when(i < npt)
    def _():
        _norm_mod_rows(x_refs[0], gam_ref, sc_ref, sh_ref, o_refs[0], tm=tm, sub=sub)

    @pl.when(i >= npt)
    def _():
        _norm_mod_rows(x_refs[-1], gam_ref, sc_ref, sh_ref, o_refs[-1], tm=tm, sub=sub)


def _norm_call(xs, gam, mod, layer, col_sc, col_sh, lay, out_dtype, split_out=False, tm=512):
    d = xs[0].shape[1]
    sub = lay["sub"]
    n_p, n_s = lay["n_prompt_rows"], lay["n_sample_rows"]
    npt, midx = _tiling(lay, tm)
    prompt_blk = lambda i: (jnp.minimum(i, npt - 1), 0)
    sample_blk = lambda i: (jnp.maximum(i - npt, 0), 0)
    whole_blk = lambda i: (i, 0)
    in_specs = ([pl.BlockSpec((tm, d), whole_blk)] if len(xs) == 1
                else [pl.BlockSpec((tm, d), prompt_blk), pl.BlockSpec((tm, d), sample_blk)])
    in_specs.append(pl.BlockSpec((1, d), lambda i: (0, 0)))
    args = list(xs) + [gam.reshape(1, d)]
    if mod is not None:
        in_specs += [pl.BlockSpec((None, sub, d), lambda i: (layer, midx(i), col_sc)),
                     pl.BlockSpec((None, sub, d), lambda i: (layer, midx(i), col_sh))]
        args += [mod, mod]
    if split_out:
        out_specs = [pl.BlockSpec((tm, d), prompt_blk), pl.BlockSpec((tm, d), sample_blk)]
        out_shape = [jax.ShapeDtypeStruct((n_p, d), out_dtype), jax.ShapeDtypeStruct((n_s, d), out_dtype)]
    else:
        out_specs = pl.BlockSpec((tm, d), whole_blk)
        out_shape = jax.ShapeDtypeStruct((n_p + n_s, d), out_dtype)
    n_out = 2 if split_out else 1
    blocks = (len(xs) + n_out) * _nbytes((tm, d), F32) + 2 * _nbytes((sub, d), F32)
    return pl.pallas_call(
        functools.partial(_norm_kernel, tm=tm, sub=sub, n_in=len(xs), n_out=n_out,
                          has_mod=mod is not None, npt=npt),
        grid=((n_p + n_s) // tm,),
        in_specs=in_specs,
        out_specs=out_specs,
        out_shape=out_shape,
        compiler_params=pltpu.CompilerParams(
            dimension_semantics=("arbitrary",), vmem_limit_bytes=_vmem_limit(blocks)),
        name="row_norm",
    )(*args)


def _load_weight(w_ref, copy_ref):
    w = w_ref[...]
    if copy_ref is not None:
        w = w.astype(BF16)
        copy_ref[...] = w
    return w


class _SideCast:
    def __init__(self, src3, layer, block, n_i, n_j):
        _, r, c = src3.shape
        br, tn = block
        ncb = c // tn
        n_blocks = (r // br) * ncb
        assert n_blocks <= n_i * n_j and r % br == 0 and c % tn == 0

        def blk(i, j):
            t = jnp.minimum(i * n_j + j, n_blocks - 1)
            return t // ncb, t % ncb

        self.arg = src3
        self.in_spec = pl.BlockSpec((None, br, tn), lambda i, j: (layer, *blk(i, j)))
        self.out_spec = pl.BlockSpec((None, br, tn), lambda i, j: (blk(i, j)[1], blk(i, j)[0], 0))
        self.out_shape = jax.ShapeDtypeStruct((ncb, r, tn), BF16)
        self.block_bytes = _nbytes(block, F32) + _nbytes(block, BF16)


def _ret_proj_kernel(h_ref, w_ref, cos_ref, sin_ref, *rest, tn, d_qk, head_dk, k_scale, cast_w, has_prev, n_side):
    n_in = (1 if has_prev else 0) + n_side
    o_ref = rest[n_in]
    for s in range(n_side):
        rest[len(rest) - n_side + s][...] = rest[n_in - n_side + s][...].astype(BF16)
    w = _load_weight(w_ref, rest[n_in + 1] if cast_w else None)
    acc = jnp.dot(h_ref[...], w, preferred_element_type=F32)
    col0 = pl.program_id(1) * tn
    is_qk = col0 < 2 * d_qk
    scale = jnp.where(jnp.logical_and(is_qk, col0 >= d_qk), k_scale, 1.0).astype(F32)
    cs = jnp.where(is_qk, cos_ref[...], 1.0) * scale
    sn = jnp.where(is_qk, sin_ref[...], 0.0) * scale
    half = head_dk // 2
    for hh in range(tn // head_dk):
        x1 = acc[:, hh * head_dk:hh * head_dk + half]
        x2 = acc[:, hh * head_dk + half:(hh + 1) * head_dk]
        o_ref[:, hh * head_dk:hh * head_dk + half] = (x1 * cs - x2 * sn).astype(o_ref.dtype)
        o_ref[:, hh * head_dk + half:(hh + 1) * head_dk] = (x1 * sn + x2 * cs).astype(o_ref.dtype)


def _ret_proj_call(h, w3, w_layer, lay, sides, rope_tables, tm=1024, tn=1024, tn_first=512):
    n, d = h.shape
    n_cols = w3.shape[2]
    d_qk = RET_HEADS * lay["dk"]
    half = lay["dk"] // 2

    per_tile = tn // tn_first

    def run(i0, n_i, tn_, weights, cast_w, prev, side_jobs):
        w_spec = (pl.BlockSpec((None, d, tn_), lambda i, j: (w_layer, 0, j)) if cast_w
                  else pl.BlockSpec((None, d, tn_), lambda i, j: (j, 0, 0)))
        in_specs = [pl.BlockSpec((tm, d), lambda i, j: (i + i0, 0)), w_spec,
                    pl.BlockSpec((tm, half), lambda i, j: (i + i0, 0)),
                    pl.BlockSpec((tm, half), lambda i, j: (i + i0, 0))]
        args = [h, weights, *rope_tables]
        aliases = {}
        if prev is not None:
            in_specs.append(pl.BlockSpec(memory_space=pl.ANY))
            args.append(prev)
            aliases = {len(args) - 1: 0}
        out_specs = [pl.BlockSpec((tm, tn_), lambda i, j: (i + i0, j))]
        out_shape = [jax.ShapeDtypeStruct((n, n_cols), BF16)]
        w_bytes = _nbytes((d, tn_), BF16)
        if cast_w:
            out_specs.append(pl.BlockSpec((None, d, tn_), lambda i, j: (j // per_tile, 0, j % per_tile)))
            out_shape.append(jax.ShapeDtypeStruct((n_cols // tn, d, tn), BF16))
            w_bytes = _nbytes((d, tn_), F32) + 2 * _nbytes((d, tn_), BF16)
        casts = [_SideCast(*job, n_i, n_cols // tn_) for job in side_jobs]
        in_specs += [sc.in_spec for sc in casts]
        args += [sc.arg for sc in casts]
        out_specs += [sc.out_spec for sc in casts]
        out_shape += [sc.out_shape for sc in casts]
        w_bytes += sum(sc.block_bytes for sc in casts)
        blocks = _nbytes((tm, d), BF16) + w_bytes + 2 * _nbytes((tm, half), F32) + _nbytes((tm, tn_), BF16)
        return pl.pallas_call(
            functools.partial(_ret_proj_kernel, tn=tn_, d_qk=d_qk, head_dk=lay["dk"],
                              k_scale=float(lay["dk"]) ** -0.5, cast_w=cast_w,
                              has_prev=prev is not None, n_side=len(casts)),
            grid=(n_i, n_cols // tn_),
            in_specs=in_specs,
            out_specs=out_specs,
            out_shape=out_shape,
            input_output_aliases=aliases,
            compiler_params=pltpu.CompilerParams(
                dimension_semantics=("arbitrary", "arbitrary"),
                vmem_limit_bytes=_vmem_limit(blocks, _nbytes((tm, tn_), F32))),
            name="ret_in_proj" + ("_first" if cast_w else ""),
        )(*args)

    out0, w_bf = run(0, 1, tn_first, w3, True, None, [])
    out, *side_w = run(1, n // tm - 1, tn, w_bf, False, out0, sides)
    return out, side_w


def _ffn_in_kernel(x_ref, gam_ref, sc_ref, sh_ref, wg_ref, wu_ref, *rest, tm, sub, has_side):
    if has_side:
        side_in, o_ref, side_out, h_ref = rest
        side_out[...] = side_in[...].astype(BF16)
    else:
        o_ref, h_ref = rest

    @pl.when(pl.program_id(1) == 0)
    def _():
        _norm_mod_rows(x_ref, gam_ref, sc_ref, sh_ref, h_ref, tm=tm, sub=sub)

    h = h_ref[...]
    gate = jnp.dot(h, wg_ref[...], preferred_element_type=F32)
    up = jnp.dot(h, wu_ref[...], preferred_element_type=F32)
    o_ref[...] = (_silu(gate) * up).astype(o_ref.dtype)


def _ffn_in_call(x, gam, mod, layer, w, lay, side=None, tm=1024):
    n, d = x.shape
    sub = lay["sub"]
    tn = w.shape[2]
    n_j = w.shape[0] // 2
    d_ff = n_j * tn
    _, midx = _tiling(lay, tm)
    in_specs = [
        pl.BlockSpec((tm, d), lambda i, j: (i, 0)),
        pl.BlockSpec((1, d), lambda i, j: (0, 0)),
        pl.BlockSpec((None, sub, d), lambda i, j: (layer, midx(i), 4)),
        pl.BlockSpec((None, sub, d), lambda i, j: (layer, midx(i), 3)),
        pl.BlockSpec((None, d, tn), lambda i, j: (j, 0, 0)),
        pl.BlockSpec((None, d, tn), lambda i, j: (j + n_j, 0, 0)),
    ]
    args = [x, gam.reshape(1, d), mod, mod, w, w]
    out_specs = [pl.BlockSpec((tm, tn), lambda i, j: (i, j))]
    out_shape = [jax.ShapeDtypeStruct((n, d_ff), BF16)]
    blocks = (_nbytes((tm, d), F32) + 2 * _nbytes((sub, d), F32) + 2 * _nbytes((d, tn), BF16)
              + _nbytes((tm, tn), BF16))
    if side is not None:
        sc = _SideCast(*side, n // tm, n_j)
        in_specs.append(sc.in_spec)
        args.append(sc.arg)
        out_specs.append(sc.out_spec)
        out_shape.append(sc.out_shape)
        blocks += sc.block_bytes
    res = pl.pallas_call(
        functools.partial(_ffn_in_kernel, tm=tm, sub=sub, has_side=side is not None),
        grid=(n // tm, n_j),
        in_specs=in_specs,
        out_specs=out_specs,
        out_shape=out_shape,
        scratch_shapes=[pltpu.VMEM((tm, d), BF16)],
        compiler_params=pltpu.CompilerParams(
            dimension_semantics=("arbitrary", "arbitrary"),
            vmem_limit_bytes=_vmem_limit(blocks, _nbytes((tm, d), BF16) + 2 * _nbytes((tm, tn), F32))),
        name="ffn_in_swiglu",
    )(*args)
    return res[0], (res[1] if side is not None else None)


def _out_proj_kernel(a_ref, w_ref, *refs, tm, sub, npt, n_x, has_side, emit_ssq):
    x_refs, gate_ref = refs[:n_x], refs[n_x]
    rest = list(refs[n_x + 1:])
    ssq_ref = rest.pop() if emit_ssq else None
    if has_side:
        side_in, o_ref, side_out = rest
        side_out[...] = side_in[...].astype(BF16)
    else:
        (o_ref,) = rest
    acc = jnp.dot(a_ref[...], w_ref[...], preferred_element_type=F32)
    gate = gate_ref[...]
    is_prompt = pl.program_id(0) < npt
    for s in range(tm // sub):
        rows = slice(s * sub, (s + 1) * sub)
        x = x_refs[0][rows, :]
        if n_x == 2:
            x = jnp.where(is_prompt, x, x_refs[1][rows, :])
        x_new = x + gate * acc[rows, :]
        o_ref[rows, :] = x_new
        if emit_ssq:
            ssq_ref[rows, :] = jnp.sum(x_new * x_new, axis=-1, keepdims=True)


def _out_proj_call(a, w, xs, mod, layer, gate_col, lay, side=None, emit_ssq=False, tm=1024):
    n, k = a.shape
    n_j, _, tn = w.shape
    d = n_j * tn
    sub = lay["sub"]
    npt, midx = _tiling(lay, tm)
    x_specs = ([pl.BlockSpec((tm, tn), lambda i, j: (i, j))] if len(xs) == 1
               else [pl.BlockSpec((tm, tn), lambda i, j: (jnp.minimum(i, npt - 1), j)),
                     pl.BlockSpec((tm, tn), lambda i, j: (jnp.maximum(i - npt, 0), j))])
    in_specs = ([pl.BlockSpec((tm, k), lambda i, j: (i, 0)), pl.BlockSpec((None, k, tn), lambda i, j: (j, 0, 0))]
                + x_specs
                + [pl.BlockSpec((None, sub, tn), lambda i, j: (layer, midx(i), gate_col * n_j + j))])
    args = [a, w, *xs, mod]
    out_specs = [pl.BlockSpec((tm, tn), lambda i, j: (i, j))]
    out_shape = [jax.ShapeDtypeStruct((n, d), F32)]
    blocks = (_nbytes((tm, k), BF16) + _nbytes((k, tn), BF16) + (1 + len(xs)) * _nbytes((tm, tn), F32)
              + _nbytes((sub, tn), F32))
    if side is not None:
        sc = _SideCast(*side, n // tm, n_j)
        in_specs.append(sc.in_spec)
        args.append(sc.arg)
        out_specs.append(sc.out_spec)
        out_shape.append(sc.out_shape)
        blocks += sc.block_bytes
    if emit_ssq:
        out_specs.append(pl.BlockSpec((None, tm, 1), lambda i, j: (j, i, 0)))
        out_shape.append(jax.ShapeDtypeStruct((n_j, n, 1), F32))
        blocks += _nbytes((tm, V7X_LANES), F32)
    res = pl.pallas_call(
        functools.partial(_out_proj_kernel, tm=tm, sub=sub, npt=npt, n_x=len(xs), has_side=side is not None,
                          emit_ssq=emit_ssq),
        grid=(n // tm, n_j),
        in_specs=in_specs,
        out_specs=out_specs,
        out_shape=out_shape,
        compiler_params=pltpu.CompilerParams(
            dimension_semantics=("arbitrary", "arbitrary"),
            vmem_limit_bytes=_vmem_limit(blocks, _nbytes((tm, tn), F32))),
        name="out_proj_residual",
    )(*args)
    return res[0], (res[1] if side is not None else None), (res[-1] if emit_ssq else None)


def _retention_decays(c):
    lg = jnp.log1p(-jnp.exp2(-5.0 - jnp.arange(RET_HEADS, dtype=F32)))
    idx = jnp.arange(c, dtype=F32)
    diff = idx[:, None] - idx[None, :]
    causal = diff >= 0
    dmask = jnp.where(causal[None], jnp.exp(jnp.where(causal, diff, 0.0)[None] * lg[:, None, None]), 0.0)
    qdec = jnp.exp((idx + 1.0)[None, :] * lg[:, None])
    kdec = jnp.exp((c - 1.0 - idx)[None, :] * lg[:, None])
    sdec = jnp.exp(c * lg)
    return dmask, qdec[:, :, None], kdec[:, :, None], sdec[:, None, None]


def _group_norm_gate(o, g, gng):
    mu = jnp.mean(o, axis=-1, keepdims=True)
    dlt = o - mu
    var = jnp.mean(dlt * dlt, axis=-1, keepdims=True)
    return _silu(g) * (dlt * lax.rsqrt(var + EPS) * gng)


_NT = (((1,), (1,)), ((), ()))
_TN = (((0,), (0,)), ((), ()))


def _retention_chunk(q, k, v, s_prev, dmask, qdec, kdec, sdec):
    scores = lax.dot_general(q, k, _NT, preferred_element_type=F32) * dmask
    o = (jnp.dot(scores.astype(BF16), v, preferred_element_type=F32)
         + qdec * jnp.dot(q, s_prev.astype(BF16), preferred_element_type=F32))
    kd = (k.astype(F32) * kdec).astype(BF16)
    s_new = sdec * s_prev + lax.dot_general(kd, v, _TN, preferred_element_type=F32)
    return o, s_new


def _ret_prompt_kernel(q_ref, k_ref, v_ref, g_ref, gng_ref, dmask_ref, qdec_ref, kdec_ref, sdec_ref,
                       o_ref, s_out_ref, s_ref, *, chunk, n_chunks):
    s_ref[...] = jnp.zeros_like(s_ref)
    dmask = dmask_ref[0]
    qdec = qdec_ref[0]
    kdec = kdec_ref[0]
    sdec = sdec_ref[0]
    gng = gng_ref[...]

    def body(c, carry):
        rows = pl.ds(pl.multiple_of(c * chunk, chunk), chunk)
        o, s_new = _retention_chunk(q_ref[rows, :], k_ref[rows, :], v_ref[rows, :], s_ref[...],
                                    dmask, qdec, kdec, sdec)
        s_ref[...] = s_new
        o_ref[rows, :] = _group_norm_gate(o, g_ref[rows, :].astype(F32), gng).astype(o_ref.dtype)
        return carry

    lax.fori_loop(0, n_chunks, body, 0, unroll=2)
    s_out_ref[0, 0] = s_ref[...]


def _ret_prompt_call(qkvg, gng, lay, chunk=2 * RET_CHUNK):
    n = qkvg.shape[0]
    bp, tp, dk, dv = lay["n_prompt"], lay["t_prompt"], lay["dk"], lay["dv"]
    h = RET_HEADS
    dmask, qdec, kdec, sdec = _retention_decays(chunk)
    v_blk0 = 2 * h * dk // dv
    g_blk0 = v_blk0 + h
    blocks = (2 * _nbytes((tp, dk), BF16) + 3 * _nbytes((tp, dv), BF16)
              + 2 * _nbytes((dk, dv), F32) + _nbytes((chunk, chunk), F32))
    return pl.pallas_call(
        functools.partial(_ret_prompt_kernel, chunk=chunk, n_chunks=tp // chunk),
        grid=(bp, h),
        in_specs=[
            pl.BlockSpec((tp, dk), lambda b, hh: (b, hh)),
            pl.BlockSpec((tp, dk), lambda b, hh: (b, h + hh)),
            pl.BlockSpec((tp, dv), lambda b, hh: (b, v_blk0 + hh)),
            pl.BlockSpec((tp, dv), lambda b, hh: (b, g_blk0 + hh)),
            pl.BlockSpec((1, dv), lambda b, hh: (0, hh)),
            pl.BlockSpec((1, chunk, chunk), lambda b, hh: (hh, 0, 0)),
            pl.BlockSpec((1, chunk, 1), lambda b, hh: (hh, 0, 0)),
            pl.BlockSpec((1, chunk, 1), lambda b, hh: (hh, 0, 0)),
            pl.BlockSpec((1, 1, 1), lambda b, hh: (hh, 0, 0)),
        ],
        out_specs=[
            pl.BlockSpec((tp, dv), lambda b, hh: (b, hh)),
            pl.BlockSpec((1, 1, dk, dv), lambda b, hh: (b, hh, 0, 0)),
        ],
        out_shape=[jax.ShapeDtypeStruct((n, h * dv), BF16),
                   jax.ShapeDtypeStruct((bp, h, dk, dv), F32)],
        scratch_shapes=[pltpu.VMEM((dk, dv), F32)],
        compiler_params=pltpu.CompilerParams(
            dimension_semantics=("arbitrary", "arbitrary"),
            vmem_limit_bytes=_vmem_limit(blocks, _nbytes((dk, dv), F32))),
        name="retention_prompt",
    )(qkvg, qkvg, qkvg, qkvg, gng.reshape(1, h * dv), dmask, qdec, kdec, sdec)


def _ret_sample_kernel(q_ref, k_ref, v_ref, g_ref, gng_ref, s0_ref, dmask_ref, qdec_ref, kdec_ref, sdec_ref,
                       prev_ref, o_ref, s_out_ref, qf_ref, kf_ref, vf_ref, gf_ref, of_ref, *, bb):
    del prev_ref
    dmask = dmask_ref[0]
    qdec = qdec_ref[0]
    kdec = kdec_ref[0]
    sdec = sdec_ref[0]
    gng = gng_ref[...]
    qf_ref[...] = q_ref[...].astype(F32)
    kf_ref[...] = k_ref[...].astype(F32)
    vf_ref[...] = v_ref[...].astype(F32)
    gf_ref[...] = g_ref[...].astype(F32)

    def body(b, carry):
        o, s_new = _retention_chunk(qf_ref[:, b, :].astype(BF16), kf_ref[:, b, :].astype(BF16),
                                    vf_ref[:, b, :].astype(BF16), s0_ref[b, 0], dmask, qdec, kdec, sdec)
        s_out_ref[b, 0] = s_new
        of_ref[:, b, :] = _group_norm_gate(o, gf_ref[:, b, :], gng)
        return carry

    lax.fori_loop(0, bb, body, 0, unroll=4)
    o_ref[...] = of_ref[...].astype(o_ref.dtype)


def _ret_sample_call(qkvg, gng, state, gated, lay, bb=16):
    bs, ts, dk, dv = lay["sub"], lay["t_sample"], lay["dk"], lay["dv"]
    h = RET_HEADS
    n = qkvg.shape[0]
    t_blk = lay["n_prompt_rows"] // (bs * ts)
    assert t_blk * bs * ts == lay["n_prompt_rows"]
    dmask, qdec, kdec, sdec = _retention_decays(ts)
    v_blk0 = 2 * h * dk // dv
    g_blk0 = v_blk0 + h
    blocks = (2 * _nbytes((ts, bb, dk), BF16) + 3 * _nbytes((ts, bb, dv), BF16) + 2 * _nbytes((bb, dk, dv), F32))
    scratch = 2 * _nbytes((ts, bb, dk), F32) + 3 * _nbytes((ts, bb, dv), F32)
    qkv3 = qkvg.reshape(n // bs, bs, qkvg.shape[1])
    gated3, s_new = pl.pallas_call(
        functools.partial(_ret_sample_kernel, bb=bb),
        grid=(bs // bb, h),
        in_specs=[
            pl.BlockSpec((ts, bb, dk), lambda b, hh: (t_blk, b, hh)),
            pl.BlockSpec((ts, bb, dk), lambda b, hh: (t_blk, b, h + hh)),
            pl.BlockSpec((ts, bb, dv), lambda b, hh: (t_blk, b, v_blk0 + hh)),
            pl.BlockSpec((ts, bb, dv), lambda b, hh: (t_blk, b, g_blk0 + hh)),
            pl.BlockSpec((1, dv), lambda b, hh: (0, hh)),
            pl.BlockSpec((bb, 1, dk, dv), lambda b, hh: (b, hh, 0, 0)),
            pl.BlockSpec((1, ts, ts), lambda b, hh: (hh, 0, 0)),
            pl.BlockSpec((1, ts, 1), lambda b, hh: (hh, 0, 0)),
            pl.BlockSpec((1, ts, 1), lambda b, hh: (hh, 0, 0)),
            pl.BlockSpec((1, 1, 1), lambda b, hh: (hh, 0, 0)),
            pl.BlockSpec(memory_space=pl.ANY),
        ],
        out_specs=[
            pl.BlockSpec((ts, bb, dv), lambda b, hh: (t_blk, b, hh)),
            pl.BlockSpec((bb, 1, dk, dv), lambda b, hh: (b, hh, 0, 0)),
        ],
        out_shape=[jax.ShapeDtypeStruct((n // bs, bs, h * dv), gated.dtype),
                   jax.ShapeDtypeStruct((bs, h, dk, dv), F32)],
        scratch_shapes=[pltpu.VMEM((ts, bb, dk), F32), pltpu.VMEM((ts, bb, dk), F32),
                        pltpu.VMEM((ts, bb, dv), F32), pltpu.VMEM((ts, bb, dv), F32),
                        pltpu.VMEM((ts, bb, dv), F32)],
        input_output_aliases={10: 0},
        compiler_params=pltpu.CompilerParams(
            dimension_semantics=("arbitrary", "arbitrary"),
            vmem_limit_bytes=_vmem_limit(blocks, scratch)),
        name="retention_sample",
    )(qkv3, qkv3, qkv3, qkv3, gng.reshape(1, h * dv), state, dmask, qdec, kdec, sdec,
      gated.reshape(n // bs, bs, h * dv))
    return gated3.reshape(n, h * dv), s_new


def _window_sum_rows(win_ref, halo, r0, rows, win, stride):
    if stride % V7X_SUBLANES == 0:
        acc = win_ref[pl.ds(halo + r0, rows), :]
        for k in range(1, win):
            acc = acc + win_ref[pl.ds(halo - k * stride + r0, rows), :]
        return acc
    assert stride == 1 and halo == 16 and win <= 16
    blk = win_ref[pl.ds(r0, rows + 16), :]
    shift = 1
    while shift < win:
        blk = blk + pltpu.roll(blk, shift, 0)
        shift *= 2
    return blk[16:, :]


def _pool_kernel(ssq_ref, gam_ref, sc_ref, sh_ref, hist_ref, x_ref, gate_ref, w_ref, scale_ref, *rest,
                 tm, sub, stride, tiles_per_seq, hist_len, chunk_rows, d_model):
    o_ref, hist_out_ref, win_ref, m_ref = rest[-4:]
    g = pl.program_id(0)
    i = pl.program_id(1)
    halo = 16 * stride
    tile_in_seq = i % tiles_per_seq

    if tiles_per_seq > 1:
        @pl.when(tile_in_seq != 0)
        def _():
            win_ref[0:halo, :] = win_ref[tm:tm + halo, :]

    @pl.when(tile_in_seq == 0)
    def _():
        win_ref[0:halo, :] = hist_ref[0]

    gam = gam_ref[...]
    for s in range(tm // sub):
        rows = slice(s * sub, (s + 1) * sub)
        ssq = ssq_ref[0, rows, :]
        for c in range(1, ssq_ref.shape[0]):
            ssq = ssq + ssq_ref[c, rows, :]
        rs = lax.rsqrt(ssq / d_model + EPS)
        win_ref[halo + s * sub:halo + (s + 1) * sub, :] = (
            ((x_ref[rows, :] * rs) * gam) * (1.0 + sc_ref[...]) + sh_ref[...])

    for gi, win in enumerate(POOL_WINDOWS):
        @pl.when(g == gi)
        def _(win=win):
            def body(c, carry):
                r0 = pl.multiple_of(c * chunk_rows, chunk_rows)
                cur = win_ref[pl.ds(halo + r0, chunk_rows), :]
                acc = _window_sum_rows(win_ref, halo, r0, chunk_rows, win, stride)
                t = (tile_in_seq * tm + r0 + lax.broadcasted_iota(jnp.int32, (chunk_rows, 1), 0)) // stride
                cnt = jnp.minimum(win, t + 1 + hist_len).astype(F32)
                m_ref[pl.ds(r0, chunk_rows), :] = (acc / cnt - cur).astype(m_ref.dtype)
                return carry

            lax.fori_loop(0, tm // chunk_rows, body, 0)

    y = jnp.dot(m_ref[...], w_ref[0], preferred_element_type=F32) * scale_ref[...]
    gate = gate_ref[...]
    for s in range(tm // sub):
        rows = slice(s * sub, (s + 1) * sub)
        o_ref[rows, :] = x_ref[rows, :] + gate * y[rows, :]
    hist_out_ref[0] = win_ref[tm + stride:tm + halo, :]


def _pool_call(ssq, gam, hist, x, mod, layer, w, scale, x_prev, *, row0, n_rows, n_seq, stride, hist_len, mod_seq0,
               lay, tm=1024):
    n, d = x.shape
    sub = lay["sub"]
    n_g = len(POOL_WINDOWS)
    gc = d // n_g
    n_tiles = n_rows // tm
    tps = n_tiles // n_seq
    blk0 = row0 // tm
    halo = 16 * stride
    n_parts = ssq.shape[0]
    blocks = (2 * _nbytes((tm, gc), F32) + _nbytes((halo, gc), F32) + 3 * _nbytes((sub, gc), F32)
              + _nbytes((gc, gc), BF16) + _nbytes((15 * stride, gc), F32) + n_parts * _nbytes((tm, V7X_LANES), F32))
    scratch = _nbytes((halo + tm, gc), F32) + _nbytes((tm, gc), BF16)
    sh_col0, sc_col0, gate_col0 = 0, n_g, 2 * n_g
    mod_blk = lambda col0: pl.BlockSpec((None, sub, gc), lambda g, i: (layer, mod_seq0 + i // tps, col0 + g))
    in_specs = [
        pl.BlockSpec((n_parts, tm, 1), lambda g, i: (0, blk0 + i, 0)),
        pl.BlockSpec((1, gc), lambda g, i: (0, g)),
        mod_blk(sc_col0),
        mod_blk(sh_col0),
        pl.BlockSpec((1, halo, gc), lambda g, i: (i // tps, 0, g)),
        pl.BlockSpec((tm, gc), lambda g, i: (blk0 + i, g)),
        mod_blk(gate_col0),
        pl.BlockSpec((1, gc, gc), lambda g, i: (g, 0, 0)),
        pl.BlockSpec((1, gc), lambda g, i: (0, g)),
    ]
    args = [ssq, gam.reshape(1, d), mod, mod, hist, x, mod, w, scale.reshape(1, d)]
    aliases = {}
    if x_prev is not None:
        in_specs.append(pl.BlockSpec(memory_space=pl.ANY))
        args.append(x_prev)
        aliases = {len(args) - 1: 0}
    return pl.pallas_call(
        functools.partial(_pool_kernel, tm=tm, sub=sub, stride=stride, tiles_per_seq=tps,
                          hist_len=hist_len, chunk_rows=64, d_model=d),
        grid=(n_g, n_tiles),
        in_specs=in_specs,
        out_specs=[
            pl.BlockSpec((tm, gc), lambda g, i: (blk0 + i, g)),
            pl.BlockSpec((1, 15 * stride, gc), lambda g, i: (i // tps, 0, g)),
        ],
        out_shape=[jax.ShapeDtypeStruct((n, d), F32),
                   jax.ShapeDtypeStruct((n_seq, 15 * stride, d), F32)],
        scratch_shapes=[pltpu.VMEM((halo + tm, gc), F32), pltpu.VMEM((tm, gc), BF16)],
        input_output_aliases=aliases,
        compiler_params=pltpu.CompilerParams(
            dimension_semantics=("arbitrary", "arbitrary"),
            vmem_limit_bytes=_vmem_limit(blocks, scratch)),
        name="pool_mixer_s%d" % stride,
    )(*args)


def _rope_tables(pos, dk):
    half = dk // 2
    inv = 1.0 / (ROPE_BASE ** (jnp.arange(half, dtype=F32) / half))
    ang = pos[:, None] * inv[None, :]
    return jnp.cos(ang), jnp.sin(ang)


def kernel(x_prompt, x_sample, c_prompt, c_sample, state_ret, state_pool, norm_mix_g, norm_ffn_g, ada_w, ada_b,
           ret_w_in, ret_gn_g, ret_w_out, pool_w, pool_scale, ffn_w_in, ffn_w_out, final_norm_g):
    bp, tp, d = x_prompt.shape
    bs, ts, _ = x_sample.shape
    depth = ada_w.shape[0]
    dk = state_ret.shape[3]
    dv = state_ret.shape[4]
    n_hist = state_pool.shape[2]
    assert bp <= C_PAD and n_hist == max(POOL_WINDOWS) - 1 and depth == 2
    n_p = bp * tp
    n_s = bs * ts
    lay = dict(n_prompt=bp, t_prompt=tp, n_prompt_rows=n_p, n_sample_rows=n_s, sub=bs, t_sample=ts, dk=dk, dv=dv)

    xp = x_prompt.reshape(n_p, d)
    xs = x_sample.transpose(1, 0, 2).reshape(n_s, d)
    c_all = jnp.concatenate([c_prompt, jnp.zeros((C_PAD - bp, d), F32), c_sample], axis=0)
    mod = _ada_call(c_all, ada_w, ada_b, bp, bs)

    cos_p, sin_p = _rope_tables(jnp.arange(tp, dtype=F32), dk)
    cos_s, sin_s = _rope_tables(float(PAST_LEN) + jnp.arange(ts, dtype=F32), dk)
    cos = jnp.concatenate([jnp.tile(cos_p, (bp, 1)), jnp.repeat(cos_s, bs, axis=0)], axis=0)
    sin = jnp.concatenate([jnp.tile(sin_p, (bp, 1)), jnp.repeat(sin_s, bs, axis=0)], axis=0)

    h0 = _norm_call((xp, xs), norm_mix_g[0], mod, 0, 1, 0, lay, BF16)
    tn_ffn = 512
    d_ff = ffn_w_out.shape[1]
    cast_in = (d // 2, tn_ffn)
    cast_out = (d_ff // 8, tn_ffn)
    qkvg, (w_ret_out, w_ffn_in0) = _ret_proj_call(
        h0, ret_w_in, 0, lay, [(ret_w_out, 0, (ret_w_out.shape[1] // 8, tn_ffn)), (ffn_w_in, 0, cast_in)], (cos, sin))
    gated, s_ret_p = _ret_prompt_call(qkvg, ret_gn_g[0], lay)
    gated, s_ret_s = _ret_sample_call(qkvg, ret_gn_g[0], state_ret[0], gated, lay)
    x, w_ffn_out0, _ = _out_proj_call(gated, w_ret_out, (xp, xs), mod, 0, 2, lay, side=(ffn_w_out, 0, cast_out))
    hmid, w_ffn_in1 = _ffn_in_call(x, norm_ffn_g[0], mod, 0, w_ffn_in0, lay, side=(ffn_w_in, 1, cast_in))
    x, w_ffn_out1, ssq = _out_proj_call(hmid, w_ffn_out0, (x,), mod, 0, 5, lay, side=(ffn_w_out, 1, cast_out),
                                        emit_ssq=True)

    pw = pool_w[0].astype(BF16)
    hist_p = jnp.zeros((bp, 16, d), F32)
    hist_s = jnp.concatenate([jnp.zeros((1, bs, d), F32), state_pool[0].transpose(1, 0, 2)], axis=0)
    x1, nh_p = _pool_call(ssq, norm_mix_g[1], hist_p, x, mod, 1, pw, pool_scale[0], None, row0=0, n_rows=n_p,
                          n_seq=bp, stride=1, hist_len=0, mod_seq0=0, lay=lay)
    x, nh_s = _pool_call(ssq, norm_mix_g[1], hist_s.reshape(1, 16 * bs, d), x, mod, 1, pw, pool_scale[0], x1,
                         row0=n_p, n_rows=n_s, n_seq=1, stride=bs, hist_len=min(PAST_LEN, n_hist), mod_seq0=bp,
                         lay=lay)
    hmid, _ = _ffn_in_call(x, norm_ffn_g[1], mod, 1, w_ffn_in1, lay)
    x, _, _ = _out_proj_call(hmid, w_ffn_out1, (x,), mod, 1, 5, lay)

    y_p, y_s = _norm_call((x,), final_norm_g, None, 0, 0, 0, lay, F32, split_out=True)
    y_prompt = y_p.reshape(bp, tp, d)
    y_sample = y_s.reshape(ts, bs, d).transpose(1, 0, 2)
    state_pool_sample = nh_s.reshape(n_hist, bs, d).transpose(1, 0, 2)
    return (y_prompt, y_sample, s_ret_p[None], nh_p[None], s_ret_s[None], state_pool_sample[None])
```

```python
import functools

import jax
import jax.numpy as jnp
from jax import lax
from jax.experimental import pallas as pl
from jax.experimental.pallas import tpu as pltpu

F32 = jnp.float32
BF16 = jnp.bfloat16

RET_HEADS = 8
RET_CHUNK = 128
ROPE_BASE = 10000.0
POOL_WINDOWS = (2, 4, 8, 16)
N_ADA = 6
EPS = 1e-6
PAST_LEN = 16384

V7X_VMEM_BYTES = 64 * 1024 * 1024
V7X_LANES = 128
V7X_SUBLANES = 8
NORM_ROWS = 16
C_PAD = 8


def _vmem_limit(block_bytes, scratch_bytes=0):
    need = 2 * block_bytes + scratch_bytes + 12 * 1024 * 1024
    return int(min(need, V7X_VMEM_BYTES - 6 * 1024 * 1024))


def _nbytes(shape, dtype):
    n = 1
    for s in shape:
        n *= s
    return n * jnp.dtype(dtype).itemsize


def _silu(x):
    return x / (1.0 + jnp.exp(-x))


def _ada_kernel(c_ref, w_ref, b_ref, o_ref, *, n_prompt, sub):
    cs = _silu(c_ref[...])
    res = jnp.dot(cs.astype(BF16), w_ref[0].astype(BF16), preferred_element_type=F32) + b_ref[0]
    tn = res.shape[-1]
    for b in range(n_prompt):
        o_ref[0, b * sub:(b + 1) * sub, :] = jnp.broadcast_to(res[b:b + 1, :], (sub, tn))
    o_ref[0, n_prompt * sub:, :] = res[C_PAD:C_PAD + sub, :]


def _ada_call(c_all, ada_w, ada_b, n_prompt, sub, tn=1024):
    depth, d, n6 = ada_w.shape
    rows = c_all.shape[0]
    out_rows = (n_prompt + 1) * sub
    blocks = (_nbytes((rows, d), F32) + _nbytes((d, tn), F32) + _nbytes((1, tn), F32)
              + _nbytes((out_rows, tn), F32))
    return pl.pallas_call(
        functools.partial(_ada_kernel, n_prompt=n_prompt, sub=sub),
        grid=(depth, n6 // tn),
        in_specs=[
            pl.BlockSpec((rows, d), lambda l, j: (0, 0)),
            pl.BlockSpec((1, d, tn), lambda l, j: (l, 0, j)),
            pl.BlockSpec((1, 1, tn), lambda l, j: (l, 0, j)),
        ],
        out_specs=pl.BlockSpec((1, out_rows, tn), lambda l, j: (l, 0, j)),
        out_shape=jax.ShapeDtypeStruct((depth, out_rows, n6), F32),
        compiler_params=pltpu.CompilerParams(
            dimension_semantics=("arbitrary", "arbitrary"),
            vmem_limit_bytes=_vmem_limit(blocks, _nbytes((d, tn), BF16))),
        name="ada_mod",
    )(c_all, ada_w, ada_b.reshape(depth, 1, n6))


def _norm_mod_rows(x_ref, gam_ref, sc_ref, sh_ref, out_ref, *, tm, sub, row0=0, unroll=4):
    gam = gam_ref[...]
    per_sub = sub // NORM_ROWS

    def body(c, carry):
        r0 = pl.multiple_of(row0 + c * NORM_ROWS, NORM_ROWS)
        x = x_ref[pl.ds(r0, NORM_ROWS), :]
        hn = (x * lax.rsqrt(jnp.mean(x * x, axis=-1, keepdims=True) + EPS)) * gam
        if sc_ref is not None:
            m0 = pl.multiple_of((c % per_sub) * NORM_ROWS, NORM_ROWS)
            hn = hn * (1.0 + sc_ref[pl.ds(m0, NORM_ROWS), :]) + sh_ref[pl.ds(m0, NORM_ROWS), :]
        out_ref[pl.ds(r0, NORM_ROWS), :] = hn.astype(out_ref.dtype)
        return carry

    lax.fori_loop(0, tm // NORM_ROWS, body, 0, unroll=unroll)


def _mod_index(i, n_prompt_tiles, tiles_per_seq, n_prompt):
    return jnp.where(i < n_prompt_tiles, i // tiles_per_seq, n_prompt)


def _tiling(lay, tm):
    npt = lay["n_prompt_rows"] // tm
    midx = functools.partial(_mod_index, n_prompt_tiles=npt, tiles_per_seq=lay["t_prompt"] // tm,
                             n_prompt=lay["n_prompt"])
    return npt, midx


def _norm_kernel(*refs, tm, sub, n_in, n_out, has_mod, npt):
    x_refs = refs[:n_in]
    gam_ref = refs[n_in]
    sc_ref, sh_ref = (refs[n_in + 1], refs[n_in + 2]) if has_mod else (None, None)
    o_refs = refs[len(refs) - n_out:]
    if n_in == 1 and n_out == 1:
        _norm_mod_rows(x_refs[0], gam_ref, sc_ref, sh_ref, o_refs[0], tm=tm, sub=sub)
        return
    i = pl.program_id(0)

    @pl.when(i < npt)
    def _():
        _norm_mod_rows(x_refs[0], gam_ref, sc_ref, sh_ref, o_refs[0], tm=tm, sub=sub)

    @pl.when(i >= npt)
    def _():
        _norm_mod_rows(x_refs[-1], gam_ref, sc_ref, sh_ref, o_refs[-1], tm=tm, sub=sub)


def _norm_call(xs, gam, mod, layer, col_sc, col_sh, lay, out_dtype, split_out=False, tm=512):
    d = xs[0].shape[1]
    sub = lay["sub"]
    n_p, n_s = lay["n_prompt_rows"], lay["n_sample_rows"]
    npt, midx = _tiling(lay, tm)
    prompt_blk = lambda i: (jnp.minimum(i, npt - 1), 0)
    sample_blk = lambda i: (jnp.maximum(i - npt, 0), 0)
    whole_blk = lambda i: (i, 0)
    in_specs = ([pl.BlockSpec((tm, d), whole_blk)] if len(xs) == 1
                else [pl.BlockSpec((tm, d), prompt_blk), pl.BlockSpec((tm, d), sample_blk)])
    in_specs.append(pl.BlockSpec((1, d), lambda i: (0, 0)))
    args = list(xs) + [gam.reshape(1, d)]
    if mod is not None:
        in_specs += [pl.BlockSpec((None, sub, d), lambda i: (layer, midx(i), col_sc)),
                     pl.BlockSpec((None, sub, d), lambda i: (layer, midx(i), col_sh))]
        args += [mod, mod]
    if split_out:
        out_specs = [pl.BlockSpec((tm, d), prompt_blk), pl.BlockSpec((tm, d), sample_blk)]
        out_shape = [jax.ShapeDtypeStruct((n_p, d), out_dtype), jax.ShapeDtypeStruct((n_s, d), out_dtype)]
    else:
        out_specs = pl.BlockSpec((tm, d), whole_blk)
        out_shape = jax.ShapeDtypeStruct((n_p + n_s, d), out_dtype)
    n_out = 2 if split_out else 1
    blocks = (len(xs) + n_out) * _nbytes((tm, d), F32) + 2 * _nbytes((sub, d), F32)
    return pl.pallas_call(
        functools.partial(_norm_kernel, tm=tm, sub=sub, n_in=len(xs), n_out=n_out,
                          has_mod=mod is not None, npt=npt),
        grid=((n_p + n_s) // tm,),
        in_specs=in_specs,
        out_specs=out_specs,
        out_shape=out_shape,
        compiler_params=pltpu.CompilerParams(
            dimension_semantics=("arbitrary",), vmem_limit_bytes=_vmem_limit(blocks)),
        name="row_norm",
    )(*args)


def _load_weight(w_ref, copy_ref):
    w = w_ref[...]
    if copy_ref is not None:
        w = w.astype(BF16)
        copy_ref[...] = w
    return w


class _SideCast:
    def __init__(self, src3, layer, block, n_i, n_j):
        _, r, c = src3.shape
        br, tn = block
        ncb = c // tn
        n_blocks = (r // br) * ncb
        assert n_blocks <= n_i * n_j and r % br == 0 and c % tn == 0

        def blk(i, j):
            t = jnp.minimum(i * n_j + j, n_blocks - 1)
            return t // ncb, t % ncb

        self.arg = src3
        self.in_spec = pl.BlockSpec((None, br, tn), lambda i, j: (layer, *blk(i, j)))
        self.out_spec = pl.BlockSpec((None, br, tn), lambda i, j: (blk(i, j)[1], blk(i, j)[0], 0))
        self.out_shape = jax.ShapeDtypeStruct((ncb, r, tn), BF16)
        self.block_bytes = _nbytes(block, F32) + _nbytes(block, BF16)


def _ret_proj_kernel(h_ref, w_ref, cos_ref, sin_ref, *rest, tn, d_qk, head_dk, k_scale, cast_w, has_prev, n_side):
    n_in = (1 if has_prev else 0) + n_side
    o_ref = rest[n_in]
    for s in range(n_side):
        rest[len(rest) - n_side + s][...] = rest[n_in - n_side + s][...].astype(BF16)
    w = _load_weight(w_ref, rest[n_in + 1] if cast_w else None)
    acc = jnp.dot(h_ref[...], w, preferred_element_type=F32)
    col0 = pl.program_id(1) * tn
    is_qk = col0 < 2 * d_qk
    scale = jnp.where(jnp.logical_and(is_qk, col0 >= d_qk), k_scale, 1.0).astype(F32)
    cs = jnp.where(is_qk, cos_ref[...], 1.0) * scale
    sn = jnp.where(is_qk, sin_ref[...], 0.0) * scale
    half = head_dk // 2
    for hh in range(tn // head_dk):
        x1 = acc[:, hh * head_dk:hh * head_dk + half]
        x2 = acc[:, hh * head_dk + half:(hh + 1) * head_dk]
        o_ref[:, hh * head_dk:hh * head_dk + half] = (x1 * cs - x2 * sn).astype(o_ref.dtype)
        o_ref[:, hh * head_dk + half:(hh + 1) * head_dk] = (x1 * sn + x2 * cs).astype(o_ref.dtype)


def _ret_proj_call(h, w3, w_layer, lay, sides, rope_tables, tm=1024, tn=1024, tn_first=512):
    n, d = h.shape
    n_cols = w3.shape[2]
    d_qk = RET_HEADS * lay["dk"]
    half = lay["dk"] // 2

    per_tile = tn // tn_first

    def run(i0, n_i, tn_, weights, cast_w, prev, side_jobs):
        w_spec = (pl.BlockSpec((None, d, tn_), lambda i, j: (w_layer, 0, j)) if cast_w
                  else pl.BlockSpec((None, d, tn_), lambda i, j: (j, 0, 0)))
        in_specs = [pl.BlockSpec((tm, d), lambda i, j: (i + i0, 0)), w_spec,
                    pl.BlockSpec((tm, half), lambda i, j: (i + i0, 0)),
                    pl.BlockSpec((tm, half), lambda i, j: (i + i0, 0))]
        args = [h, weights, *rope_tables]
        aliases = {}
        if prev is not None:
            in_specs.append(pl.BlockSpec(memory_space=pl.ANY))
            args.append(prev)
            aliases = {len(args) - 1: 0}
        out_specs = [pl.BlockSpec((tm, tn_), lambda i, j: (i + i0, j))]
        out_shape = [jax.ShapeDtypeStruct((n, n_cols), BF16)]
        w_bytes = _nbytes((d, tn_), BF16)
        if cast_w:
            out_specs.append(pl.BlockSpec((None, d, tn_), lambda i, j: (j // per_tile, 0, j % per_tile)))
            out_shape.append(jax.ShapeDtypeStruct((n_cols // tn, d, tn), BF16))
            w_bytes = _nbytes((d, tn_), F32) + 2 * _nbytes((d, tn_), BF16)
        casts = [_SideCast(*job, n_i, n_cols // tn_) for job in side_jobs]
        in_specs += [sc.in_spec for sc in casts]
        args += [sc.arg for sc in casts]
        out_specs += [sc.out_spec for sc in casts]
        out_shape += [sc.out_shape for sc in casts]
        w_bytes += sum(sc.block_bytes for sc in casts)
        blocks = _nbytes((tm, d), BF16) + w_bytes + 2 * _nbytes((tm, half), F32) + _nbytes((tm, tn_), BF16)
        return pl.pallas_call(
            functools.partial(_ret_proj_kernel, tn=tn_, d_qk=d_qk, head_dk=lay["dk"],
                              k_scale=float(lay["dk"]) ** -0.5, cast_w=cast_w,
                              has_prev=prev is not None, n_side=len(casts)),
            grid=(n_i, n_cols // tn_),
            in_specs=in_specs,
            out_specs=out_specs,
            out_shape=out_shape,
            input_output_aliases=aliases,
            compiler_params=pltpu.CompilerParams(
                dimension_semantics=("arbitrary", "arbitrary"),
                vmem_limit_bytes=_vmem_limit(blocks, _nbytes((tm, tn_), F32))),
            name="ret_in_proj" + ("_first" if cast_w else ""),
        )(*args)

    out0, w_bf = run(0, 1, tn_first, w3, True, None, [])
    out, *side_w = run(1, n // tm - 1, tn, w_bf, False, out0, sides)
    return out, side_w


def _ffn_in_kernel(x_ref, gam_ref, sc_ref, sh_ref, wg_ref, wu_ref, *rest, tm, sub, has_side):
    if has_side:
        side_in, o_ref, side_out, h_ref = rest
        side_out[...] = side_in[...].astype(BF16)
    else:
        o_ref, h_ref = rest

    @pl.when(pl.program_id(1) == 0)
    def _():
        _norm_mod_rows(x_ref, gam_ref, sc_ref, sh_ref, h_ref, tm=tm, sub=sub)

    h = h_ref[...]
    gate = jnp.dot(h, wg_ref[...], preferred_element_type=F32)
    up = jnp.dot(h, wu_ref[...], preferred_element_type=F32)
    o_ref[...] = (_silu(gate) * up).astype(o_ref.dtype)


def _ffn_in_call(x, gam, mod, layer, w, lay, side=None, tm=1024):
    n, d = x.shape
    sub = lay["sub"]
    tn = w.shape[2]
    n_j = w.shape[0] // 2
    d_ff = n_j * tn
    _, midx = _tiling(lay, tm)
    in_specs = [
        pl.BlockSpec((tm, d), lambda i, j: (i, 0)),
        pl.BlockSpec((1, d), lambda i, j: (0, 0)),
        pl.BlockSpec((None, sub, d), lambda i, j: (layer, midx(i), 4)),
        pl.BlockSpec((None, sub, d), lambda i, j: (layer, midx(i), 3)),
        pl.BlockSpec((None, d, tn), lambda i, j: (j, 0, 0)),
        pl.BlockSpec((None, d, tn), lambda i, j: (j + n_j, 0, 0)),
    ]
    args = [x, gam.reshape(1, d), mod, mod, w, w]
    out_specs = [pl.BlockSpec((tm, tn), lambda i, j: (i, j))]
    out_shape = [jax.ShapeDtypeStruct((n, d_ff), BF16)]
    blocks = (_nbytes((tm, d), F32) + 2 * _nbytes((sub, d), F32) + 2 * _nbytes((d, tn), BF16)
              + _nbytes((tm, tn), BF16))
    if side is not None:
        sc = _SideCast(*side, n // tm, n_j)
        in_specs.append(sc.in_spec)
        args.append(sc.arg)
        out_specs.append(sc.out_spec)
        out_shape.append(sc.out_shape)
        blocks += sc.block_bytes
    res = pl.pallas_call(
        functools.partial(_ffn_in_kernel, tm=tm, sub=sub, has_side=side is not None),
        grid=(n // tm, n_j),
        in_specs=in_specs,
        out_specs=out_specs,
        out_shape=out_shape,
        scratch_shapes=[pltpu.VMEM((tm, d), BF16)],
        compiler_params=pltpu.CompilerParams(
            dimension_semantics=("arbitrary", "arbitrary"),
            vmem_limit_bytes=_vmem_limit(blocks, _nbytes((tm, d), BF16) + 2 * _nbytes((tm, tn), F32))),
        name="ffn_in_swiglu",
    )(*args)
    return res[0], (res[1] if side is not None else None)


def _out_proj_kernel(*refs, tm, sub, npt, n_src, has_side, emit_ssq):
    a_refs, w_ref = refs[:n_src], refs[n_src]
    x_refs, gate_ref = refs[n_src + 1:2 * n_src + 1], refs[2 * n_src + 1]
    rest = list(refs[2 * n_src + 2:])
    ssq_ref = rest.pop() if emit_ssq else None
    if has_side:
        side_in, o_ref, side_out = rest
        side_out[...] = side_in[...].astype(BF16)
    else:
        (o_ref,) = rest

    def body(a_ref, x_ref):
        acc = jnp.dot(a_ref[...], w_ref[...], preferred_element_type=F32)
        gate = gate_ref[...]
        for s in range(tm // sub):
            rows = slice(s * sub, (s + 1) * sub)
            x_new = x_ref[rows, :] + gate * acc[rows, :]
            o_ref[rows, :] = x_new
            if emit_ssq:
                ssq_ref[rows, :] = jnp.sum(x_new * x_new, axis=-1, keepdims=True)

    if n_src == 1:
        body(a_refs[0], x_refs[0])
    else:
        is_prompt = pl.program_id(0) < npt
        pl.when(is_prompt)(functools.partial(body, a_refs[0], x_refs[0]))
        pl.when(jnp.logical_not(is_prompt))(functools.partial(body, a_refs[1], x_refs[1]))


def _out_proj_call(srcs, w, mod, layer, gate_col, lay, side=None, emit_ssq=False, tm=1024):
    n = sum(a.shape[0] for a, _ in srcs)
    k = srcs[0][0].shape[1]
    n_j, _, tn = w.shape
    d = n_j * tn
    sub = lay["sub"]
    npt, midx = _tiling(lay, tm)
    if len(srcs) == 1:
        a_specs = [pl.BlockSpec((tm, k), lambda i, j: (i, 0))]
        x_specs = [pl.BlockSpec((tm, tn), lambda i, j: (i, j))]
        a_bytes = 2 * _nbytes((tm, k), BF16)
    else:
        a_specs = [pl.BlockSpec((tm, k), lambda i, j: (jnp.minimum(i, npt - 1), 0)),
                   pl.BlockSpec((tm, k), lambda i, j: (jnp.maximum(i - npt, 0), 0), pipeline_mode=pl.Buffered(1))]
        x_specs = [pl.BlockSpec((tm, tn), lambda i, j: (jnp.minimum(i, npt - 1), j)),
                   pl.BlockSpec((tm, tn), lambda i, j: (jnp.maximum(i - npt, 0), j))]
        a_bytes = 3 * _nbytes((tm, k), BF16)
    in_specs = (a_specs + [pl.BlockSpec((None, k, tn), lambda i, j: (j, 0, 0))] + x_specs
                + [pl.BlockSpec((None, sub, tn), lambda i, j: (layer, midx(i), gate_col * n_j + j))])
    args = [a for a, _ in srcs] + [w] + [x for _, x in srcs] + [mod]
    out_specs = [pl.BlockSpec((tm, tn), lambda i, j: (i, j))]
    out_shape = [jax.ShapeDtypeStruct((n, d), F32)]
    blocks = (a_bytes // 2 + _nbytes((k, tn), BF16) + (1 + len(srcs)) * _nbytes((tm, tn), F32)
              + _nbytes((sub, tn), F32))
    if side is not None:
        sc = _SideCast(*side, n // tm, n_j)
        in_specs.append(sc.in_spec)
        args.append(sc.arg)
        out_specs.append(sc.out_spec)
        out_shape.append(sc.out_shape)
        blocks += sc.block_bytes
    if emit_ssq:
        out_specs.append(pl.BlockSpec((None, tm, 1), lambda i, j: (j, i, 0)))
        out_shape.append(jax.ShapeDtypeStruct((n_j, n, 1), F32))
        blocks += _nbytes((tm, V7X_LANES), F32)
    res = pl.pallas_call(
        functools.partial(_out_proj_kernel, tm=tm, sub=sub, npt=npt, n_src=len(srcs), has_side=side is not None,
                          emit_ssq=emit_ssq),
        grid=(n // tm, n_j),
        in_specs=in_specs,
        out_specs=out_specs,
        out_shape=out_shape,
        compiler_params=pltpu.CompilerParams(
            dimension_semantics=("arbitrary", "arbitrary"),
            vmem_limit_bytes=_vmem_limit(blocks, _nbytes((tm, tn), F32))),
        name="out_proj_residual",
    )(*args)
    return res[0], (res[1] if side is not None else None), (res[-1] if emit_ssq else None)


def _retention_decays(c):
    lg = jnp.log1p(-jnp.exp2(-5.0 - jnp.arange(RET_HEADS, dtype=F32)))
    idx = jnp.arange(c, dtype=F32)
    diff = idx[:, None] - idx[None, :]
    causal = diff >= 0
    dmask = jnp.where(causal[None], jnp.exp(jnp.where(causal, diff, 0.0)[None] * lg[:, None, None]), 0.0)
    qdec = jnp.exp((idx + 1.0)[None, :] * lg[:, None])
    kdec = jnp.exp((c - 1.0 - idx)[None, :] * lg[:, None])
    sdec = jnp.exp(c * lg)
    return dmask, qdec[:, :, None], kdec[:, :, None], sdec[:, None, None]


def _group_norm_gate(o, g, gng):
    mu = jnp.mean(o, axis=-1, keepdims=True)
    dlt = o - mu
    var = jnp.mean(dlt * dlt, axis=-1, keepdims=True)
    return _silu(g) * (dlt * lax.rsqrt(var + EPS) * gng)


_NT = (((1,), (1,)), ((), ()))
_TN = (((0,), (0,)), ((), ()))


def _retention_chunk(q, k, v, s_prev, dmask, qdec, kdec, sdec):
    scores = lax.dot_general(q, k, _NT, preferred_element_type=F32) * dmask
    o = (jnp.dot(scores.astype(BF16), v, preferred_element_type=F32)
         + qdec * jnp.dot(q, s_prev.astype(BF16), preferred_element_type=F32))
    kd = (k.astype(F32) * kdec).astype(BF16)
    s_new = sdec * s_prev + lax.dot_general(kd, v, _TN, preferred_element_type=F32)
    return o, s_new


def _retention_kernel(pq_ref, pk_ref, pv_ref, pg_ref, pgng_ref, pdmask_ref, pqdec_ref, pkdec_ref, psdec_ref,
                      sq_ref, sk_ref, sv_ref, sg_ref, sgng_ref, s0_ref, sdmask_ref, sqdec_ref, skdec_ref, ssdec_ref,
                      po_ref, ps_out_ref, so_ref, ss_out_ref,
                      ps_ref, qf_ref, kf_ref, vf_ref, of_ref, *, chunk, n_chunks, bb):
    @pl.when(pl.program_id(0) % 2 == 0)
    def _():
        ps_ref[...] = jnp.zeros_like(ps_ref)

    pdmask, pqdec, pkdec, psdec, pgng = pdmask_ref[0], pqdec_ref[0], pkdec_ref[0], psdec_ref[0], pgng_ref[...]

    def prompt_body(c, carry):
        rows = pl.ds(pl.multiple_of(c * chunk, chunk), chunk)
        o, s_new = _retention_chunk(pq_ref[rows, :], pk_ref[rows, :], pv_ref[rows, :], ps_ref[...],
                                    pdmask, pqdec, pkdec, psdec)
        ps_ref[...] = s_new
        po_ref[rows, :] = _group_norm_gate(o, pg_ref[rows, :].astype(F32), pgng).astype(po_ref.dtype)
        return carry

    lax.fori_loop(0, n_chunks, prompt_body, 0, unroll=2)
    ps_out_ref[0, 0] = ps_ref[...]

    smask, sqdec, skdec, ssdec, sgng = sdmask_ref[0], sqdec_ref[0], skdec_ref[0], ssdec_ref[0], sgng_ref[...]
    ts, _, dk = sq_ref.shape
    dv = sv_ref.shape[2]
    rows = ts * bb
    scores = lax.dot_general(sq_ref[...].reshape(rows, dk), sk_ref[...].reshape(rows, dk), _NT,
                             preferred_element_type=F32) * smask
    o_intra = jnp.dot(scores.astype(BF16), sv_ref[...].reshape(rows, dv), preferred_element_type=F32)
    qf_ref[...] = sq_ref[...].astype(F32)
    kf_ref[...] = sk_ref[...].astype(F32)
    vf_ref[...] = sv_ref[...].astype(F32)

    def sample_body(b, carry):
        s0 = s0_ref[b, 0]
        of_ref[:, b, :] = jnp.dot(qf_ref[:, b, :].astype(BF16), s0.astype(BF16), preferred_element_type=F32)
        kd = (kf_ref[:, b, :] * skdec).astype(BF16)
        ss_out_ref[b, 0] = ssdec * s0 + lax.dot_general(kd, vf_ref[:, b, :].astype(BF16), _TN,
                                                        preferred_element_type=F32)
        return carry

    lax.fori_loop(0, bb, sample_body, 0, unroll=4)
    o = o_intra + sqdec * of_ref[...].reshape(rows, dv)
    gated = _group_norm_gate(o, sg_ref[...].reshape(rows, dv).astype(F32), sgng)
    so_ref[...] = gated.astype(so_ref.dtype).reshape(ts, bb, dv)


def _retention_call(qkvg, gng, state, lay, chunk=2 * RET_CHUNK, bb=16):
    n = qkvg.shape[0]
    bp, tp, bs, ts, dk, dv = lay["n_prompt"], lay["t_prompt"], lay["sub"], lay["t_sample"], lay["dk"], lay["dv"]
    h = RET_HEADS
    n_p = lay["n_prompt_rows"]
    half = tp // 2
    n_steps = 2 * bp * h
    assert (bs // bb) * h == n_steps and half % chunk == 0
    t_blk = n_p // (bs * ts)
    assert t_blk * bs * ts == n_p
    v_blk0 = 2 * h * dk // dv
    g_blk0 = v_blk0 + h
    pdec = _retention_decays(chunk)
    dmask_s, qdec_s, kdec_s, sdec_s = _retention_decays(ts)
    r = jnp.arange(ts * bb)
    same_seq = (r[:, None] % bb) == (r[None, :] % bb)
    sdec = (jnp.where(same_seq[None], dmask_s[:, r[:, None] // bb, r[None, :] // bb], 0.0),
            jnp.repeat(qdec_s, bb, axis=1), kdec_s, sdec_s)
    qkv3 = qkvg.reshape(n // bs, bs, qkvg.shape[1])
    gng2 = gng.reshape(1, h * dv)

    p_row = lambda s: (s // 2 // h) * 2 + s % 2
    p_head = lambda s: (s // 2) % h
    s_blk = lambda s: s // h
    s_head = lambda s: s % h
    dec_specs = lambda c, head: [
        pl.BlockSpec((1, c, c), lambda s: (head(s), 0, 0)),
        pl.BlockSpec((1, c, 1), lambda s: (head(s), 0, 0)),
        pl.BlockSpec((1, c, 1), lambda s: (head(s), 0, 0)),
        pl.BlockSpec((1, 1, 1), lambda s: (head(s), 0, 0)),
    ]
    in_specs = [
        pl.BlockSpec((half, dk), lambda s: (p_row(s), p_head(s))),
        pl.BlockSpec((half, dk), lambda s: (p_row(s), h + p_head(s))),
        pl.BlockSpec((half, dv), lambda s: (p_row(s), v_blk0 + p_head(s))),
        pl.BlockSpec((half, dv), lambda s: (p_row(s), g_blk0 + p_head(s))),
        pl.BlockSpec((1, dv), lambda s: (0, p_head(s))),
        *dec_specs(chunk, p_head),
        pl.BlockSpec((ts, bb, dk), lambda s: (t_blk, s_blk(s), s_head(s))),
        pl.BlockSpec((ts, bb, dk), lambda s: (t_blk, s_blk(s), h + s_head(s))),
        pl.BlockSpec((ts, bb, dv), lambda s: (t_blk, s_blk(s), v_blk0 + s_head(s))),
        pl.BlockSpec((ts, bb, dv), lambda s: (t_blk, s_blk(s), g_blk0 + s_head(s))),
        pl.BlockSpec((1, dv), lambda s: (0, s_head(s))),
        pl.BlockSpec((bb, 1, dk, dv), lambda s: (s_blk(s), s_head(s), 0, 0)),
        pl.BlockSpec((1, ts * bb, ts * bb), lambda s: (s_head(s), 0, 0)),
        pl.BlockSpec((1, ts * bb, 1), lambda s: (s_head(s), 0, 0)),
        pl.BlockSpec((1, ts, 1), lambda s: (s_head(s), 0, 0)),
        pl.BlockSpec((1, 1, 1), lambda s: (s_head(s), 0, 0)),
    ]
    out_specs = [
        pl.BlockSpec((half, dv), lambda s: (p_row(s), p_head(s))),
        pl.BlockSpec((1, 1, dk, dv), lambda s: (s // 2 // h, p_head(s), 0, 0)),
        pl.BlockSpec((ts, bb, dv), lambda s: (0, s_blk(s), s_head(s))),
        pl.BlockSpec((bb, 1, dk, dv), lambda s: (s_blk(s), s_head(s), 0, 0)),
    ]
    out_shape = [jax.ShapeDtypeStruct((n_p, h * dv), BF16),
                 jax.ShapeDtypeStruct((bp, h, dk, dv), F32),
                 jax.ShapeDtypeStruct((ts, bs, h * dv), BF16),
                 jax.ShapeDtypeStruct((bs, h, dk, dv), F32)]
    blocks = (2 * _nbytes((half, dk), BF16) + 3 * _nbytes((half, dv), BF16) + _nbytes((dk, dv), F32)
              + _nbytes((chunk, chunk), F32)
              + 2 * _nbytes((ts, bb, dk), BF16) + 3 * _nbytes((ts, bb, dv), BF16) + 2 * _nbytes((bb, dk, dv), F32))
    scratch = _nbytes((dk, dv), F32) + 2 * _nbytes((ts, bb, dk), F32) + 2 * _nbytes((ts, bb, dv), F32)
    gated_p, s_p, gated_s, s_s = pl.pallas_call(
        functools.partial(_retention_kernel, chunk=chunk, n_chunks=half // chunk, bb=bb),
        grid=(n_steps,),
        in_specs=in_specs,
        out_specs=out_specs,
        out_shape=out_shape,
        scratch_shapes=[pltpu.VMEM((dk, dv), F32),
                        pltpu.VMEM((ts, bb, dk), F32), pltpu.VMEM((ts, bb, dk), F32),
                        pltpu.VMEM((ts, bb, dv), F32), pltpu.VMEM((ts, bb, dv), F32)],
        compiler_params=pltpu.CompilerParams(
            dimension_semantics=("arbitrary",),
            vmem_limit_bytes=_vmem_limit(blocks, scratch)),
        name="retention",
    )(qkvg, qkvg, qkvg, qkvg, gng2, *pdec, qkv3, qkv3, qkv3, qkv3, gng2, state, *sdec)
    return gated_p, gated_s.reshape(bs * ts, h * dv), s_p, s_s


def _window_sum_rows(win_ref, halo, r0, rows, win, stride):
    if stride % V7X_SUBLANES == 0:
        acc = win_ref[pl.ds(halo + r0, rows), :]
        for k in range(1, win):
            acc = acc + win_ref[pl.ds(halo - k * stride + r0, rows), :]
        return acc
    assert stride == 1 and halo == 16 and win <= 16
    blk = win_ref[pl.ds(r0, rows + 16), :]
    shift = 1
    while shift < win:
        blk = blk + pltpu.roll(blk, shift, 0)
        shift *= 2
    return blk[16:, :]


def _pool_kernel(ssq_ref, gam_ref, sc_ref, sh_ref, hist_ref, x_ref, gate_ref, w_ref, scale_ref, *rest,
                 tm, sub, stride, tiles_per_seq, hist_len, chunk_rows, d_model):
    o_ref, hist_out_ref, win_ref, m_ref = rest[-4:]
    g = pl.program_id(0)
    i = pl.program_id(1)
    halo = 16 * stride
    tile_in_seq = i % tiles_per_seq

    if tiles_per_seq > 1:
        @pl.when(tile_in_seq != 0)
        def _():
            win_ref[0:halo, :] = win_ref[tm:tm + halo, :]

    @pl.when(tile_in_seq == 0)
    def _():
        win_ref[0:halo, :] = hist_ref[0]

    gam = gam_ref[...]
    for s in range(tm // sub):
        rows = slice(s * sub, (s + 1) * sub)
        ssq = ssq_ref[0, rows, :]
        for c in range(1, ssq_ref.shape[0]):
            ssq = ssq + ssq_ref[c, rows, :]
        rs = lax.rsqrt(ssq / d_model + EPS)
        win_ref[halo + s * sub:halo + (s + 1) * sub, :] = (
            ((x_ref[rows, :] * rs) * gam) * (1.0 + sc_ref[...]) + sh_ref[...])

    for gi, win in enumerate(POOL_WINDOWS):
        @pl.when(g == gi)
        def _(win=win):
            def body(c, carry):
                r0 = pl.multiple_of(c * chunk_rows, chunk_rows)
                cur = win_ref[pl.ds(halo + r0, chunk_rows), :]
                acc = _window_sum_rows(win_ref, halo, r0, chunk_rows, win, stride)
                t = (tile_in_seq * tm + r0 + lax.broadcasted_iota(jnp.int32, (chunk_rows, 1), 0)) // stride
                cnt = jnp.minimum(win, t + 1 + hist_len).astype(F32)
                m_ref[pl.ds(r0, chunk_rows), :] = (acc / cnt - cur).astype(m_ref.dtype)
                return carry

            lax.fori_loop(0, tm // chunk_rows, body, 0)

    y = jnp.dot(m_ref[...], w_ref[0], preferred_element_type=F32) * scale_ref[...]
    gate = gate_ref[...]
    for s in range(tm // sub):
        rows = slice(s * sub, (s + 1) * sub)
        o_ref[rows, :] = x_ref[rows, :] + gate * y[rows, :]
    hist_out_ref[0] = win_ref[tm + stride:tm + halo, :]


def _pool_call(ssq, gam, hist, x, mod, layer, w, scale, x_prev, *, row0, n_rows, n_seq, stride, hist_len, mod_seq0,
               lay, tm=1024):
    n, d = x.shape
    sub = lay["sub"]
    n_g = len(POOL_WINDOWS)
    gc = d // n_g
    n_tiles = n_rows // tm
    tps = n_tiles // n_seq
    blk0 = row0 // tm
    halo = 16 * stride
    n_parts = ssq.shape[0]
    blocks = (2 * _nbytes((tm, gc), F32) + _nbytes((halo, gc), F32) + 3 * _nbytes((sub, gc), F32)
              + _nbytes((gc, gc), BF16) + _nbytes((15 * stride, gc), F32) + n_parts * _nbytes((tm, V7X_LANES), F32))
    scratch = _nbytes((halo + tm, gc), F32) + _nbytes((tm, gc), BF16)
    sh_col0, sc_col0, gate_col0 = 0, n_g, 2 * n_g
    mod_blk = lambda col0: pl.BlockSpec((None, sub, gc), lambda g, i: (layer, mod_seq0 + i // tps, col0 + g))
    in_specs = [
        pl.BlockSpec((n_parts, tm, 1), lambda g, i: (0, blk0 + i, 0)),
        pl.BlockSpec((1, gc), lambda g, i: (0, g)),
        mod_blk(sc_col0),
        mod_blk(sh_col0),
        pl.BlockSpec((1, halo, gc), lambda g, i: (i // tps, 0, g)),
        pl.BlockSpec((tm, gc), lambda g, i: (blk0 + i, g)),
        mod_blk(gate_col0),
        pl.BlockSpec((1, gc, gc), lambda g, i: (g, 0, 0)),
        pl.BlockSpec((1, gc), lambda g, i: (0, g)),
    ]
    args = [ssq, gam.reshape(1, d), mod, mod, hist, x, mod, w, scale.reshape(1, d)]
    aliases = {}
    if x_prev is not None:
        in_specs.append(pl.BlockSpec(memory_space=pl.ANY))
        args.append(x_prev)
        aliases = {len(args) - 1: 0}
    return pl.pallas_call(
        functools.partial(_pool_kernel, tm=tm, sub=sub, stride=stride, tiles_per_seq=tps,
                          hist_len=hist_len, chunk_rows=64, d_model=d),
        grid=(n_g, n_tiles),
        in_specs=in_specs,
        out_specs=[
            pl.BlockSpec((tm, gc), lambda g, i: (blk0 + i, g)),
            pl.BlockSpec((1, 15 * stride, gc), lambda g, i: (i // tps, 0, g)),
        ],
        out_shape=[jax.ShapeDtypeStruct((n, d), F32),
                   jax.ShapeDtypeStruct((n_seq, 15 * stride, d), F32)],
        scratch_shapes=[pltpu.VMEM((halo + tm, gc), F32), pltpu.VMEM((tm, gc), BF16)],
        input_output_aliases=aliases,
        compiler_params=pltpu.CompilerParams(
            dimension_semantics=("arbitrary", "arbitrary"),
            vmem_limit_bytes=_vmem_limit(blocks, scratch)),
        name="pool_mixer_s%d" % stride,
    )(*args)


def _rope_tables(pos, dk):
    half = dk // 2
    inv = 1.0 / (ROPE_BASE ** (jnp.arange(half, dtype=F32) / half))
    ang = pos[:, None] * inv[None, :]
    return jnp.cos(ang), jnp.sin(ang)


def kernel(x_prompt, x_sample, c_prompt, c_sample, state_ret, state_pool, norm_mix_g, norm_ffn_g, ada_w, ada_b,
           ret_w_in, ret_gn_g, ret_w_out, pool_w, pool_scale, ffn_w_in, ffn_w_out, final_norm_g):
    bp, tp, d = x_prompt.shape
    bs, ts, _ = x_sample.shape
    depth = ada_w.shape[0]
    dk = state_ret.shape[3]
    dv = state_ret.shape[4]
    n_hist = state_pool.shape[2]
    assert bp <= C_PAD and n_hist == max(POOL_WINDOWS) - 1 and depth == 2
    n_p = bp * tp
    n_s = bs * ts
    lay = dict(n_prompt=bp, t_prompt=tp, n_prompt_rows=n_p, n_sample_rows=n_s, sub=bs, t_sample=ts, dk=dk, dv=dv)

    xp = x_prompt.reshape(n_p, d)
    xs = x_sample.transpose(1, 0, 2).reshape(n_s, d)
    c_all = jnp.concatenate([c_prompt, jnp.zeros((C_PAD - bp, d), F32), c_sample], axis=0)
    mod = _ada_call(c_all, ada_w, ada_b, bp, bs)

    cos_p, sin_p = _rope_tables(jnp.arange(tp, dtype=F32), dk)
    cos_s, sin_s = _rope_tables(float(PAST_LEN) + jnp.arange(ts, dtype=F32), dk)
    cos = jnp.concatenate([jnp.tile(cos_p, (bp, 1)), jnp.repeat(cos_s, bs, axis=0)], axis=0)
    sin = jnp.concatenate([jnp.tile(sin_p, (bp, 1)), jnp.repeat(sin_s, bs, axis=0)], axis=0)

    h0 = _norm_call((xp, xs), norm_mix_g[0], mod, 0, 1, 0, lay, BF16)
    tn_ffn = 512
    d_ff = ffn_w_out.shape[1]
    cast_in = (d // 2, tn_ffn)
    cast_out = (d_ff // 8, tn_ffn)
    qkvg, (w_ret_out, w_ffn_in0) = _ret_proj_call(
        h0, ret_w_in, 0, lay, [(ret_w_out, 0, (ret_w_out.shape[1] // 8, tn_ffn)), (ffn_w_in, 0, cast_in)], (cos, sin))
    gated_p, gated_s, s_ret_p, s_ret_s = _retention_call(qkvg, ret_gn_g[0], state_ret[0], lay)
    x, w_ffn_out0, _ = _out_proj_call([(gated_p, xp), (gated_s, xs)], w_ret_out, mod, 0, 2, lay,
                                      side=(ffn_w_out, 0, cast_out))
    hmid, w_ffn_in1 = _ffn_in_call(x, norm_ffn_g[0], mod, 0, w_ffn_in0, lay, side=(ffn_w_in, 1, cast_in))
    x, w_ffn_out1, ssq = _out_proj_call([(hmid, x)], w_ffn_out0, mod, 0, 5, lay, side=(ffn_w_out, 1, cast_out),
                                        emit_ssq=True)

    pw = pool_w[0].astype(BF16)
    hist_p = jnp.zeros((bp, 16, d), F32)
    hist_s = jnp.concatenate([jnp.zeros((1, bs, d), F32), state_pool[0].transpose(1, 0, 2)], axis=0)
    x1, nh_p = _pool_call(ssq, norm_mix_g[1], hist_p, x, mod, 1, pw, pool_scale[0], None, row0=0, n_rows=n_p,
                          n_seq=bp, stride=1, hist_len=0, mod_seq0=0, lay=lay)
    x, nh_s = _pool_call(ssq, norm_mix_g[1], hist_s.reshape(1, 16 * bs, d), x, mod, 1, pw, pool_scale[0], x1,
                         row0=n_p, n_rows=n_s, n_seq=1, stride=bs, hist_len=min(PAST_LEN, n_hist), mod_seq0=bp,
                         lay=lay)
    hmid, _ = _ffn_in_call(x, norm_ffn_g[1], mod, 1, w_ffn_in1, lay)
    x, _, _ = _out_proj_call([(hmid, x)], w_ffn_out1, mod, 1, 5, lay)

    y_p, y_s = _norm_call((x,), final_norm_g, None, 0, 0, 0, lay, F32, split_out=True)
    y_prompt = y_p.reshape(bp, tp, d)
    y_sample = y_s.reshape(ts, bs, d).transpose(1, 0, 2)
    state_pool_sample = nh_s.reshape(n_hist, bs, d).transpose(1, 0, 2)
    return (y_prompt, y_sample, s_ret_p[None], nh_p[None], s_ret_s[None], state_pool_sample[None])
```

```python
import functools

import jax
import jax.numpy as jnp
from jax import lax
from jax.experimental import pallas as pl
from jax.experimental.pallas import tpu as pltpu

F32 = jnp.float32
BF16 = jnp.bfloat16

RET_HEADS = 8
RET_CHUNK = 128
ROPE_BASE = 10000.0
POOL_WINDOWS = (2, 4, 8, 16)
N_ADA = 6
EPS = 1e-6
PAST_LEN = 16384

V7X_VMEM_BYTES = 64 * 1024 * 1024
V7X_LANES = 128
V7X_SUBLANES = 8
NORM_ROWS = 16
C_PAD = 8


def _vmem_limit(block_bytes, scratch_bytes=0):
    need = 2 * block_bytes + scratch_bytes + 12 * 1024 * 1024
    return int(min(need, V7X_VMEM_BYTES - 6 * 1024 * 1024))


def _nbytes(shape, dtype):
    n = 1
    for s in shape:
        n *= s
    return n * jnp.dtype(dtype).itemsize


def _silu(x):
    return x / (1.0 + jnp.exp(-x))


def _ada_kernel(c_ref, w_ref, b_ref, o_ref, *, n_prompt, sub):
    cs = _silu(c_ref[...])
    res = jnp.dot(cs.astype(BF16), w_ref[0].astype(BF16), preferred_element_type=F32) + b_ref[0]
    tn = res.shape[-1]
    for b in range(n_prompt):
        o_ref[0, b * sub:(b + 1) * sub, :] = jnp.broadcast_to(res[b:b + 1, :], (sub, tn))
    o_ref[0, n_prompt * sub:, :] = res[C_PAD:C_PAD + sub, :]


def _ada_call(c_all, ada_w, ada_b, n_prompt, sub, tn=1024):
    depth, d, n6 = ada_w.shape
    rows = c_all.shape[0]
    out_rows = (n_prompt + 1) * sub
    blocks = (_nbytes((rows, d), F32) + _nbytes((d, tn), F32) + _nbytes((1, tn), F32)
              + _nbytes((out_rows, tn), F32))
    return pl.pallas_call(
        functools.partial(_ada_kernel, n_prompt=n_prompt, sub=sub),
        grid=(depth, n6 // tn),
        in_specs=[
            pl.BlockSpec((rows, d), lambda l, j: (0, 0)),
            pl.BlockSpec((1, d, tn), lambda l, j: (l, 0, j)),
            pl.BlockSpec((1, 1, tn), lambda l, j: (l, 0, j)),
        ],
        out_specs=pl.BlockSpec((1, out_rows, tn), lambda l, j: (l, 0, j)),
        out_shape=jax.ShapeDtypeStruct((depth, out_rows, n6), F32),
        compiler_params=pltpu.CompilerParams(
            dimension_semantics=("arbitrary", "arbitrary"),
            vmem_limit_bytes=_vmem_limit(blocks, _nbytes((d, tn), BF16))),
        name="ada_mod",
    )(c_all, ada_w, ada_b.reshape(depth, 1, n6))


def _norm_mod_rows(x_ref, gam_ref, sc_ref, sh_ref, out_ref, *, tm, sub, row0=0, unroll=4):
    gam = gam_ref[...]
    per_sub = sub // NORM_ROWS

    def body(c, carry):
        r0 = pl.multiple_of(row0 + c * NORM_ROWS, NORM_ROWS)
        x = x_ref[pl.ds(r0, NORM_ROWS), :]
        hn = (x * lax.rsqrt(jnp.mean(x * x, axis=-1, keepdims=True) + EPS)) * gam
        if sc_ref is not None:
            m0 = pl.multiple_of((c % per_sub) * NORM_ROWS, NORM_ROWS)
            hn = hn * (1.0 + sc_ref[pl.ds(m0, NORM_ROWS), :]) + sh_ref[pl.ds(m0, NORM_ROWS), :]
        out_ref[pl.ds(r0, NORM_ROWS), :] = hn.astype(out_ref.dtype)
        return carry

    lax.fori_loop(0, tm // NORM_ROWS, body, 0, unroll=unroll)


def _mod_index(i, n_prompt_tiles, tiles_per_seq, n_prompt):
    return jnp.where(i < n_prompt_tiles, i // tiles_per_seq, n_prompt)


def _tiling(lay, tm):
    npt = lay["n_prompt_rows"] // tm
    midx = functools.partial(_mod_index, n_prompt_tiles=npt, tiles_per_seq=lay["t_prompt"] // tm,
                             n_prompt=lay["n_prompt"])
    return npt, midx


def _norm_kernel(*refs, tm, sub, n_in, n_out, has_mod, npt):
    x_refs = refs[:n_in]
    gam_ref = refs[n_in]
    sc_ref, sh_ref = (refs[n_in + 1], refs[n_in + 2]) if has_mod else (None, None)
    o_refs = refs[len(refs) - n_out:]
    if n_in == 1 and n_out == 1:
        _norm_mod_rows(x_refs[0], gam_ref, sc_ref, sh_ref, o_refs[0], tm=tm, sub=sub)
        return
    i = pl.program_id(0)

    @pl.when(i < npt)
    def _():
        _norm_mod_rows(x_refs[0], gam_ref, sc_ref, sh_ref, o_refs[0], tm=tm, sub=sub)

    @pl.when(i >= npt)
    def _():
        _norm_mod_rows(x_refs[-1], gam_ref, sc_ref, sh_ref, o_refs[-1], tm=tm, sub=sub)


def _norm_call(xs, gam, mod, layer, col_sc, col_sh, lay, out_dtype, split_out=False, tm=512):
    d = xs[0].shape[1]
    sub = lay["sub"]
    n_p, n_s = lay["n_prompt_rows"], lay["n_sample_rows"]
    npt, midx = _tiling(lay, tm)
    prompt_blk = lambda i: (jnp.minimum(i, npt - 1), 0)
    sample_blk = lambda i: (jnp.maximum(i - npt, 0), 0)
    whole_blk = lambda i: (i, 0)
    in_specs = ([pl.BlockSpec((tm, d), whole_blk)] if len(xs) == 1
                else [pl.BlockSpec((tm, d), prompt_blk), pl.BlockSpec((tm, d), sample_blk)])
    in_specs.append(pl.BlockSpec((1, d), lambda i: (0, 0)))
    args = list(xs) + [gam.reshape(1, d)]
    if mod is not None:
        in_specs += [pl.BlockSpec((None, sub, d), lambda i: (layer, midx(i), col_sc)),
                     pl.BlockSpec((None, sub, d), lambda i: (layer, midx(i), col_sh))]
        args += [mod, mod]
    if split_out:
        out_specs = [pl.BlockSpec((tm, d), prompt_blk), pl.BlockSpec((tm, d), sample_blk)]
        out_shape = [jax.ShapeDtypeStruct((n_p, d), out_dtype), jax.ShapeDtypeStruct((n_s, d), out_dtype)]
    else:
        out_specs = pl.BlockSpec((tm, d), whole_blk)
        out_shape = jax.ShapeDtypeStruct((n_p + n_s, d), out_dtype)
    n_out = 2 if split_out else 1
    blocks = (len(xs) + n_out) * _nbytes((tm, d), F32) + 2 * _nbytes((sub, d), F32)
    return pl.pallas_call(
        functools.partial(_norm_kernel, tm=tm, sub=sub, n_in=len(xs), n_out=n_out,
                          has_mod=mod is not None, npt=npt),
        grid=((n_p + n_s) // tm,),
        in_specs=in_specs,
        out_specs=out_specs,
        out_shape=out_shape,
        compiler_params=pltpu.CompilerParams(
            dimension_semantics=("arbitrary",), vmem_limit_bytes=_vmem_limit(blocks)),
        name="row_norm",
    )(*args)


def _load_weight(w_ref, copy_ref):
    w = w_ref[...]
    if copy_ref is not None:
        w = w.astype(BF16)
        copy_ref[...] = w
    return w


class _SideCast:
    def __init__(self, src3, layer, block, n_i, n_j):
        _, r, c = src3.shape
        br, tn = block
        ncb = c // tn
        n_blocks = (r // br) * ncb
        assert n_blocks <= n_i * n_j and r % br == 0 and c % tn == 0

        def blk(i, j):
            t = jnp.minimum(i * n_j + j, n_blocks - 1)
            return t // ncb, t % ncb

        self.arg = src3
        self.in_spec = pl.BlockSpec((None, br, tn), lambda i, j: (layer, *blk(i, j)))
        self.out_spec = pl.BlockSpec((None, br, tn), lambda i, j: (blk(i, j)[1], blk(i, j)[0], 0))
        self.out_shape = jax.ShapeDtypeStruct((ncb, r, tn), BF16)
        self.block_bytes = _nbytes(block, F32) + _nbytes(block, BF16)


def _ret_proj_kernel(h_ref, w_ref, cos_ref, sin_ref, *rest, tn, d_qk, head_dk, k_scale, cast_w, has_prev, n_side):
    n_in = (1 if has_prev else 0) + n_side
    o_ref = rest[n_in]
    for s in range(n_side):
        rest[len(rest) - n_side + s][...] = rest[n_in - n_side + s][...].astype(BF16)
    w = _load_weight(w_ref, rest[n_in + 1] if cast_w else None)
    acc = jnp.dot(h_ref[...], w, preferred_element_type=F32)
    col0 = pl.program_id(1) * tn
    is_qk = col0 < 2 * d_qk
    scale = jnp.where(jnp.logical_and(is_qk, col0 >= d_qk), k_scale, 1.0).astype(F32)
    cs = jnp.where(is_qk, cos_ref[...], 1.0) * scale
    sn = jnp.where(is_qk, sin_ref[...], 0.0) * scale
    half = head_dk // 2
    for hh in range(tn // head_dk):
        x1 = acc[:, hh * head_dk:hh * head_dk + half]
        x2 = acc[:, hh * head_dk + half:(hh + 1) * head_dk]
        o_ref[:, hh * head_dk:hh * head_dk + half] = (x1 * cs - x2 * sn).astype(o_ref.dtype)
        o_ref[:, hh * head_dk + half:(hh + 1) * head_dk] = (x1 * sn + x2 * cs).astype(o_ref.dtype)


def _ret_proj_call(h, w3, w_layer, lay, sides, rope_tables, tm=1024, tn=1024, tn_first=512):
    n, d = h.shape
    n_cols = w3.shape[2]
    d_qk = RET_HEADS * lay["dk"]
    half = lay["dk"] // 2

    per_tile = tn // tn_first

    def run(i0, n_i, tn_, weights, cast_w, prev, side_jobs):
        w_spec = (pl.BlockSpec((None, d, tn_), lambda i, j: (w_layer, 0, j)) if cast_w
                  else pl.BlockSpec((None, d, tn_), lambda i, j: (j, 0, 0)))
        in_specs = [pl.BlockSpec((tm, d), lambda i, j: (i + i0, 0)), w_spec,
                    pl.BlockSpec((tm, half), lambda i, j: (i + i0, 0)),
                    pl.BlockSpec((tm, half), lambda i, j: (i + i0, 0))]
        args = [h, weights, *rope_tables]
        aliases = {}
        if prev is not None:
            in_specs.append(pl.BlockSpec(memory_space=pl.ANY))
            args.append(prev)
            aliases = {len(args) - 1: 0}
        out_specs = [pl.BlockSpec((tm, tn_), lambda i, j: (i + i0, j))]
        out_shape = [jax.ShapeDtypeStruct((n, n_cols), BF16)]
        w_bytes = _nbytes((d, tn_), BF16)
        if cast_w:
            out_specs.append(pl.BlockSpec((None, d, tn_), lambda i, j: (j // per_tile, 0, j % per_tile)))
            out_shape.append(jax.ShapeDtypeStruct((n_cols // tn, d, tn), BF16))
            w_bytes = _nbytes((d, tn_), F32) + 2 * _nbytes((d, tn_), BF16)
        casts = [_SideCast(*job, n_i, n_cols // tn_) for job in side_jobs]
        in_specs += [sc.in_spec for sc in casts]
        args += [sc.arg for sc in casts]
        out_specs += [sc.out_spec for sc in casts]
        out_shape += [sc.out_shape for sc in casts]
        w_bytes += sum(sc.block_bytes for sc in casts)
        blocks = _nbytes((tm, d), BF16) + w_bytes + 2 * _nbytes((tm, half), F32) + _nbytes((tm, tn_), BF16)
        return pl.pallas_call(
            functools.partial(_ret_proj_kernel, tn=tn_, d_qk=d_qk, head_dk=lay["dk"],
                              k_scale=float(lay["dk"]) ** -0.5, cast_w=cast_w,
                              has_prev=prev is not None, n_side=len(casts)),
            grid=(n_i, n_cols // tn_),
            in_specs=in_specs,
            out_specs=out_specs,
            out_shape=out_shape,
            input_output_aliases=aliases,
            compiler_params=pltpu.CompilerParams(
                dimension_semantics=("arbitrary", "arbitrary"),
                vmem_limit_bytes=_vmem_limit(blocks, _nbytes((tm, tn_), F32))),
            name="ret_in_proj" + ("_first" if cast_w else ""),
        )(*args)

    out0, w_bf = run(0, 1, tn_first, w3, True, None, [])
    out, *side_w = run(1, n // tm - 1, tn, w_bf, False, out0, sides)
    return out, side_w


def _ffn_in_kernel(x_ref, gam_ref, sc_ref, sh_ref, wg_ref, wu_ref, *rest, tm, sub, has_side):
    if has_side:
        side_in, o_ref, side_out, h_ref = rest
        side_out[...] = side_in[...].astype(BF16)
    else:
        o_ref, h_ref = rest

    @pl.when(pl.program_id(1) == 0)
    def _():
        _norm_mod_rows(x_ref, gam_ref, sc_ref, sh_ref, h_ref, tm=tm, sub=sub)

    h = h_ref[...]
    gate = jnp.dot(h, wg_ref[...], preferred_element_type=F32)
    up = jnp.dot(h, wu_ref[...], preferred_element_type=F32)
    o_ref[...] = (_silu(gate) * up).astype(o_ref.dtype)


def _ffn_in_call(x, gam, mod, layer, w, lay, side=None, tm=1024):
    n, d = x.shape
    sub = lay["sub"]
    tn = w.shape[2]
    n_j = w.shape[0] // 2
    d_ff = n_j * tn
    _, midx = _tiling(lay, tm)
    in_specs = [
        pl.BlockSpec((tm, d), lambda i, j: (i, 0)),
        pl.BlockSpec((1, d), lambda i, j: (0, 0)),
        pl.BlockSpec((None, sub, d), lambda i, j: (layer, midx(i), 4)),
        pl.BlockSpec((None, sub, d), lambda i, j: (layer, midx(i), 3)),
        pl.BlockSpec((None, d, tn), lambda i, j: (j, 0, 0)),
        pl.BlockSpec((None, d, tn), lambda i, j: (j + n_j, 0, 0)),
    ]
    args = [x, gam.reshape(1, d), mod, mod, w, w]
    out_specs = [pl.BlockSpec((tm, tn), lambda i, j: (i, j))]
    out_shape = [jax.ShapeDtypeStruct((n, d_ff), BF16)]
    blocks = (_nbytes((tm, d), F32) + 2 * _nbytes((sub, d), F32) + 2 * _nbytes((d, tn), BF16)
              + _nbytes((tm, tn), BF16))
    if side is not None:
        sc = _SideCast(*side, n // tm, n_j)
        in_specs.append(sc.in_spec)
        args.append(sc.arg)
        out_specs.append(sc.out_spec)
        out_shape.append(sc.out_shape)
        blocks += sc.block_bytes
    res = pl.pallas_call(
        functools.partial(_ffn_in_kernel, tm=tm, sub=sub, has_side=side is not None),
        grid=(n // tm, n_j),
        in_specs=in_specs,
        out_specs=out_specs,
        out_shape=out_shape,
        scratch_shapes=[pltpu.VMEM((tm, d), BF16)],
        compiler_params=pltpu.CompilerParams(
            dimension_semantics=("arbitrary", "arbitrary"),
            vmem_limit_bytes=_vmem_limit(blocks, _nbytes((tm, d), BF16) + 2 * _nbytes((tm, tn), F32))),
        name="ffn_in_swiglu",
    )(*args)
    return res[0], (res[1] if side is not None else None)


def _out_proj_kernel(*refs, tm, sub, npt, n_src, has_side, emit_ssq):
    a_refs, w_ref = refs[:n_src], refs[n_src]
    x_refs, gate_ref = refs[n_src + 1:2 * n_src + 1], refs[2 * n_src + 1]
    rest = list(refs[2 * n_src + 2:])
    ssq_ref = rest.pop() if emit_ssq else None
    if has_side:
        side_in, o_ref, side_out = rest
        side_out[...] = side_in[...].astype(BF16)
    else:
        (o_ref,) = rest

    def body(a_ref, x_ref):
        acc = jnp.dot(a_ref[...], w_ref[...], preferred_element_type=F32)
        gate = gate_ref[...]
        for s in range(tm // sub):
            rows = slice(s * sub, (s + 1) * sub)
            x_new = x_ref[rows, :] + gate * acc[rows, :]
            o_ref[rows, :] = x_new
            if emit_ssq:
                ssq_ref[rows, :] = jnp.sum(x_new * x_new, axis=-1, keepdims=True)

    if n_src == 1:
        body(a_refs[0], x_refs[0])
    else:
        is_prompt = pl.program_id(0) < npt
        pl.when(is_prompt)(functools.partial(body, a_refs[0], x_refs[0]))
        pl.when(jnp.logical_not(is_prompt))(functools.partial(body, a_refs[1], x_refs[1]))


def _out_proj_call(srcs, w, mod, layer, gate_col, lay, side=None, emit_ssq=False, tm=1024):
    n = sum(a.shape[0] for a, _ in srcs)
    k = srcs[0][0].shape[1]
    n_j, _, tn = w.shape
    d = n_j * tn
    sub = lay["sub"]
    npt, midx = _tiling(lay, tm)
    if len(srcs) == 1:
        a_specs = [pl.BlockSpec((tm, k), lambda i, j: (i, 0))]
        x_specs = [pl.BlockSpec((tm, tn), lambda i, j: (i, j))]
        a_bytes = 2 * _nbytes((tm, k), BF16)
    else:
        a_specs = [pl.BlockSpec((tm, k), lambda i, j: (jnp.minimum(i, npt - 1), 0)),
                   pl.BlockSpec((tm, k), lambda i, j: (jnp.maximum(i - npt, 0), 0), pipeline_mode=pl.Buffered(1))]
        x_specs = [pl.BlockSpec((tm, tn), lambda i, j: (jnp.minimum(i, npt - 1), j)),
                   pl.BlockSpec((tm, tn), lambda i, j: (jnp.maximum(i - npt, 0), j))]
        a_bytes = 3 * _nbytes((tm, k), BF16)
    in_specs = (a_specs + [pl.BlockSpec((None, k, tn), lambda i, j: (j, 0, 0))] + x_specs
                + [pl.BlockSpec((None, sub, tn), lambda i, j: (layer, midx(i), gate_col * n_j + j))])
    args = [a for a, _ in srcs] + [w] + [x for _, x in srcs] + [mod]
    out_specs = [pl.BlockSpec((tm, tn), lambda i, j: (i, j))]
    out_shape = [jax.ShapeDtypeStruct((n, d), F32)]
    blocks = (a_bytes // 2 + _nbytes((k, tn), BF16) + (1 + len(srcs)) * _nbytes((tm, tn), F32)
              + _nbytes((sub, tn), F32))
    if side is not None:
        sc = _SideCast(*side, n // tm, n_j)
        in_specs.append(sc.in_spec)
        args.append(sc.arg)
        out_specs.append(sc.out_spec)
        out_shape.append(sc.out_shape)
        blocks += sc.block_bytes
    if emit_ssq:
        out_specs.append(pl.BlockSpec((None, tm, 1), lambda i, j: (j, i, 0)))
        out_shape.append(jax.ShapeDtypeStruct((n_j, n, 1), F32))
        blocks += _nbytes((tm, V7X_LANES), F32)
    res = pl.pallas_call(
        functools.partial(_out_proj_kernel, tm=tm, sub=sub, npt=npt, n_src=len(srcs), has_side=side is not None,
                          emit_ssq=emit_ssq),
        grid=(n // tm, n_j),
        in_specs=in_specs,
        out_specs=out_specs,
        out_shape=out_shape,
        compiler_params=pltpu.CompilerParams(
            dimension_semantics=("arbitrary", "arbitrary"),
            vmem_limit_bytes=_vmem_limit(blocks, _nbytes((tm, tn), F32))),
        name="out_proj_residual",
    )(*args)
    return res[0], (res[1] if side is not None else None), (res[-1] if emit_ssq else None)


def _retention_decays(c):
    lg = jnp.log1p(-jnp.exp2(-5.0 - jnp.arange(RET_HEADS, dtype=F32)))
    idx = jnp.arange(c, dtype=F32)
    diff = idx[:, None] - idx[None, :]
    causal = diff >= 0
    dmask = jnp.where(causal[None], jnp.exp(jnp.where(causal, diff, 0.0)[None] * lg[:, None, None]), 0.0)
    qdec = jnp.exp((idx + 1.0)[None, :] * lg[:, None])
    kdec = jnp.exp((c - 1.0 - idx)[None, :] * lg[:, None])
    sdec = jnp.exp(c * lg)
    return dmask, qdec[:, :, None], kdec[:, :, None], sdec[:, None, None]


def _group_norm_gate(o, g, gng):
    mu = jnp.mean(o, axis=-1, keepdims=True)
    dlt = o - mu
    var = jnp.mean(dlt * dlt, axis=-1, keepdims=True)
    return _silu(g) * (dlt * lax.rsqrt(var + EPS) * gng)


_NT = (((1,), (1,)), ((), ()))
_TN = (((0,), (0,)), ((), ()))


def _retention_chunk(q, k, v, s_prev, dmask, qdec, kdec, sdec):
    scores = lax.dot_general(q, k, _NT, preferred_element_type=F32) * dmask
    o = (jnp.dot(scores.astype(BF16), v, preferred_element_type=F32)
         + qdec * jnp.dot(q, s_prev.astype(BF16), preferred_element_type=F32))
    kd = (k.astype(F32) * kdec).astype(BF16)
    s_new = sdec * s_prev + lax.dot_general(kd, v, _TN, preferred_element_type=F32)
    return o, s_new


def _retention_kernel(pq_ref, pk_ref, pv_ref, pg_ref, pgng_ref, pdmask_ref, pqdec_ref, pkdec_ref, psdec_ref,
                      sq_ref, sk_ref, sv_ref, sg_ref, sgng_ref, s0_ref, sdmask_ref, sqdec_ref, skdec_ref, ssdec_ref,
                      po_ref, ps_out_ref, so_ref, ss_out_ref,
                      ps_ref, qf_ref, kf_ref, vf_ref, of_ref, *, chunk, n_chunks, bb):
    @pl.when(pl.program_id(0) % 2 == 0)
    def _():
        ps_ref[...] = jnp.zeros_like(ps_ref)

    pdmask, pqdec, pkdec, psdec, pgng = pdmask_ref[0], pqdec_ref[0], pkdec_ref[0], psdec_ref[0], pgng_ref[...]

    def prompt_body(c, carry):
        rows = pl.ds(pl.multiple_of(c * chunk, chunk), chunk)
        o, s_new = _retention_chunk(pq_ref[rows, :], pk_ref[rows, :], pv_ref[rows, :], ps_ref[...],
                                    pdmask, pqdec, pkdec, psdec)
        ps_ref[...] = s_new
        po_ref[rows, :] = _group_norm_gate(o, pg_ref[rows, :].astype(F32), pgng).astype(po_ref.dtype)
        return carry

    lax.fori_loop(0, n_chunks, prompt_body, 0, unroll=2)
    ps_out_ref[0, 0] = ps_ref[...]

    smask, sqdec, skdec, ssdec, sgng = sdmask_ref[0], sqdec_ref[0], skdec_ref[0], ssdec_ref[0], sgng_ref[...]
    ts, _, dk = sq_ref.shape
    dv = sv_ref.shape[2]
    rows = ts * bb
    scores = lax.dot_general(sq_ref[...].reshape(rows, dk), sk_ref[...].reshape(rows, dk), _NT,
                             preferred_element_type=F32) * smask
    o_intra = jnp.dot(scores.astype(BF16), sv_ref[...].reshape(rows, dv), preferred_element_type=F32)
    qf_ref[...] = sq_ref[...].astype(F32)
    kf_ref[...] = sk_ref[...].astype(F32)
    vf_ref[...] = sv_ref[...].astype(F32)

    def sample_body(b, carry):
        s0 = s0_ref[b, 0]
        of_ref[:, b, :] = jnp.dot(qf_ref[:, b, :].astype(BF16), s0.astype(BF16), preferred_element_type=F32)
        kd = (kf_ref[:, b, :] * skdec).astype(BF16)
        ss_out_ref[b, 0] = ssdec * s0 + lax.dot_general(kd, vf_ref[:, b, :].astype(BF16), _TN,
                                                        preferred_element_type=F32)
        return carry

    lax.fori_loop(0, bb, sample_body, 0, unroll=4)
    o = o_intra + sqdec * of_ref[...].reshape(rows, dv)
    gated = _group_norm_gate(o, sg_ref[...].reshape(rows, dv).astype(F32), sgng)
    so_ref[...] = gated.astype(so_ref.dtype).reshape(ts, bb, dv)


def _retention_call(qkvg, gng, state, lay, chunk=2 * RET_CHUNK, bb=16):
    n = qkvg.shape[0]
    bp, tp, bs, ts, dk, dv = lay["n_prompt"], lay["t_prompt"], lay["sub"], lay["t_sample"], lay["dk"], lay["dv"]
    h = RET_HEADS
    n_p = lay["n_prompt_rows"]
    half = tp // 2
    n_steps = 2 * bp * h
    assert (bs // bb) * h == n_steps and half % chunk == 0
    t_blk = n_p // (bs * ts)
    assert t_blk * bs * ts == n_p
    v_blk0 = 2 * h * dk // dv
    g_blk0 = v_blk0 + h
    pdec = _retention_decays(chunk)
    dmask_s, qdec_s, kdec_s, sdec_s = _retention_decays(ts)
    same_seq = jnp.eye(bb, dtype=F32)[None, None, :, None, :]
    block_mask = (dmask_s[:, :, None, :, None] * same_seq).reshape(h, ts * bb, ts * bb)
    qdec_rows = jnp.broadcast_to(qdec_s[:, :, None, :], (h, ts, bb, 1)).reshape(h, ts * bb, 1)
    sdec = (block_mask, qdec_rows, kdec_s, sdec_s)
    qkv3 = qkvg.reshape(n // bs, bs, qkvg.shape[1])
    gng2 = gng.reshape(1, h * dv)

    p_row = lambda s: (s // 2 // h) * 2 + s % 2
    p_head = lambda s: (s // 2) % h
    s_blk = lambda s: s // h
    s_head = lambda s: s % h
    dec_specs = lambda c, head: [
        pl.BlockSpec((1, c, c), lambda s: (head(s), 0, 0)),
        pl.BlockSpec((1, c, 1), lambda s: (head(s), 0, 0)),
        pl.BlockSpec((1, c, 1), lambda s: (head(s), 0, 0)),
        pl.BlockSpec((1, 1, 1), lambda s: (head(s), 0, 0)),
    ]
    in_specs = [
        pl.BlockSpec((half, dk), lambda s: (p_row(s), p_head(s))),
        pl.BlockSpec((half, dk), lambda s: (p_row(s), h + p_head(s))),
        pl.BlockSpec((half, dv), lambda s: (p_row(s), v_blk0 + p_head(s))),
        pl.BlockSpec((half, dv), lambda s: (p_row(s), g_blk0 + p_head(s))),
        pl.BlockSpec((1, dv), lambda s: (0, p_head(s))),
        *dec_specs(chunk, p_head),
        pl.BlockSpec((ts, bb, dk), lambda s: (t_blk, s_blk(s), s_head(s))),
        pl.BlockSpec((ts, bb, dk), lambda s: (t_blk, s_blk(s), h + s_head(s))),
        pl.BlockSpec((ts, bb, dv), lambda s: (t_blk, s_blk(s), v_blk0 + s_head(s))),
        pl.BlockSpec((ts, bb, dv), lambda s: (t_blk, s_blk(s), g_blk0 + s_head(s))),
        pl.BlockSpec((1, dv), lambda s: (0, s_head(s))),
        pl.BlockSpec((bb, 1, dk, dv), lambda s: (s_blk(s), s_head(s), 0, 0)),
        pl.BlockSpec((1, ts * bb, ts * bb), lambda s: (s_head(s), 0, 0)),
        pl.BlockSpec((1, ts * bb, 1), lambda s: (s_head(s), 0, 0)),
        pl.BlockSpec((1, ts, 1), lambda s: (s_head(s), 0, 0)),
        pl.BlockSpec((1, 1, 1), lambda s: (s_head(s), 0, 0)),
    ]
    out_specs = [
        pl.BlockSpec((half, dv), lambda s: (p_row(s), p_head(s))),
        pl.BlockSpec((1, 1, dk, dv), lambda s: (s // 2 // h, p_head(s), 0, 0)),
        pl.BlockSpec((ts, bb, dv), lambda s: (0, s_blk(s), s_head(s))),
        pl.BlockSpec((bb, 1, dk, dv), lambda s: (s_blk(s), s_head(s), 0, 0)),
    ]
    out_shape = [jax.ShapeDtypeStruct((n_p, h * dv), BF16),
                 jax.ShapeDtypeStruct((bp, h, dk, dv), F32),
                 jax.ShapeDtypeStruct((ts, bs, h * dv), BF16),
                 jax.ShapeDtypeStruct((bs, h, dk, dv), F32)]
    blocks = (2 * _nbytes((half, dk), BF16) + 3 * _nbytes((half, dv), BF16) + _nbytes((dk, dv), F32)
              + _nbytes((chunk, chunk), F32)
              + 2 * _nbytes((ts, bb, dk), BF16) + 3 * _nbytes((ts, bb, dv), BF16) + 2 * _nbytes((bb, dk, dv), F32))
    scratch = _nbytes((dk, dv), F32) + 2 * _nbytes((ts, bb, dk), F32) + 2 * _nbytes((ts, bb, dv), F32)
    gated_p, s_p, gated_s, s_s = pl.pallas_call(
        functools.partial(_retention_kernel, chunk=chunk, n_chunks=half // chunk, bb=bb),
        grid=(n_steps,),
        in_specs=in_specs,
        out_specs=out_specs,
        out_shape=out_shape,
        scratch_shapes=[pltpu.VMEM((dk, dv), F32),
                        pltpu.VMEM((ts, bb, dk), F32), pltpu.VMEM((ts, bb, dk), F32),
                        pltpu.VMEM((ts, bb, dv), F32), pltpu.VMEM((ts, bb, dv), F32)],
        compiler_params=pltpu.CompilerParams(
            dimension_semantics=("arbitrary",),
            vmem_limit_bytes=_vmem_limit(blocks, scratch)),
        name="retention",
    )(qkvg, qkvg, qkvg, qkvg, gng2, *pdec, qkv3, qkv3, qkv3, qkv3, gng2, state, *sdec)
    return gated_p, gated_s.reshape(bs * ts, h * dv), s_p, s_s


def _window_sum_rows(win_ref, halo, r0, rows, win, stride):
    if stride % V7X_SUBLANES == 0:
        acc = win_ref[pl.ds(halo + r0, rows), :]
        for k in range(1, win):
            acc = acc + win_ref[pl.ds(halo - k * stride + r0, rows), :]
        return acc
    assert stride == 1 and halo == 16 and win <= 16
    blk = win_ref[pl.ds(r0, rows + 16), :]
    shift = 1
    while shift < win:
        blk = blk + pltpu.roll(blk, shift, 0)
        shift *= 2
    return blk[16:, :]


def _pool_kernel(ssq_ref, gam_ref, sc_ref, sh_ref, hist_ref, x_ref, gate_ref, w_ref, scale_ref, *rest,
                 tm, sub, stride, tiles_per_seq, hist_len, chunk_rows, d_model):
    o_ref, hist_out_ref, win_ref, m_ref = rest[-4:]
    g = pl.program_id(0)
    i = pl.program_id(1)
    halo = 16 * stride
    tile_in_seq = i % tiles_per_seq

    if tiles_per_seq > 1:
        @pl.when(tile_in_seq != 0)
        def _():
            win_ref[0:halo, :] = win_ref[tm:tm + halo, :]

    @pl.when(tile_in_seq == 0)
    def _():
        win_ref[0:halo, :] = hist_ref[0]

    gam = gam_ref[...]
    for s in range(tm // sub):
        rows = slice(s * sub, (s + 1) * sub)
        ssq = ssq_ref[0, rows, :]
        for c in range(1, ssq_ref.shape[0]):
            ssq = ssq + ssq_ref[c, rows, :]
        rs = lax.rsqrt(ssq / d_model + EPS)
        win_ref[halo + s * sub:halo + (s + 1) * sub, :] = (
            ((x_ref[rows, :] * rs) * gam) * (1.0 + sc_ref[...]) + sh_ref[...])

    for gi, win in enumerate(POOL_WINDOWS):
        @pl.when(g == gi)
        def _(win=win):
            def body(c, carry):
                r0 = pl.multiple_of(c * chunk_rows, chunk_rows)
                cur = win_ref[pl.ds(halo + r0, chunk_rows), :]
                acc = _window_sum_rows(win_ref, halo, r0, chunk_rows, win, stride)
                t = (tile_in_seq * tm + r0 + lax.broadcasted_iota(jnp.int32, (chunk_rows, 1), 0)) // stride
                cnt = jnp.minimum(win, t + 1 + hist_len).astype(F32)
                m_ref[pl.ds(r0, chunk_rows), :] = (acc / cnt - cur).astype(m_ref.dtype)
                return carry

            lax.fori_loop(0, tm // chunk_rows, body, 0)

    y = jnp.dot(m_ref[...], w_ref[0], preferred_element_type=F32) * scale_ref[...]
    gate = gate_ref[...]
    for s in range(tm // sub):
        rows = slice(s * sub, (s + 1) * sub)
        o_ref[rows, :] = x_ref[rows, :] + gate * y[rows, :]
    hist_out_ref[0] = win_ref[tm + stride:tm + halo, :]


def _pool_call(ssq, gam, hist, x, mod, layer, w, scale, x_prev, *, row0, n_rows, n_seq, stride, hist_len, mod_seq0,
               lay, tm=1024):
    n, d = x.shape
    sub = lay["sub"]
    n_g = len(POOL_WINDOWS)
    gc = d // n_g
    n_tiles = n_rows // tm
    tps = n_tiles // n_seq
    blk0 = row0 // tm
    halo = 16 * stride
    n_parts = ssq.shape[0]
    blocks = (2 * _nbytes((tm, gc), F32) + _nbytes((halo, gc), F32) + 3 * _nbytes((sub, gc), F32)
              + _nbytes((gc, gc), BF16) + _nbytes((15 * stride, gc), F32) + n_parts * _nbytes((tm, V7X_LANES), F32))
    scratch = _nbytes((halo + tm, gc), F32) + _nbytes((tm, gc), BF16)
    sh_col0, sc_col0, gate_col0 = 0, n_g, 2 * n_g
    mod_blk = lambda col0: pl.BlockSpec((None, sub, gc), lambda g, i: (layer, mod_seq0 + i // tps, col0 + g))
    in_specs = [
        pl.BlockSpec((n_parts, tm, 1), lambda g, i: (0, blk0 + i, 0)),
        pl.BlockSpec((1, gc), lambda g, i: (0, g)),
        mod_blk(sc_col0),
        mod_blk(sh_col0),
        pl.BlockSpec((1, halo, gc), lambda g, i: (i // tps, 0, g)),
        pl.BlockSpec((tm, gc), lambda g, i: (blk0 + i, g)),
        mod_blk(gate_col0),
        pl.BlockSpec((1, gc, gc), lambda g, i: (g, 0, 0)),
        pl.BlockSpec((1, gc), lambda g, i: (0, g)),
    ]
    args = [ssq, gam.reshape(1, d), mod, mod, hist, x, mod, w, scale.reshape(1, d)]
    aliases = {}
    if x_prev is not None:
        in_specs.append(pl.BlockSpec(memory_space=pl.ANY))
        args.append(x_prev)
        aliases = {len(args) - 1: 0}
    return pl.pallas_call(
        functools.partial(_pool_kernel, tm=tm, sub=sub, stride=stride, tiles_per_seq=tps,
                          hist_len=hist_len, chunk_rows=64, d_model=d),
        grid=(n_g, n_tiles),
        in_specs=in_specs,
        out_specs=[
            pl.BlockSpec((tm, gc), lambda g, i: (blk0 + i, g)),
            pl.BlockSpec((1, 15 * stride, gc), lambda g, i: (i // tps, 0, g)),
        ],
        out_shape=[jax.ShapeDtypeStruct((n, d), F32),
                   jax.ShapeDtypeStruct((n_seq, 15 * stride, d), F32)],
        scratch_shapes=[pltpu.VMEM((halo + tm, gc), F32), pltpu.VMEM((tm, gc), BF16)],
        input_output_aliases=aliases,
        compiler_params=pltpu.CompilerParams(
            dimension_semantics=("arbitrary", "arbitrary"),
            vmem_limit_bytes=_vmem_limit(blocks, scratch)),
        name="pool_mixer_s%d" % stride,
    )(*args)


def _rope_tables(pos, dk):
    half = dk // 2
    inv = 1.0 / (ROPE_BASE ** (jnp.arange(half, dtype=F32) / half))
    ang = pos[:, None] * inv[None, :]
    return jnp.cos(ang), jnp.sin(ang)


def kernel(x_prompt, x_sample, c_prompt, c_sample, state_ret, state_pool, norm_mix_g, norm_ffn_g, ada_w, ada_b,
           ret_w_in, ret_gn_g, ret_w_out, pool_w, pool_scale, ffn_w_in, ffn_w_out, final_norm_g):
    bp, tp, d = x_prompt.shape
    bs, ts, _ = x_sample.shape
    depth = ada_w.shape[0]
    dk = state_ret.shape[3]
    dv = state_ret.shape[4]
    n_hist = state_pool.shape[2]
    assert bp <= C_PAD and n_hist == max(POOL_WINDOWS) - 1 and depth == 2
    n_p = bp * tp
    n_s = bs * ts
    lay = dict(n_prompt=bp, t_prompt=tp, n_prompt_rows=n_p, n_sample_rows=n_s, sub=bs, t_sample=ts, dk=dk, dv=dv)

    xp = x_prompt.reshape(n_p, d)
    xs = x_sample.transpose(1, 0, 2).reshape(n_s, d)
    c_all = jnp.concatenate([c_prompt, jnp.zeros((C_PAD - bp, d), F32), c_sample], axis=0)
    mod = _ada_call(c_all, ada_w, ada_b, bp, bs)

    cos_p, sin_p = _rope_tables(jnp.arange(tp, dtype=F32), dk)
    cos_s, sin_s = _rope_tables(float(PAST_LEN) + jnp.arange(ts, dtype=F32), dk)
    cos = jnp.concatenate([jnp.tile(cos_p, (bp, 1)), jnp.repeat(cos_s, bs, axis=0)], axis=0)
    sin = jnp.concatenate([jnp.tile(sin_p, (bp, 1)), jnp.repeat(sin_s, bs, axis=0)], axis=0)

    h0 = _norm_call((xp, xs), norm_mix_g[0], mod, 0, 1, 0, lay, BF16)
    tn_ffn = 512
    d_ff = ffn_w_out.shape[1]
    cast_in = (d // 2, tn_ffn)
    cast_out = (d_ff // 8, tn_ffn)
    qkvg, (w_ret_out, w_ffn_in0) = _ret_proj_call(
        h0, ret_w_in, 0, lay, [(ret_w_out, 0, (ret_w_out.shape[1] // 8, tn_ffn)), (ffn_w_in, 0, cast_in)], (cos, sin))
    gated_p, gated_s, s_ret_p, s_ret_s = _retention_call(qkvg, ret_gn_g[0], state_ret[0], lay)
    x, w_ffn_out0, _ = _out_proj_call([(gated_p, xp), (gated_s, xs)], w_ret_out, mod, 0, 2, lay,
                                      side=(ffn_w_out, 0, cast_out))
    hmid, w_ffn_in1 = _ffn_in_call(x, norm_ffn_g[0], mod, 0, w_ffn_in0, lay, side=(ffn_w_in, 1, cast_in))
    x, w_ffn_out1, ssq = _out_proj_call([(hmid, x)], w_ffn_out0, mod, 0, 5, lay, side=(ffn_w_out, 1, cast_out),
                                        emit_ssq=True)

    pw = pool_w[0].astype(BF16)
    hist_p = jnp.zeros((bp, 16, d), F32)
    hist_s = jnp.concatenate([jnp.zeros((1, bs, d), F32), state_pool[0].transpose(1, 0, 2)], axis=0)
    x1, nh_p = _pool_call(ssq, norm_mix_g[1], hist_p, x, mod, 1, pw, pool_scale[0], None, row0=0, n_rows=n_p,
                          n_seq=bp, stride=1, hist_len=0, mod_seq0=0, lay=lay)
    x, nh_s = _pool_call(ssq, norm_mix_g[1], hist_s.reshape(1, 16 * bs, d), x, mod, 1, pw, pool_scale[0], x1,
                         row0=n_p, n_rows=n_s, n_seq=1, stride=bs, hist_len=min(PAST_LEN, n_hist), mod_seq0=bp,
                         lay=lay)
    hmid, _ = _ffn_in_call(x, norm_ffn_g[1], mod, 1, w_ffn_in1, lay)
    x, _, _ = _out_proj_call([(hmid, x)], w_ffn_out1, mod, 1, 5, lay)

    y_p, y_s = _norm_call((x,), final_norm_g, None, 0, 0, 0, lay, F32, split_out=True)
    y_prompt = y_p.reshape(bp, tp, d)
    y_sample = y_s.reshape(ts, bs, d).transpose(1, 0, 2)
    state_pool_sample = nh_s.reshape(n_hist, bs, d).transpose(1, 0, 2)
    return (y_prompt, y_sample, s_ret_p[None], nh_p[None], s_ret_s[None], state_pool_sample[None])
```

```python
import functools

import jax
import jax.numpy as jnp
from jax import lax
from jax.experimental import pallas as pl
from jax.experimental.pallas import tpu as pltpu

F32 = jnp.float32
BF16 = jnp.bfloat16

RET_HEADS = 8
RET_CHUNK = 128
ROPE_BASE = 10000.0
POOL_WINDOWS = (2, 4, 8, 16)
EPS = 1e-6
PAST_LEN = 16384

V7X_VMEM_BYTES = 64 * 1024 * 1024
V7X_LANES = 128
V7X_SUBLANES = 8
NORM_ROWS = 16
C_PAD = 8


def _vmem_limit(block_bytes, scratch_bytes=0):
    need = 2 * block_bytes + scratch_bytes + 12 * 1024 * 1024
    return int(min(need, V7X_VMEM_BYTES - 6 * 1024 * 1024))


def _nbytes(shape, dtype):
    n = 1
    for s in shape:
        n *= s
    return n * jnp.dtype(dtype).itemsize


def _silu(x):
    return x / (1.0 + jnp.exp(-x))


def _ada_kernel(c_ref, w_ref, b_ref, o_ref, *, n_prompt, sub):
    cs = _silu(c_ref[...])
    res = jnp.dot(cs.astype(BF16), w_ref[0].astype(BF16), preferred_element_type=F32) + b_ref[0]
    tn = res.shape[-1]
    for b in range(n_prompt):
        o_ref[0, b * sub:(b + 1) * sub, :] = jnp.broadcast_to(res[b:b + 1, :], (sub, tn))
    o_ref[0, n_prompt * sub:, :] = res[C_PAD:C_PAD + sub, :]


def _ada_call(c_all, ada_w, ada_b, n_prompt, sub, tn=2048):
    depth, d, n6 = ada_w.shape
    rows = c_all.shape[0]
    out_rows = (n_prompt + 1) * sub
    blocks = (_nbytes((rows, d), F32) + _nbytes((d, tn), F32) + _nbytes((1, tn), F32)
              + _nbytes((out_rows, tn), F32))
    return pl.pallas_call(
        functools.partial(_ada_kernel, n_prompt=n_prompt, sub=sub),
        grid=(depth, n6 // tn),
        in_specs=[
            pl.BlockSpec((rows, d), lambda l, j: (0, 0)),
            pl.BlockSpec((1, d, tn), lambda l, j: (l, 0, j)),
            pl.BlockSpec((1, 1, tn), lambda l, j: (l, 0, j)),
        ],
        out_specs=pl.BlockSpec((1, out_rows, tn), lambda l, j: (l, 0, j)),
        out_shape=jax.ShapeDtypeStruct((depth, out_rows, n6), F32),
        compiler_params=pltpu.CompilerParams(
            dimension_semantics=("arbitrary", "arbitrary"),
            vmem_limit_bytes=_vmem_limit(blocks, _nbytes((d, tn), BF16))),
        name="ada_mod",
    )(c_all, ada_w, ada_b.reshape(depth, 1, n6))


def _norm_mod_rows(x_ref, gam_ref, sc_ref, sh_ref, out_ref, *, tm, sub, row0=0, unroll=4):
    gam = gam_ref[...]
    per_sub = sub // NORM_ROWS

    def body(c, carry):
        r0 = pl.multiple_of(row0 + c * NORM_ROWS, NORM_ROWS)
        x = x_ref[pl.ds(r0, NORM_ROWS), :]
        hn = (x * lax.rsqrt(jnp.mean(x * x, axis=-1, keepdims=True) + EPS)) * gam
        if sc_ref is not None:
            m0 = pl.multiple_of((c % per_sub) * NORM_ROWS, NORM_ROWS)
            hn = hn * (1.0 + sc_ref[pl.ds(m0, NORM_ROWS), :]) + sh_ref[pl.ds(m0, NORM_ROWS), :]
        out_ref[pl.ds(r0, NORM_ROWS), :] = hn.astype(out_ref.dtype)
        return carry

    lax.fori_loop(0, tm // NORM_ROWS, body, 0, unroll=unroll)


def _mod_index(i, n_prompt_tiles, tiles_per_seq, n_prompt):
    return jnp.where(i < n_prompt_tiles, i // tiles_per_seq, n_prompt)


def _tiling(lay, tm):
    npt = lay["n_prompt_rows"] // tm
    midx = functools.partial(_mod_index, n_prompt_tiles=npt, tiles_per_seq=lay["t_prompt"] // tm,
                             n_prompt=lay["n_prompt"])
    return npt, midx


def _norm_kernel(*refs, tm, sub, n_in, n_out, has_mod, npt):
    x_refs = refs[:n_in]
    gam_ref = refs[n_in]
    sc_ref, sh_ref = (refs[n_in + 1], refs[n_in + 2]) if has_mod else (None, None)
    o_refs = refs[len(refs) - n_out:]
    if n_in == 1 and n_out == 1:
        _norm_mod_rows(x_refs[0], gam_ref, sc_ref, sh_ref, o_refs[0], tm=tm, sub=sub)
        return
    i = pl.program_id(0)

    @pl.when(i < npt)
    def _():
        _norm_mod_rows(x_refs[0], gam_ref, sc_ref, sh_ref, o_refs[0], tm=tm, sub=sub)

    @pl.when(i >= npt)
    def _():
        _norm_mod_rows(x_refs[-1], gam_ref, sc_ref, sh_ref, o_refs[-1], tm=tm, sub=sub)


def _norm_call(xs, gam, mod, layer, col_sc, col_sh, lay, out_dtype, split_out=False, tm=512):
    d = xs[0].shape[1]
    sub = lay["sub"]
    n_p, n_s = lay["n_prompt_rows"], lay["n_sample_rows"]
    npt, midx = _tiling(lay, tm)
    prompt_blk = lambda i: (jnp.minimum(i, npt - 1), 0)
    sample_blk = lambda i: (jnp.maximum(i - npt, 0), 0)
    whole_blk = lambda i: (i, 0)
    in_specs = ([pl.BlockSpec((tm, d), whole_blk)] if len(xs) == 1
                else [pl.BlockSpec((tm, d), prompt_blk), pl.BlockSpec((tm, d), sample_blk)])
    in_specs.append(pl.BlockSpec((1, d), lambda i: (0, 0)))
    args = list(xs) + [gam.reshape(1, d)]
    if mod is not None:
        in_specs += [pl.BlockSpec((None, sub, d), lambda i: (layer, midx(i), col_sc)),
                     pl.BlockSpec((None, sub, d), lambda i: (layer, midx(i), col_sh))]
        args += [mod, mod]
    if split_out:
        out_specs = [pl.BlockSpec((tm, d), prompt_blk), pl.BlockSpec((tm, d), sample_blk)]
        out_shape = [jax.ShapeDtypeStruct((n_p, d), out_dtype), jax.ShapeDtypeStruct((n_s, d), out_dtype)]
    else:
        out_specs = pl.BlockSpec((tm, d), whole_blk)
        out_shape = jax.ShapeDtypeStruct((n_p + n_s, d), out_dtype)
    n_out = 2 if split_out else 1
    blocks = (len(xs) + n_out) * _nbytes((tm, d), F32) + 2 * _nbytes((sub, d), F32)
    return pl.pallas_call(
        functools.partial(_norm_kernel, tm=tm, sub=sub, n_in=len(xs), n_out=n_out,
                          has_mod=mod is not None, npt=npt),
        grid=((n_p + n_s) // tm,),
        in_specs=in_specs,
        out_specs=out_specs,
        out_shape=out_shape,
        compiler_params=pltpu.CompilerParams(
            dimension_semantics=("arbitrary",), vmem_limit_bytes=_vmem_limit(blocks)),
        name="row_norm",
    )(*args)


def _load_weight(w_ref, copy_ref):
    w = w_ref[...]
    if copy_ref is not None:
        w = w.astype(BF16)
        copy_ref[...] = w
    return w


class _SideCast:
    def __init__(self, src3, layer, block, n_i, n_j):
        _, r, c = src3.shape
        br, tn = block
        ncb = c // tn
        n_blocks = (r // br) * ncb
        assert n_blocks <= n_i * n_j and r % br == 0 and c % tn == 0

        def blk(i, j):
            t = jnp.minimum(i * n_j + j, n_blocks - 1)
            return t // ncb, t % ncb

        self.arg = src3
        self.in_spec = pl.BlockSpec((None, br, tn), lambda i, j: (layer, *blk(i, j)))
        self.out_spec = pl.BlockSpec((None, br, tn), lambda i, j: (blk(i, j)[1], blk(i, j)[0], 0))
        self.out_shape = jax.ShapeDtypeStruct((ncb, r, tn), BF16)
        self.block_bytes = _nbytes(block, F32) + _nbytes(block, BF16)


def _ret_proj_kernel(h_ref, w_ref, cos_ref, sin_ref, *rest, tn, d_qk, head_dk, k_scale, cast_w, has_prev, n_side):
    n_in = (1 if has_prev else 0) + n_side
    o_ref = rest[n_in]
    for s in range(n_side):
        rest[len(rest) - n_side + s][...] = rest[n_in - n_side + s][...].astype(BF16)
    w = _load_weight(w_ref, rest[n_in + 1] if cast_w else None)
    acc = jnp.dot(h_ref[...], w, preferred_element_type=F32)
    col0 = pl.program_id(1) * tn
    is_qk = col0 < 2 * d_qk
    scale = jnp.where(jnp.logical_and(is_qk, col0 >= d_qk), k_scale, 1.0).astype(F32)
    cs = jnp.where(is_qk, cos_ref[...], 1.0) * scale
    sn = jnp.where(is_qk, sin_ref[...], 0.0) * scale
    half = head_dk // 2
    for hh in range(tn // head_dk):
        x1 = acc[:, hh * head_dk:hh * head_dk + half]
        x2 = acc[:, hh * head_dk + half:(hh + 1) * head_dk]
        o_ref[:, hh * head_dk:hh * head_dk + half] = (x1 * cs - x2 * sn).astype(o_ref.dtype)
        o_ref[:, hh * head_dk + half:(hh + 1) * head_dk] = (x1 * sn + x2 * cs).astype(o_ref.dtype)


def _ret_proj_call(h, w3, w_layer, lay, sides, rope_tables, tm=1024, tn=1024, tn_first=512):
    n, d = h.shape
    n_cols = w3.shape[2]
    d_qk = RET_HEADS * lay["dk"]
    half = lay["dk"] // 2

    per_tile = tn // tn_first

    def run(i0, n_i, tn_, weights, cast_w, prev, side_jobs):
        w_spec = (pl.BlockSpec((None, d, tn_), lambda i, j: (w_layer, 0, j)) if cast_w
                  else pl.BlockSpec((None, d, tn_), lambda i, j: (j, 0, 0)))
        in_specs = [pl.BlockSpec((tm, d), lambda i, j: (i + i0, 0)), w_spec,
                    pl.BlockSpec((tm, half), lambda i, j: (i + i0, 0)),
                    pl.BlockSpec((tm, half), lambda i, j: (i + i0, 0))]
        args = [h, weights, *rope_tables]
        aliases = {}
        if prev is not None:
            in_specs.append(pl.BlockSpec(memory_space=pl.ANY))
            args.append(prev)
            aliases = {len(args) - 1: 0}
        out_specs = [pl.BlockSpec((tm, tn_), lambda i, j: (i + i0, j))]
        out_shape = [jax.ShapeDtypeStruct((n, n_cols), BF16)]
        w_bytes = _nbytes((d, tn_), BF16)
        if cast_w:
            out_specs.append(pl.BlockSpec((None, d, tn_), lambda i, j: (j // per_tile, 0, j % per_tile)))
            out_shape.append(jax.ShapeDtypeStruct((n_cols // tn, d, tn), BF16))
            w_bytes = _nbytes((d, tn_), F32) + 2 * _nbytes((d, tn_), BF16)
        casts = [_SideCast(*job, n_i, n_cols // tn_) for job in side_jobs]
        in_specs += [sc.in_spec for sc in casts]
        args += [sc.arg for sc in casts]
        out_specs += [sc.out_spec for sc in casts]
        out_shape += [sc.out_shape for sc in casts]
        w_bytes += sum(sc.block_bytes for sc in casts)
        blocks = _nbytes((tm, d), BF16) + w_bytes + 2 * _nbytes((tm, half), F32) + _nbytes((tm, tn_), BF16)
        return pl.pallas_call(
            functools.partial(_ret_proj_kernel, tn=tn_, d_qk=d_qk, head_dk=lay["dk"],
                              k_scale=float(lay["dk"]) ** -0.5, cast_w=cast_w,
                              has_prev=prev is not None, n_side=len(casts)),
            grid=(n_i, n_cols // tn_),
            in_specs=in_specs,
            out_specs=out_specs,
            out_shape=out_shape,
            input_output_aliases=aliases,
            compiler_params=pltpu.CompilerParams(
                dimension_semantics=("arbitrary", "arbitrary"),
                vmem_limit_bytes=_vmem_limit(blocks, _nbytes((tm, tn_), F32))),
            name="ret_in_proj" + ("_first" if cast_w else ""),
        )(*args)

    out0, w_bf = run(0, 1, tn_first, w3, True, None, [])
    out, *side_w = run(1, n // tm - 1, tn, w_bf, False, out0, sides)
    return out, side_w


def _ffn_in_kernel(x_ref, gam_ref, sc_ref, sh_ref, wg_ref, wu_ref, *rest, tm, sub, has_side):
    if has_side:
        side_in, o_ref, side_out, h_ref = rest
        side_out[...] = side_in[...].astype(BF16)
    else:
        o_ref, h_ref = rest

    @pl.when(pl.program_id(1) == 0)
    def _():
        _norm_mod_rows(x_ref, gam_ref, sc_ref, sh_ref, h_ref, tm=tm, sub=sub)

    h = h_ref[...]
    gate = jnp.dot(h, wg_ref[...], preferred_element_type=F32)
    up = jnp.dot(h, wu_ref[...], preferred_element_type=F32)
    o_ref[...] = (_silu(gate) * up).astype(o_ref.dtype)


def _ffn_in_call(x, gam, mod, layer, w, lay, side=None, tm=1024):
    n, d = x.shape
    sub = lay["sub"]
    tn = w.shape[2]
    n_j = w.shape[0] // 2
    d_ff = n_j * tn
    _, midx = _tiling(lay, tm)
    in_specs = [
        pl.BlockSpec((tm, d), lambda i, j: (i, 0)),
        pl.BlockSpec((1, d), lambda i, j: (0, 0)),
        pl.BlockSpec((None, sub, d), lambda i, j: (layer, midx(i), 4)),
        pl.BlockSpec((None, sub, d), lambda i, j: (layer, midx(i), 3)),
        pl.BlockSpec((None, d, tn), lambda i, j: (j, 0, 0)),
        pl.BlockSpec((None, d, tn), lambda i, j: (j + n_j, 0, 0)),
    ]
    args = [x, gam.reshape(1, d), mod, mod, w, w]
    out_specs = [pl.BlockSpec((tm, tn), lambda i, j: (i, j))]
    out_shape = [jax.ShapeDtypeStruct((n, d_ff), BF16)]
    blocks = (_nbytes((tm, d), F32) + 2 * _nbytes((sub, d), F32) + 2 * _nbytes((d, tn), BF16)
              + _nbytes((tm, tn), BF16))
    if side is not None:
        sc = _SideCast(*side, n // tm, n_j)
        in_specs.append(sc.in_spec)
        args.append(sc.arg)
        out_specs.append(sc.out_spec)
        out_shape.append(sc.out_shape)
        blocks += sc.block_bytes
    res = pl.pallas_call(
        functools.partial(_ffn_in_kernel, tm=tm, sub=sub, has_side=side is not None),
        grid=(n // tm, n_j),
        in_specs=in_specs,
        out_specs=out_specs,
        out_shape=out_shape,
        scratch_shapes=[pltpu.VMEM((tm, d), BF16)],
        compiler_params=pltpu.CompilerParams(
            dimension_semantics=("arbitrary", "arbitrary"),
            vmem_limit_bytes=_vmem_limit(blocks, _nbytes((tm, d), BF16) + 2 * _nbytes((tm, tn), F32))),
        name="ffn_in_swiglu",
    )(*args)
    return res[0], (res[1] if side is not None else None)


def _out_proj_kernel(*refs, tm, sub, npt, n_src, has_side, emit_ssq):
    a_refs, w_ref = refs[:n_src], refs[n_src]
    x_refs, gate_ref = refs[n_src + 1:2 * n_src + 1], refs[2 * n_src + 1]
    rest = list(refs[2 * n_src + 2:])
    ssq_ref = rest.pop() if emit_ssq else None
    if has_side:
        side_in, o_ref, side_out = rest
        side_out[...] = side_in[...].astype(BF16)
    else:
        (o_ref,) = rest

    def body(a_ref, x_ref):
        acc = jnp.dot(a_ref[...], w_ref[...], preferred_element_type=F32)
        gate = gate_ref[...]
        for s in range(tm // sub):
            rows = slice(s * sub, (s + 1) * sub)
            x_new = x_ref[rows, :] + gate * acc[rows, :]
            o_ref[rows, :] = x_new
            if emit_ssq:
                ssq_ref[rows, :] = jnp.sum(x_new * x_new, axis=-1, keepdims=True)

    if n_src == 1:
        body(a_refs[0], x_refs[0])
    else:
        is_prompt = pl.program_id(0) < npt
        pl.when(is_prompt)(functools.partial(body, a_refs[0], x_refs[0]))
        pl.when(jnp.logical_not(is_prompt))(functools.partial(body, a_refs[1], x_refs[1]))


def _out_proj_call(srcs, w, mod, layer, gate_col, lay, side=None, emit_ssq=False, tm=1024):
    n = sum(a.shape[0] for a, _ in srcs)
    k = srcs[0][0].shape[1]
    n_j, _, tn = w.shape
    d = n_j * tn
    sub = lay["sub"]
    npt, midx = _tiling(lay, tm)
    if len(srcs) == 1:
        a_specs = [pl.BlockSpec((tm, k), lambda i, j: (i, 0))]
        x_specs = [pl.BlockSpec((tm, tn), lambda i, j: (i, j))]
        a_bytes = 2 * _nbytes((tm, k), BF16)
    else:
        a_specs = [pl.BlockSpec((tm, k), lambda i, j: (jnp.minimum(i, npt - 1), 0)),
                   pl.BlockSpec((tm, k), lambda i, j: (jnp.maximum(i - npt, 0), 0), pipeline_mode=pl.Buffered(1))]
        x_specs = [pl.BlockSpec((tm, tn), lambda i, j: (jnp.minimum(i, npt - 1), j)),
                   pl.BlockSpec((tm, tn), lambda i, j: (jnp.maximum(i - npt, 0), j))]
        a_bytes = 3 * _nbytes((tm, k), BF16)
    in_specs = (a_specs + [pl.BlockSpec((None, k, tn), lambda i, j: (j, 0, 0))] + x_specs
                + [pl.BlockSpec((None, sub, tn), lambda i, j: (layer, midx(i), gate_col * n_j + j))])
    args = [a for a, _ in srcs] + [w] + [x for _, x in srcs] + [mod]
    out_specs = [pl.BlockSpec((tm, tn), lambda i, j: (i, j))]
    out_shape = [jax.ShapeDtypeStruct((n, d), F32)]
    blocks = (a_bytes // 2 + _nbytes((k, tn), BF16) + (1 + len(srcs)) * _nbytes((tm, tn), F32)
              + _nbytes((sub, tn), F32))
    if side is not None:
        sc = _SideCast(*side, n // tm, n_j)
        in_specs.append(sc.in_spec)
        args.append(sc.arg)
        out_specs.append(sc.out_spec)
        out_shape.append(sc.out_shape)
        blocks += sc.block_bytes
    if emit_ssq:
        out_specs.append(pl.BlockSpec((None, tm, 1), lambda i, j: (j, i, 0)))
        out_shape.append(jax.ShapeDtypeStruct((n_j, n, 1), F32))
        blocks += _nbytes((tm, V7X_LANES), F32)
    res = pl.pallas_call(
        functools.partial(_out_proj_kernel, tm=tm, sub=sub, npt=npt, n_src=len(srcs), has_side=side is not None,
                          emit_ssq=emit_ssq),
        grid=(n // tm, n_j),
        in_specs=in_specs,
        out_specs=out_specs,
        out_shape=out_shape,
        compiler_params=pltpu.CompilerParams(
            dimension_semantics=("arbitrary", "arbitrary"),
            vmem_limit_bytes=_vmem_limit(blocks, _nbytes((tm, tn), F32))),
        name="out_proj_residual",
    )(*args)
    return res[0], (res[1] if side is not None else None), (res[-1] if emit_ssq else None)


def _retention_decays(c):
    lg = jnp.log1p(-jnp.exp2(-5.0 - jnp.arange(RET_HEADS, dtype=F32)))
    idx = jnp.arange(c, dtype=F32)
    diff = idx[:, None] - idx[None, :]
    causal = diff >= 0
    dmask = jnp.where(causal[None], jnp.exp(jnp.where(causal, diff, 0.0)[None] * lg[:, None, None]), 0.0)
    qdec = jnp.exp((idx + 1.0)[None, :] * lg[:, None])
    kdec = jnp.exp((c - 1.0 - idx)[None, :] * lg[:, None])
    sdec = jnp.exp(c * lg)
    return dmask, qdec[:, :, None], kdec[:, :, None], sdec[:, None, None]


def _group_norm_gate(o, g, gng):
    mu = jnp.mean(o, axis=-1, keepdims=True)
    dlt = o - mu
    var = jnp.mean(dlt * dlt, axis=-1, keepdims=True)
    return _silu(g) * (dlt * lax.rsqrt(var + EPS) * gng)


_NT = (((1,), (1,)), ((), ()))
_TN = (((0,), (0,)), ((), ()))


def _retention_chunk(q, k, v, s_prev, dmask, qdec, kdec, sdec):
    scores = lax.dot_general(q, k, _NT, preferred_element_type=F32) * dmask
    o = (jnp.dot(scores.astype(BF16), v, preferred_element_type=F32)
         + qdec * jnp.dot(q, s_prev.astype(BF16), preferred_element_type=F32))
    kd = (k.astype(F32) * kdec).astype(BF16)
    s_new = sdec * s_prev + lax.dot_general(kd, v, _TN, preferred_element_type=F32)
    return o, s_new


def _retention_kernel(pq_ref, pk_ref, pv_ref, pg_ref, pgng_ref, pdmask_ref, pqdec_ref, pkdec_ref, psdec_ref,
                      sq_ref, sk_ref, sv_ref, sg_ref, sgng_ref, s0_ref, sdmask_ref, sqdec_ref, skdec_ref, ssdec_ref,
                      po_ref, ps_out_ref, so_ref, ss_out_ref,
                      ps_ref, qf_ref, kf_ref, vf_ref, of_ref, *, chunk, n_chunks, bb):
    @pl.when(pl.program_id(0) % 2 == 0)
    def _():
        ps_ref[...] = jnp.zeros_like(ps_ref)

    pdmask, pqdec, pkdec, psdec, pgng = pdmask_ref[0], pqdec_ref[0], pkdec_ref[0], psdec_ref[0], pgng_ref[...]

    def prompt_body(c, carry):
        rows = pl.ds(pl.multiple_of(c * chunk, chunk), chunk)
        o, s_new = _retention_chunk(pq_ref[rows, :], pk_ref[rows, :], pv_ref[rows, :], ps_ref[...],
                                    pdmask, pqdec, pkdec, psdec)
        ps_ref[...] = s_new
        po_ref[rows, :] = _group_norm_gate(o, pg_ref[rows, :].astype(F32), pgng).astype(po_ref.dtype)
        return carry

    lax.fori_loop(0, n_chunks, prompt_body, 0, unroll=2)
    ps_out_ref[0, 0] = ps_ref[...]

    smask, sqdec, skdec, ssdec, sgng = sdmask_ref[0], sqdec_ref[0], skdec_ref[0], ssdec_ref[0], sgng_ref[...]
    ts, _, dk = sq_ref.shape
    dv = sv_ref.shape[2]
    rows = ts * bb
    scores = lax.dot_general(sq_ref[...].reshape(rows, dk), sk_ref[...].reshape(rows, dk), _NT,
                             preferred_element_type=F32) * smask
    o_intra = jnp.dot(scores.astype(BF16), sv_ref[...].reshape(rows, dv), preferred_element_type=F32)
    qf_ref[...] = sq_ref[...].astype(F32)
    kf_ref[...] = sk_ref[...].astype(F32)
    vf_ref[...] = sv_ref[...].astype(F32)

    def sample_body(b, carry):
        s0 = s0_ref[b, 0]
        of_ref[:, b, :] = jnp.dot(qf_ref[:, b, :].astype(BF16), s0.astype(BF16), preferred_element_type=F32)
        kd = (kf_ref[:, b, :] * skdec).astype(BF16)
        ss_out_ref[b, 0] = ssdec * s0 + lax.dot_general(kd, vf_ref[:, b, :].astype(BF16), _TN,
                                                        preferred_element_type=F32)
        return carry

    lax.fori_loop(0, bb, sample_body, 0, unroll=4)
    o = o_intra + sqdec * of_ref[...].reshape(rows, dv)
    gated = _group_norm_gate(o, sg_ref[...].reshape(rows, dv).astype(F32), sgng)
    so_ref[...] = gated.astype(so_ref.dtype).reshape(ts, bb, dv)


def _retention_call(qkvg, gng, state, lay, chunk=2 * RET_CHUNK, bb=16):
    n = qkvg.shape[0]
    bp, tp, bs, ts, dk, dv = lay["n_prompt"], lay["t_prompt"], lay["sub"], lay["t_sample"], lay["dk"], lay["dv"]
    h = RET_HEADS
    n_p = lay["n_prompt_rows"]
    half = tp // 2
    n_steps = 2 * bp * h
    assert (bs // bb) * h == n_steps and half % chunk == 0
    t_blk = n_p // (bs * ts)
    assert t_blk * bs * ts == n_p
    v_blk0 = 2 * h * dk // dv
    g_blk0 = v_blk0 + h
    pdec = _retention_decays(chunk)
    dmask_s, qdec_s, kdec_s, sdec_s = _retention_decays(ts)
    same_seq = jnp.eye(bb, dtype=F32)[None, None, :, None, :]
    block_mask = (dmask_s[:, :, None, :, None] * same_seq).reshape(h, ts * bb, ts * bb)
    qdec_rows = jnp.broadcast_to(qdec_s[:, :, None, :], (h, ts, bb, 1)).reshape(h, ts * bb, 1)
    sdec = (block_mask, qdec_rows, kdec_s, sdec_s)
    qkv3 = qkvg.reshape(n // bs, bs, qkvg.shape[1])
    gng2 = gng.reshape(1, h * dv)

    p_row = lambda s: (s // 2 // h) * 2 + s % 2
    p_head = lambda s: (s // 2) % h
    s_blk = lambda s: s // h
    s_head = lambda s: s % h
    dec_specs = lambda c, head: [
        pl.BlockSpec((1, c, c), lambda s: (head(s), 0, 0)),
        pl.BlockSpec((1, c, 1), lambda s: (head(s), 0, 0)),
        pl.BlockSpec((1, c, 1), lambda s: (head(s), 0, 0)),
        pl.BlockSpec((1, 1, 1), lambda s: (head(s), 0, 0)),
    ]
    in_specs = [
        pl.BlockSpec((half, dk), lambda s: (p_row(s), p_head(s))),
        pl.BlockSpec((half, dk), lambda s: (p_row(s), h + p_head(s))),
        pl.BlockSpec((half, dv), lambda s: (p_row(s), v_blk0 + p_head(s))),
        pl.BlockSpec((half, dv), lambda s: (p_row(s), g_blk0 + p_head(s))),
        pl.BlockSpec((1, dv), lambda s: (0, p_head(s))),
        *dec_specs(chunk, p_head),
        pl.BlockSpec((ts, bb, dk), lambda s: (t_blk, s_blk(s), s_head(s))),
        pl.BlockSpec((ts, bb, dk), lambda s: (t_blk, s_blk(s), h + s_head(s))),
        pl.BlockSpec((ts, bb, dv), lambda s: (t_blk, s_blk(s), v_blk0 + s_head(s))),
        pl.BlockSpec((ts, bb, dv), lambda s: (t_blk, s_blk(s), g_blk0 + s_head(s))),
        pl.BlockSpec((1, dv), lambda s: (0, s_head(s))),
        pl.BlockSpec((bb, 1, dk, dv), lambda s: (s_blk(s), s_head(s), 0, 0)),
        pl.BlockSpec((1, ts * bb, ts * bb), lambda s: (s_head(s), 0, 0)),
        pl.BlockSpec((1, ts * bb, 1), lambda s: (s_head(s), 0, 0)),
        pl.BlockSpec((1, ts, 1), lambda s: (s_head(s), 0, 0)),
        pl.BlockSpec((1, 1, 1), lambda s: (s_head(s), 0, 0)),
    ]
    out_specs = [
        pl.BlockSpec((half, dv), lambda s: (p_row(s), p_head(s))),
        pl.BlockSpec((1, 1, dk, dv), lambda s: (s // 2 // h, p_head(s), 0, 0)),
        pl.BlockSpec((ts, bb, dv), lambda s: (0, s_blk(s), s_head(s))),
        pl.BlockSpec((bb, 1, dk, dv), lambda s: (s_blk(s), s_head(s), 0, 0)),
    ]
    out_shape = [jax.ShapeDtypeStruct((n_p, h * dv), BF16),
                 jax.ShapeDtypeStruct((bp, h, dk, dv), F32),
                 jax.ShapeDtypeStruct((ts, bs, h * dv), BF16),
                 jax.ShapeDtypeStruct((bs, h, dk, dv), F32)]
    blocks = (2 * _nbytes((half, dk), BF16) + 3 * _nbytes((half, dv), BF16) + _nbytes((dk, dv), F32)
              + _nbytes((chunk, chunk), F32)
              + 2 * _nbytes((ts, bb, dk), BF16) + 3 * _nbytes((ts, bb, dv), BF16) + 2 * _nbytes((bb, dk, dv), F32))
    scratch = _nbytes((dk, dv), F32) + 2 * _nbytes((ts, bb, dk), F32) + 2 * _nbytes((ts, bb, dv), F32)
    gated_p, s_p, gated_s, s_s = pl.pallas_call(
        functools.partial(_retention_kernel, chunk=chunk, n_chunks=half // chunk, bb=bb),
        grid=(n_steps,),
        in_specs=in_specs,
        out_specs=out_specs,
        out_shape=out_shape,
        scratch_shapes=[pltpu.VMEM((dk, dv), F32),
                        pltpu.VMEM((ts, bb, dk), F32), pltpu.VMEM((ts, bb, dk), F32),
                        pltpu.VMEM((ts, bb, dv), F32), pltpu.VMEM((ts, bb, dv), F32)],
        compiler_params=pltpu.CompilerParams(
            dimension_semantics=("arbitrary",),
            vmem_limit_bytes=_vmem_limit(blocks, scratch)),
        name="retention",
    )(qkvg, qkvg, qkvg, qkvg, gng2, *pdec, qkv3, qkv3, qkv3, qkv3, gng2, state, *sdec)
    return gated_p, gated_s.reshape(bs * ts, h * dv), s_p, s_s


def _window_sum_rows(win_ref, halo, r0, rows, win, stride):
    if stride % V7X_SUBLANES == 0:
        acc = win_ref[pl.ds(halo + r0, rows), :]
        for k in range(1, win):
            acc = acc + win_ref[pl.ds(halo - k * stride + r0, rows), :]
        return acc
    assert stride == 1 and halo == 16 and win <= 16
    blk = win_ref[pl.ds(r0, rows + 16), :]
    shift = 1
    while shift < win:
        blk = blk + pltpu.roll(blk, shift, 0)
        shift *= 2
    return blk[16:, :]


def _pool_kernel(ssq_ref, gam_ref, sc_ref, sh_ref, hist_ref, x_ref, gate_ref, w_ref, scale_ref, *rest,
                 tm, sub, stride, tiles_per_seq, hist_len, chunk_rows, d_model, n_hist):
    o_ref, hist_out_ref, win_ref, m_ref = rest[-4:]
    g = pl.program_id(0)
    i = pl.program_id(1)
    halo = 16 * stride
    tile_in_seq = i % tiles_per_seq

    if tiles_per_seq > 1:
        @pl.when(tile_in_seq != 0)
        def _():
            win_ref[0:halo, :] = win_ref[tm:tm + halo, :]

    @pl.when(tile_in_seq == 0)
    def _():
        win_ref[0:stride, :] = jnp.zeros((stride, win_ref.shape[1]), F32)
        if stride == 1:
            win_ref[1:halo, :] = hist_ref[0]
        else:
            for p in range(n_hist):
                win_ref[(1 + p) * stride:(2 + p) * stride, :] = hist_ref[:, p, :]

    gam = gam_ref[...]
    for s in range(tm // sub):
        rows = slice(s * sub, (s + 1) * sub)
        ssq = ssq_ref[0, rows, :]
        for c in range(1, ssq_ref.shape[0]):
            ssq = ssq + ssq_ref[c, rows, :]
        rs = lax.rsqrt(ssq / d_model + EPS)
        win_ref[halo + s * sub:halo + (s + 1) * sub, :] = (
            ((x_ref[rows, :] * rs) * gam) * (1.0 + sc_ref[...]) + sh_ref[...])

    for gi, win in enumerate(POOL_WINDOWS):
        @pl.when(g == gi)
        def _(win=win):
            def body(c, carry):
                r0 = pl.multiple_of(c * chunk_rows, chunk_rows)
                cur = win_ref[pl.ds(halo + r0, chunk_rows), :]
                acc = _window_sum_rows(win_ref, halo, r0, chunk_rows, win, stride)
                t = (tile_in_seq * tm + r0 + lax.broadcasted_iota(jnp.int32, (chunk_rows, 1), 0)) // stride
                cnt = jnp.minimum(win, t + 1 + hist_len).astype(F32)
                m_ref[pl.ds(r0, chunk_rows), :] = (acc / cnt - cur).astype(m_ref.dtype)
                return carry

            lax.fori_loop(0, tm // chunk_rows, body, 0)

    y = jnp.dot(m_ref[...], w_ref[0], preferred_element_type=F32) * scale_ref[...]
    gate = gate_ref[...]
    for s in range(tm // sub):
        rows = slice(s * sub, (s + 1) * sub)
        o_ref[rows, :] = x_ref[rows, :] + gate * y[rows, :]
    if stride == 1:
        hist_out_ref[0] = win_ref[tm + 1:tm + halo, :]
    else:
        for p in range(n_hist):
            hist_out_ref[:, p, :] = win_ref[tm + (1 + p) * stride:tm + (2 + p) * stride, :]


def _pool_call(ssq, gam, hist, x, mod, layer, w, scale, x_prev, *, row0, n_rows, n_seq, stride, hist_len, mod_seq0,
               lay, tm=1024):
    n, d = x.shape
    sub = lay["sub"]
    n_g = len(POOL_WINDOWS)
    gc = d // n_g
    n_tiles = n_rows // tm
    tps = n_tiles // n_seq
    blk0 = row0 // tm
    halo = 16 * stride
    n_parts = ssq.shape[0]
    n_hist = hist.shape[1]
    assert n_hist == 15 and (stride == 1 or (n_seq == 1 and hist.shape[0] == stride))
    hist_block = (1, n_hist, gc) if stride == 1 else (stride, n_hist, gc)
    hist_blk = lambda g, i: (i // tps, 0, g)
    blocks = (2 * _nbytes((tm, gc), F32) + 2 * _nbytes((halo, gc), F32) + 3 * _nbytes((sub, gc), F32)
              + _nbytes((gc, gc), BF16) + n_parts * _nbytes((tm, V7X_LANES), F32))
    scratch = _nbytes((halo + tm, gc), F32) + _nbytes((tm, gc), BF16)
    sh_col0, sc_col0, gate_col0 = 0, n_g, 2 * n_g
    mod_blk = lambda col0: pl.BlockSpec((None, sub, gc), lambda g, i: (layer, mod_seq0 + i // tps, col0 + g))
    in_specs = [
        pl.BlockSpec((n_parts, tm, 1), lambda g, i: (0, blk0 + i, 0)),
        pl.BlockSpec((1, gc), lambda g, i: (0, g)),
        mod_blk(sc_col0),
        mod_blk(sh_col0),
        pl.BlockSpec(hist_block, hist_blk),
        pl.BlockSpec((tm, gc), lambda g, i: (blk0 + i, g)),
        mod_blk(gate_col0),
        pl.BlockSpec((1, gc, gc), lambda g, i: (g, 0, 0)),
        pl.BlockSpec((1, gc), lambda g, i: (0, g)),
    ]
    args = [ssq, gam.reshape(1, d), mod, mod, hist, x, mod, w, scale.reshape(1, d)]
    aliases = {}
    if x_prev is not None:
        in_specs.append(pl.BlockSpec(memory_space=pl.ANY))
        args.append(x_prev)
        aliases = {len(args) - 1: 0}
    return pl.pallas_call(
        functools.partial(_pool_kernel, tm=tm, sub=sub, stride=stride, tiles_per_seq=tps,
                          hist_len=hist_len, chunk_rows=64, d_model=d, n_hist=n_hist),
        grid=(n_g, n_tiles),
        in_specs=in_specs,
        out_specs=[
            pl.BlockSpec((tm, gc), lambda g, i: (blk0 + i, g)),
            pl.BlockSpec(hist_block, hist_blk),
        ],
        out_shape=[jax.ShapeDtypeStruct((n, d), F32),
                   jax.ShapeDtypeStruct(hist.shape, F32)],
        scratch_shapes=[pltpu.VMEM((halo + tm, gc), F32), pltpu.VMEM((tm, gc), BF16)],
        input_output_aliases=aliases,
        compiler_params=pltpu.CompilerParams(
            dimension_semantics=("arbitrary", "arbitrary"),
            vmem_limit_bytes=_vmem_limit(blocks, scratch)),
        name="pool_mixer_s%d" % stride,
    )(*args)


def _rope_tables(pos, dk):
    half = dk // 2
    inv = 1.0 / (ROPE_BASE ** (jnp.arange(half, dtype=F32) / half))
    ang = pos[:, None] * inv[None, :]
    return jnp.cos(ang), jnp.sin(ang)


def kernel(x_prompt, x_sample, c_prompt, c_sample, state_ret, state_pool, norm_mix_g, norm_ffn_g, ada_w, ada_b,
           ret_w_in, ret_gn_g, ret_w_out, pool_w, pool_scale, ffn_w_in, ffn_w_out, final_norm_g):
    bp, tp, d = x_prompt.shape
    bs, ts, _ = x_sample.shape
    depth = ada_w.shape[0]
    dk = state_ret.shape[3]
    dv = state_ret.shape[4]
    n_hist = state_pool.shape[2]
    assert bp <= C_PAD and n_hist == max(POOL_WINDOWS) - 1 and depth == 2
    n_p = bp * tp
    n_s = bs * ts
    lay = dict(n_prompt=bp, t_prompt=tp, n_prompt_rows=n_p, n_sample_rows=n_s, sub=bs, t_sample=ts, dk=dk, dv=dv)

    xp = x_prompt.reshape(n_p, d)
    xs = x_sample.transpose(1, 0, 2).reshape(n_s, d)
    c_all = jnp.concatenate([c_prompt, jnp.zeros((C_PAD - bp, d), F32), c_sample], axis=0)
    mod = _ada_call(c_all, ada_w, ada_b, bp, bs)

    cos_p, sin_p = _rope_tables(jnp.arange(tp, dtype=F32), dk)
    cos_s, sin_s = _rope_tables(float(PAST_LEN) + jnp.arange(ts, dtype=F32), dk)
    cos = jnp.concatenate([jnp.tile(cos_p, (bp, 1)), jnp.repeat(cos_s, bs, axis=0)], axis=0)
    sin = jnp.concatenate([jnp.tile(sin_p, (bp, 1)), jnp.repeat(sin_s, bs, axis=0)], axis=0)

    h0 = _norm_call((xp, xs), norm_mix_g[0], mod, 0, 1, 0, lay, BF16)
    tn_ffn = 512
    d_ff = ffn_w_out.shape[1]
    cast_in = (d // 2, tn_ffn)
    cast_out = (d_ff // 8, tn_ffn)
    qkvg, (w_ret_out, w_ffn_in0) = _ret_proj_call(
        h0, ret_w_in, 0, lay, [(ret_w_out, 0, (ret_w_out.shape[1] // 8, tn_ffn)), (ffn_w_in, 0, cast_in)], (cos, sin))
    gated_p, gated_s, s_ret_p, s_ret_s = _retention_call(qkvg, ret_gn_g[0], state_ret[0], lay)
    x, w_ffn_out0, _ = _out_proj_call([(gated_p, xp), (gated_s, xs)], w_ret_out, mod, 0, 2, lay,
                                      side=(ffn_w_out, 0, cast_out))
    hmid, w_ffn_in1 = _ffn_in_call(x, norm_ffn_g[0], mod, 0, w_ffn_in0, lay, side=(ffn_w_in, 1, cast_in))
    x, w_ffn_out1, ssq = _out_proj_call([(hmid, x)], w_ffn_out0, mod, 0, 5, lay, side=(ffn_w_out, 1, cast_out),
                                        emit_ssq=True)

    pw = pool_w[0].astype(BF16)
    hist_p = jnp.zeros((bp, n_hist, d), F32)
    x1, nh_p = _pool_call(ssq, norm_mix_g[1], hist_p, x, mod, 1, pw, pool_scale[0], None, row0=0, n_rows=n_p,
                          n_seq=bp, stride=1, hist_len=0, mod_seq0=0, lay=lay)
    x, nh_s = _pool_call(ssq, norm_mix_g[1], state_pool[0], x, mod, 1, pw, pool_scale[0], x1,
                         row0=n_p, n_rows=n_s, n_seq=1, stride=bs, hist_len=min(PAST_LEN, n_hist), mod_seq0=bp,
                         lay=lay)
    hmid, _ = _ffn_in_call(x, norm_ffn_g[1], mod, 1, w_ffn_in1, lay)
    x, _, _ = _out_proj_call([(hmid, x)], w_ffn_out1, mod, 1, 5, lay)

    y_p, y_s = _norm_call((x,), final_norm_g, None, 0, 0, 0, lay, F32, split_out=True)
    y_prompt = y_p.reshape(bp, tp, d)
    y_sample = y_s.reshape(ts, bs, d).transpose(1, 0, 2)
    return (y_prompt, y_sample, s_ret_p[None], nh_p[None], s_ret_s[None], nh_s[None])
```

```python
import functools

import jax
import jax.numpy as jnp
from jax import lax
from jax.experimental import pallas as pl
from jax.experimental.pallas import tpu as pltpu

F32 = jnp.float32
BF16 = jnp.bfloat16

RET_HEADS = 8
RET_CHUNK = 128
ROPE_BASE = 10000.0
POOL_WINDOWS = (2, 4, 8, 16)
N_ADA = 6
EPS = 1e-6
PAST_LEN = 16384

V7X_VMEM_BYTES = 64 * 1024 * 1024
V7X_LANES = 128
V7X_SUBLANES = 8
NORM_ROWS = 16
C_PAD = 8


def _vmem_limit(block_bytes, scratch_bytes=0):
    need = 2 * block_bytes + scratch_bytes + 12 * 1024 * 1024
    return int(min(need, V7X_VMEM_BYTES - 6 * 1024 * 1024))


def _nbytes(shape, dtype):
    n = 1
    for s in shape:
        n *= s
    return n * jnp.dtype(dtype).itemsize


def _silu(x):
    return x / (1.0 + jnp.exp(-x))


def _ada_kernel(c_ref, w_ref, b_ref, o_ref, *, n_prompt, sub):
    cs = _silu(c_ref[...])
    res = jnp.dot(cs.astype(BF16), w_ref[0].astype(BF16), preferred_element_type=F32) + b_ref[0]
    tn = res.shape[-1]
    for b in range(n_prompt):
        o_ref[0, b * sub:(b + 1) * sub, :] = jnp.broadcast_to(res[b:b + 1, :], (sub, tn))
    o_ref[0, n_prompt * sub:, :] = res[C_PAD:C_PAD + sub, :]


def _ada_call(c_all, ada_w, ada_b, n_prompt, sub, tn=1024):
    depth, d, n6 = ada_w.shape
    rows = c_all.shape[0]
    out_rows = (n_prompt + 1) * sub
    blocks = (_nbytes((rows, d), F32) + _nbytes((d, tn), F32) + _nbytes((1, tn), F32)
              + _nbytes((out_rows, tn), F32))
    return pl.pallas_call(
        functools.partial(_ada_kernel, n_prompt=n_prompt, sub=sub),
        grid=(depth, n6 // tn),
        in_specs=[
            pl.BlockSpec((rows, d), lambda l, j: (0, 0)),
            pl.BlockSpec((1, d, tn), lambda l, j: (l, 0, j)),
            pl.BlockSpec((1, 1, tn), lambda l, j: (l, 0, j)),
        ],
        out_specs=pl.BlockSpec((1, out_rows, tn), lambda l, j: (l, 0, j)),
        out_shape=jax.ShapeDtypeStruct((depth, out_rows, n6), F32),
        compiler_params=pltpu.CompilerParams(
            dimension_semantics=("arbitrary", "arbitrary"),
            vmem_limit_bytes=_vmem_limit(blocks, _nbytes((d, tn), BF16))),
        name="ada_mod",
    )(c_all, ada_w, ada_b.reshape(depth, 1, n6))


def _norm_mod_rows(x_ref, gam_ref, sc_ref, sh_ref, out_ref, *, tm, sub):
    gam = gam_ref[...]

    def body(m, carry):
        m0 = pl.multiple_of(m * NORM_ROWS, NORM_ROWS)
        scale = gam
        if sc_ref is not None:
            scale = gam * (1.0 + sc_ref[pl.ds(m0, NORM_ROWS), :])
            shift = sh_ref[pl.ds(m0, NORM_ROWS), :]
        for s in range(tm // sub):
            rows = pl.ds(pl.multiple_of(s * sub + m0, NORM_ROWS), NORM_ROWS)
            x = x_ref[rows, :]
            hn = (x * lax.rsqrt(jnp.mean(x * x, axis=-1, keepdims=True) + EPS)) * scale
            if sc_ref is not None:
                hn = hn + shift
            out_ref[rows, :] = hn.astype(out_ref.dtype)
        return carry

    lax.fori_loop(0, sub // NORM_ROWS, body, 0)


def _mod_index(i, n_prompt_tiles, tiles_per_seq, n_prompt):
    return jnp.where(i < n_prompt_tiles, i // tiles_per_seq, n_prompt)


def _tiling(lay, tm):
    npt = lay["n_prompt_rows"] // tm
    midx = functools.partial(_mod_index, n_prompt_tiles=npt, tiles_per_seq=lay["t_prompt"] // tm,
                             n_prompt=lay["n_prompt"])
    return npt, midx


def _norm_kernel(*refs, tm, sub, n_in, n_out, has_mod, npt):
    x_refs = refs[:n_in]
    gam_ref = refs[n_in]
    sc_ref, sh_ref = (refs[n_in + 1], refs[n_in + 2]) if has_mod else (None, None)
    o_refs = refs[len(refs) - n_out:]
    if n_in == 1 and n_out == 1:
        _norm_mod_rows(x_refs[0], gam_ref, sc_ref, sh_ref, o_refs[0], tm=tm, sub=sub)
        return
    i = pl.program_id(0)

    @pl.when(i < npt)
    def _():
        _norm_mod_rows(x_refs[0], gam_ref, sc_ref, sh_ref, o_refs[0], tm=tm, sub=sub)

    @pl.when(i >= npt)
    def _():
        _norm_mod_rows(x_refs[-1], gam_ref, sc_ref, sh_ref, o_refs[-1], tm=tm, sub=sub)


def _norm_call(xs, gam, mod, layer, col_sc, col_sh, lay, out_dtype, split_out=False, tm=1024):
    d = xs[0].shape[1]
    sub = lay["sub"]
    n_p, n_s = lay["n_prompt_rows"], lay["n_sample_rows"]
    npt, midx = _tiling(lay, tm)
    prompt_blk = lambda i: (jnp.minimum(i, npt - 1), 0)
    sample_blk = lambda i: (jnp.maximum(i - npt, 0), 0)
    whole_blk = lambda i: (i, 0)
    in_specs = ([pl.BlockSpec((tm, d), whole_blk)] if len(xs) == 1
                else [pl.BlockSpec((tm, d), prompt_blk), pl.BlockSpec((tm, d), sample_blk)])
    in_specs.append(pl.BlockSpec((1, d), lambda i: (0, 0)))
    args = list(xs) + [gam.reshape(1, d)]
    if mod is not None:
        in_specs += [pl.BlockSpec((None, sub, d), lambda i: (layer, midx(i), col_sc)),
                     pl.BlockSpec((None, sub, d), lambda i: (layer, midx(i), col_sh))]
        args += [mod, mod]
    if split_out:
        out_specs = [pl.BlockSpec((tm, d), prompt_blk), pl.BlockSpec((tm, d), sample_blk)]
        out_shape = [jax.ShapeDtypeStruct((n_p, d), out_dtype), jax.ShapeDtypeStruct((n_s, d), out_dtype)]
    else:
        out_specs = pl.BlockSpec((tm, d), whole_blk)
        out_shape = jax.ShapeDtypeStruct((n_p + n_s, d), out_dtype)
    n_out = 2 if split_out else 1
    blocks = (len(xs) + n_out) * _nbytes((tm, d), F32) + 2 * _nbytes((sub, d), F32)
    return pl.pallas_call(
        functools.partial(_norm_kernel, tm=tm, sub=sub, n_in=len(xs), n_out=n_out,
                          has_mod=mod is not None, npt=npt),
        grid=((n_p + n_s) // tm,),
        in_specs=in_specs,
        out_specs=out_specs,
        out_shape=out_shape,
        compiler_params=pltpu.CompilerParams(
            dimension_semantics=("arbitrary",), vmem_limit_bytes=_vmem_limit(blocks)),
        name="row_norm",
    )(*args)


def _load_weight(w_ref, copy_ref):
    w = w_ref[...]
    if copy_ref is not None:
        w = w.astype(BF16)
        copy_ref[...] = w
    return w


class _SideCast:
    def __init__(self, src3, layer, block, n_i, n_j):
        _, r, c = src3.shape
        br, tn = block
        ncb = c // tn
        n_blocks = (r // br) * ncb
        assert n_blocks <= n_i * n_j and r % br == 0 and c % tn == 0

        def blk(i, j):
            t = jnp.minimum(i * n_j + j, n_blocks - 1)
            return t // ncb, t % ncb

        self.arg = src3
        self.in_spec = pl.BlockSpec((None, br, tn), lambda i, j: (layer, *blk(i, j)))
        self.out_spec = pl.BlockSpec((None, br, tn), lambda i, j: (blk(i, j)[1], blk(i, j)[0], 0))
        self.out_shape = jax.ShapeDtypeStruct((ncb, r, tn), BF16)
        self.block_bytes = _nbytes(block, F32) + _nbytes(block, BF16)


def _ret_proj_kernel(h_ref, w_ref, cos_ref, sin_ref, *rest, tn, d_qk, head_dk, k_scale, cast_w, has_prev, n_side):
    n_in = (1 if has_prev else 0) + n_side
    o_ref = rest[n_in]
    for s in range(n_side):
        rest[len(rest) - n_side + s][...] = rest[n_in - n_side + s][...].astype(BF16)
    w = _load_weight(w_ref, rest[n_in + 1] if cast_w else None)
    acc = jnp.dot(h_ref[...], w, preferred_element_type=F32)
    col0 = pl.program_id(1) * tn
    is_qk = col0 < 2 * d_qk
    scale = jnp.where(jnp.logical_and(is_qk, col0 >= d_qk), k_scale, 1.0).astype(F32)
    cs = jnp.where(is_qk, cos_ref[...], 1.0) * scale
    sn = jnp.where(is_qk, sin_ref[...], 0.0) * scale
    half = head_dk // 2
    for hh in range(tn // head_dk):
        x1 = acc[:, hh * head_dk:hh * head_dk + half]
        x2 = acc[:, hh * head_dk + half:(hh + 1) * head_dk]
        o_ref[:, hh * head_dk:hh * head_dk + half] = (x1 * cs - x2 * sn).astype(o_ref.dtype)
        o_ref[:, hh * head_dk + half:(hh + 1) * head_dk] = (x1 * sn + x2 * cs).astype(o_ref.dtype)


def _ret_proj_call(h, w3, w_layer, lay, sides, rope_tables, tm=1024, tn=1024, tn_first=512):
    n, d = h.shape
    n_cols = w3.shape[2]
    d_qk = RET_HEADS * lay["dk"]
    half = lay["dk"] // 2

    per_tile = tn // tn_first

    def run(i0, n_i, tn_, weights, cast_w, prev, side_jobs):
        w_spec = (pl.BlockSpec((None, d, tn_), lambda i, j: (w_layer, 0, j)) if cast_w
                  else pl.BlockSpec((None, d, tn_), lambda i, j: (j, 0, 0)))
        in_specs = [pl.BlockSpec((tm, d), lambda i, j: (i + i0, 0)), w_spec,
                    pl.BlockSpec((tm, half), lambda i, j: (i + i0, 0)),
                    pl.BlockSpec((tm, half), lambda i, j: (i + i0, 0))]
        args = [h, weights, *rope_tables]
        aliases = {}
        if prev is not None:
            in_specs.append(pl.BlockSpec(memory_space=pl.ANY))
            args.append(prev)
            aliases = {len(args) - 1: 0}
        out_specs = [pl.BlockSpec((tm, tn_), lambda i, j: (i + i0, j))]
        out_shape = [jax.ShapeDtypeStruct((n, n_cols), BF16)]
        w_bytes = _nbytes((d, tn_), BF16)
        if cast_w:
            out_specs.append(pl.BlockSpec((None, d, tn_), lambda i, j: (j // per_tile, 0, j % per_tile)))
            out_shape.append(jax.ShapeDtypeStruct((n_cols // tn, d, tn), BF16))
            w_bytes = _nbytes((d, tn_), F32) + 2 * _nbytes((d, tn_), BF16)
        casts = [_SideCast(*job, n_i, n_cols // tn_) for job in side_jobs]
        in_specs += [sc.in_spec for sc in casts]
        args += [sc.arg for sc in casts]
        out_specs += [sc.out_spec for sc in casts]
        out_shape += [sc.out_shape for sc in casts]
        w_bytes += sum(sc.block_bytes for sc in casts)
        blocks = _nbytes((tm, d), BF16) + w_bytes + 2 * _nbytes((tm, half), F32) + _nbytes((tm, tn_), BF16)
        return pl.pallas_call(
            functools.partial(_ret_proj_kernel, tn=tn_, d_qk=d_qk, head_dk=lay["dk"],
                              k_scale=float(lay["dk"]) ** -0.5, cast_w=cast_w,
                              has_prev=prev is not None, n_side=len(casts)),
            grid=(n_i, n_cols // tn_),
            in_specs=in_specs,
            out_specs=out_specs,
            out_shape=out_shape,
            input_output_aliases=aliases,
            compiler_params=pltpu.CompilerParams(
                dimension_semantics=("arbitrary", "arbitrary"),
                vmem_limit_bytes=_vmem_limit(blocks, _nbytes((tm, tn_), F32))),
            name="ret_in_proj" + ("_first" if cast_w else ""),
        )(*args)

    out0, w_bf = run(0, 1, tn_first, w3, True, None, [])
    out, *side_w = run(1, n // tm - 1, tn, w_bf, False, out0, sides)
    return out, side_w


def _ffn_in_kernel(x_ref, gam_ref, sc_ref, sh_ref, wg_ref, wu_ref, *rest, tm, sub, has_side):
    if has_side:
        side_in, o_ref, side_out, h_ref = rest
        side_out[...] = side_in[...].astype(BF16)
    else:
        o_ref, h_ref = rest

    @pl.when(pl.program_id(1) == 0)
    def _():
        _norm_mod_rows(x_ref, gam_ref, sc_ref, sh_ref, h_ref, tm=tm, sub=sub)

    h = h_ref[...]
    gate = jnp.dot(h, wg_ref[...], preferred_element_type=F32)
    up = jnp.dot(h, wu_ref[...], preferred_element_type=F32)
    o_ref[...] = (_silu(gate) * up).astype(o_ref.dtype)


def _ffn_in_call(x, gam, mod, layer, w, lay, side=None, tm=1024):
    n, d = x.shape
    sub = lay["sub"]
    tn = w.shape[2]
    n_j = w.shape[0] // 2
    d_ff = n_j * tn
    _, midx = _tiling(lay, tm)
    in_specs = [
        pl.BlockSpec((tm, d), lambda i, j: (i, 0)),
        pl.BlockSpec((1, d), lambda i, j: (0, 0)),
        pl.BlockSpec((None, sub, d), lambda i, j: (layer, midx(i), 4)),
        pl.BlockSpec((None, sub, d), lambda i, j: (layer, midx(i), 3)),
        pl.BlockSpec((None, d, tn), lambda i, j: (j, 0, 0)),
        pl.BlockSpec((None, d, tn), lambda i, j: (j + n_j, 0, 0)),
    ]
    args = [x, gam.reshape(1, d), mod, mod, w, w]
    out_specs = [pl.BlockSpec((tm, tn), lambda i, j: (i, j))]
    out_shape = [jax.ShapeDtypeStruct((n, d_ff), BF16)]
    blocks = (_nbytes((tm, d), F32) + 2 * _nbytes((sub, d), F32) + 2 * _nbytes((d, tn), BF16)
              + _nbytes((tm, tn), BF16))
    if side is not None:
        sc = _SideCast(*side, n // tm, n_j)
        in_specs.append(sc.in_spec)
        args.append(sc.arg)
        out_specs.append(sc.out_spec)
        out_shape.append(sc.out_shape)
        blocks += sc.block_bytes
    res = pl.pallas_call(
        functools.partial(_ffn_in_kernel, tm=tm, sub=sub, has_side=side is not None),
        grid=(n // tm, n_j),
        in_specs=in_specs,
        out_specs=out_specs,
        out_shape=out_shape,
        scratch_shapes=[pltpu.VMEM((tm, d), BF16)],
        compiler_params=pltpu.CompilerParams(
            dimension_semantics=("arbitrary", "arbitrary"),
            vmem_limit_bytes=_vmem_limit(blocks, _nbytes((tm, d), BF16) + 2 * _nbytes((tm, tn), F32))),
        name="ffn_in_swiglu",
    )(*args)
    return res[0], (res[1] if side is not None else None)


def _out_proj_kernel(*refs, tm, sub, npt, n_src, has_side, emit_ssq):
    a_refs, w_ref = refs[:n_src], refs[n_src]
    x_refs, gate_ref = refs[n_src + 1:2 * n_src + 1], refs[2 * n_src + 1]
    rest = list(refs[2 * n_src + 2:])
    ssq_ref = rest.pop() if emit_ssq else None
    if has_side:
        side_in, o_ref, side_out = rest
        side_out[...] = side_in[...].astype(BF16)
    else:
        (o_ref,) = rest

    def body(a_ref, x_ref):
        acc = jnp.dot(a_ref[...], w_ref[...], preferred_element_type=F32)
        gate = gate_ref[...]
        for s in range(tm // sub):
            rows = slice(s * sub, (s + 1) * sub)
            x_new = x_ref[rows, :] + gate * acc[rows, :]
            o_ref[rows, :] = x_new
            if emit_ssq:
                ssq_ref[rows, :] = jnp.sum(x_new * x_new, axis=-1, keepdims=True)

    if n_src == 1:
        body(a_refs[0], x_refs[0])
    else:
        is_prompt = pl.program_id(0) < npt
        pl.when(is_prompt)(functools.partial(body, a_refs[0], x_refs[0]))
        pl.when(jnp.logical_not(is_prompt))(functools.partial(body, a_refs[1], x_refs[1]))


def _out_proj_call(srcs, w, mod, layer, gate_col, lay, side=None, emit_ssq=False, tm=1024):
    n = sum(a.shape[0] for a, _ in srcs)
    k = srcs[0][0].shape[1]
    n_j, _, tn = w.shape
    d = n_j * tn
    sub = lay["sub"]
    npt, midx = _tiling(lay, tm)
    if len(srcs) == 1:
        a_specs = [pl.BlockSpec((tm, k), lambda i, j: (i, 0))]
        x_specs = [pl.BlockSpec((tm, tn), lambda i, j: (i, j))]
        a_bytes = 2 * _nbytes((tm, k), BF16)
    else:
        a_specs = [pl.BlockSpec((tm, k), lambda i, j: (jnp.minimum(i, npt - 1), 0)),
                   pl.BlockSpec((tm, k), lambda i, j: (jnp.maximum(i - npt, 0), 0), pipeline_mode=pl.Buffered(1))]
        x_specs = [pl.BlockSpec((tm, tn), lambda i, j: (jnp.minimum(i, npt - 1), j)),
                   pl.BlockSpec((tm, tn), lambda i, j: (jnp.maximum(i - npt, 0), j))]
        a_bytes = 3 * _nbytes((tm, k), BF16)
    in_specs = (a_specs + [pl.BlockSpec((None, k, tn), lambda i, j: (j, 0, 0))] + x_specs
                + [pl.BlockSpec((None, sub, tn), lambda i, j: (layer, midx(i), gate_col * n_j + j))])
    args = [a for a, _ in srcs] + [w] + [x for _, x in srcs] + [mod]
    out_specs = [pl.BlockSpec((tm, tn), lambda i, j: (i, j))]
    out_shape = [jax.ShapeDtypeStruct((n, d), F32)]
    blocks = (a_bytes // 2 + _nbytes((k, tn), BF16) + (1 + len(srcs)) * _nbytes((tm, tn), F32)
              + _nbytes((sub, tn), F32))
    if side is not None:
        sc = _SideCast(*side, n // tm, n_j)
        in_specs.append(sc.in_spec)
        args.append(sc.arg)
        out_specs.append(sc.out_spec)
        out_shape.append(sc.out_shape)
        blocks += sc.block_bytes
    if emit_ssq:
        out_specs.append(pl.BlockSpec((None, tm, 1), lambda i, j: (j, i, 0)))
        out_shape.append(jax.ShapeDtypeStruct((n_j, n, 1), F32))
        blocks += _nbytes((tm, V7X_LANES), F32)
    res = pl.pallas_call(
        functools.partial(_out_proj_kernel, tm=tm, sub=sub, npt=npt, n_src=len(srcs), has_side=side is not None,
                          emit_ssq=emit_ssq),
        grid=(n // tm, n_j),
        in_specs=in_specs,
        out_specs=out_specs,
        out_shape=out_shape,
        compiler_params=pltpu.CompilerParams(
            dimension_semantics=("arbitrary", "arbitrary"),
            vmem_limit_bytes=_vmem_limit(blocks, _nbytes((tm, tn), F32))),
        name="out_proj_residual",
    )(*args)
    return res[0], (res[1] if side is not None else None), (res[-1] if emit_ssq else None)


def _retention_decays(c):
    lg = jnp.log1p(-jnp.exp2(-5.0 - jnp.arange(RET_HEADS, dtype=F32)))
    idx = jnp.arange(c, dtype=F32)
    diff = idx[:, None] - idx[None, :]
    causal = diff >= 0
    dmask = jnp.where(causal[None], jnp.exp(jnp.where(causal, diff, 0.0)[None] * lg[:, None, None]), 0.0)
    qdec = jnp.exp((idx + 1.0)[None, :] * lg[:, None])
    kdec = jnp.exp((c - 1.0 - idx)[None, :] * lg[:, None])
    sdec = jnp.exp(c * lg)
    return dmask, qdec[:, :, None], kdec[:, :, None], sdec[:, None, None]


def _group_norm_gate(o, g, gng):
    mu = jnp.mean(o, axis=-1, keepdims=True)
    dlt = o - mu
    var = jnp.mean(dlt * dlt, axis=-1, keepdims=True)
    return _silu(g) * (dlt * lax.rsqrt(var + EPS) * gng)


_NT = (((1,), (1,)), ((), ()))
_TN = (((0,), (0,)), ((), ()))


def _retention_chunk(q, k, v, s_prev, dmask, qdec, kdec, sdec):
    scores = lax.dot_general(q, k, _NT, preferred_element_type=F32) * dmask
    o = (jnp.dot(scores.astype(BF16), v, preferred_element_type=F32)
         + qdec * jnp.dot(q, s_prev.astype(BF16), preferred_element_type=F32))
    kd = (k.astype(F32) * kdec).astype(BF16)
    s_new = sdec * s_prev + lax.dot_general(kd, v, _TN, preferred_element_type=F32)
    return o, s_new


def _retention_kernel(pq_ref, pk_ref, pv_ref, pg_ref, pgng_ref, pdmask_ref, pqdec_ref, pkdec_ref, psdec_ref,
                      sq_ref, sk_ref, sv_ref, sg_ref, sgng_ref, s0_ref, sdmask_ref, sqdec_ref, skdec_ref, ssdec_ref,
                      po_ref, ps_out_ref, so_ref, ss_out_ref,
                      ps_ref, qf_ref, kf_ref, vf_ref, of_ref, *, chunk, n_chunks, bb):
    @pl.when(pl.program_id(0) % 2 == 0)
    def _():
        ps_ref[...] = jnp.zeros_like(ps_ref)

    pdmask, pqdec, pkdec, psdec, pgng = pdmask_ref[0], pqdec_ref[0], pkdec_ref[0], psdec_ref[0], pgng_ref[...]

    def prompt_body(c, carry):
        rows = pl.ds(pl.multiple_of(c * chunk, chunk), chunk)
        o, s_new = _retention_chunk(pq_ref[rows, :], pk_ref[rows, :], pv_ref[rows, :], ps_ref[...],
                                    pdmask, pqdec, pkdec, psdec)
        ps_ref[...] = s_new
        po_ref[rows, :] = _group_norm_gate(o, pg_ref[rows, :].astype(F32), pgng).astype(po_ref.dtype)
        return carry

    lax.fori_loop(0, n_chunks, prompt_body, 0, unroll=2)
    ps_out_ref[0, 0] = ps_ref[...]

    smask, sqdec, skdec, ssdec, sgng = sdmask_ref[0], sqdec_ref[0], skdec_ref[0], ssdec_ref[0], sgng_ref[...]
    ts, _, dk = sq_ref.shape
    dv = sv_ref.shape[2]
    rows = ts * bb
    scores = lax.dot_general(sq_ref[...].reshape(rows, dk), sk_ref[...].reshape(rows, dk), _NT,
                             preferred_element_type=F32) * smask
    o_intra = jnp.dot(scores.astype(BF16), sv_ref[...].reshape(rows, dv), preferred_element_type=F32)
    qf_ref[...] = sq_ref[...].astype(F32)
    kf_ref[...] = sk_ref[...].astype(F32)
    vf_ref[...] = sv_ref[...].astype(F32)

    def sample_body(b, carry):
        s0 = s0_ref[b, 0]
        of_ref[:, b, :] = jnp.dot(qf_ref[:, b, :].astype(BF16), s0.astype(BF16), preferred_element_type=F32)
        kd = (kf_ref[:, b, :] * skdec).astype(BF16)
        ss_out_ref[b, 0] = ssdec * s0 + lax.dot_general(kd, vf_ref[:, b, :].astype(BF16), _TN,
                                                        preferred_element_type=F32)
        return carry

    lax.fori_loop(0, bb, sample_body, 0, unroll=4)
    o = o_intra + sqdec * of_ref[...].reshape(rows, dv)
    gated = _group_norm_gate(o, sg_ref[...].reshape(rows, dv).astype(F32), sgng)
    so_ref[...] = gated.astype(so_ref.dtype).reshape(ts, bb, dv)


def _retention_call(qkvg, gng, state, lay, chunk=2 * RET_CHUNK, bb=16):
    n = qkvg.shape[0]
    bp, tp, bs, ts, dk, dv = lay["n_prompt"], lay["t_prompt"], lay["sub"], lay["t_sample"], lay["dk"], lay["dv"]
    h = RET_HEADS
    n_p = lay["n_prompt_rows"]
    half = tp // 2
    n_steps = 2 * bp * h
    assert (bs // bb) * h == n_steps and half % chunk == 0
    t_blk = n_p // (bs * ts)
    assert t_blk * bs * ts == n_p
    v_blk0 = 2 * h * dk // dv
    g_blk0 = v_blk0 + h
    pdec = _retention_decays(chunk)
    dmask_s, qdec_s, kdec_s, sdec_s = _retention_decays(ts)
    same_seq = jnp.eye(bb, dtype=F32)[None, None, :, None, :]
    block_mask = (dmask_s[:, :, None, :, None] * same_seq).reshape(h, ts * bb, ts * bb)
    qdec_rows = jnp.broadcast_to(qdec_s[:, :, None, :], (h, ts, bb, 1)).reshape(h, ts * bb, 1)
    sdec = (block_mask, qdec_rows, kdec_s, sdec_s)
    qkv3 = qkvg.reshape(n // bs, bs, qkvg.shape[1])
    gng2 = gng.reshape(1, h * dv)

    p_row = lambda s: (s // 2 // h) * 2 + s % 2
    p_head = lambda s: (s // 2) % h
    s_blk = lambda s: s // h
    s_head = lambda s: s % h
    dec_specs = lambda c, head: [
        pl.BlockSpec((1, c, c), lambda s: (head(s), 0, 0)),
        pl.BlockSpec((1, c, 1), lambda s: (head(s), 0, 0)),
        pl.BlockSpec((1, c, 1), lambda s: (head(s), 0, 0)),
        pl.BlockSpec((1, 1, 1), lambda s: (head(s), 0, 0)),
    ]
    in_specs = [
        pl.BlockSpec((half, dk), lambda s: (p_row(s), p_head(s))),
        pl.BlockSpec((half, dk), lambda s: (p_row(s), h + p_head(s))),
        pl.BlockSpec((half, dv), lambda s: (p_row(s), v_blk0 + p_head(s))),
        pl.BlockSpec((half, dv), lambda s: (p_row(s), g_blk0 + p_head(s))),
        pl.BlockSpec((1, dv), lambda s: (0, p_head(s))),
        *dec_specs(chunk, p_head),
        pl.BlockSpec((ts, bb, dk), lambda s: (t_blk, s_blk(s), s_head(s))),
        pl.BlockSpec((ts, bb, dk), lambda s: (t_blk, s_blk(s), h + s_head(s))),
        pl.BlockSpec((ts, bb, dv), lambda s: (t_blk, s_blk(s), v_blk0 + s_head(s))),
        pl.BlockSpec((ts, bb, dv), lambda s: (t_blk, s_blk(s), g_blk0 + s_head(s))),
        pl.BlockSpec((1, dv), lambda s: (0, s_head(s))),
        pl.BlockSpec((bb, 1, dk, dv), lambda s: (s_blk(s), s_head(s), 0, 0)),
        pl.BlockSpec((1, ts * bb, ts * bb), lambda s: (s_head(s), 0, 0)),
        pl.BlockSpec((1, ts * bb, 1), lambda s: (s_head(s), 0, 0)),
        pl.BlockSpec((1, ts, 1), lambda s: (s_head(s), 0, 0)),
        pl.BlockSpec((1, 1, 1), lambda s: (s_head(s), 0, 0)),
    ]
    out_specs = [
        pl.BlockSpec((half, dv), lambda s: (p_row(s), p_head(s))),
        pl.BlockSpec((1, 1, dk, dv), lambda s: (s // 2 // h, p_head(s), 0, 0)),
        pl.BlockSpec((ts, bb, dv), lambda s: (0, s_blk(s), s_head(s))),
        pl.BlockSpec((bb, 1, dk, dv), lambda s: (s_blk(s), s_head(s), 0, 0)),
    ]
    out_shape = [jax.ShapeDtypeStruct((n_p, h * dv), BF16),
                 jax.ShapeDtypeStruct((bp, h, dk, dv), F32),
                 jax.ShapeDtypeStruct((ts, bs, h * dv), BF16),
                 jax.ShapeDtypeStruct((bs, h, dk, dv), F32)]
    blocks = (2 * _nbytes((half, dk), BF16) + 3 * _nbytes((half, dv), BF16) + _nbytes((dk, dv), F32)
              + _nbytes((chunk, chunk), F32)
              + 2 * _nbytes((ts, bb, dk), BF16) + 3 * _nbytes((ts, bb, dv), BF16) + 2 * _nbytes((bb, dk, dv), F32))
    scratch = _nbytes((dk, dv), F32) + 2 * _nbytes((ts, bb, dk), F32) + 2 * _nbytes((ts, bb, dv), F32)
    gated_p, s_p, gated_s, s_s = pl.pallas_call(
        functools.partial(_retention_kernel, chunk=chunk, n_chunks=half // chunk, bb=bb),
        grid=(n_steps,),
        in_specs=in_specs,
        out_specs=out_specs,
        out_shape=out_shape,
        scratch_shapes=[pltpu.VMEM((dk, dv), F32),
                        pltpu.VMEM((ts, bb, dk), F32), pltpu.VMEM((ts, bb, dk), F32),
                        pltpu.VMEM((ts, bb, dv), F32), pltpu.VMEM((ts, bb, dv), F32)],
        compiler_params=pltpu.CompilerParams(
            dimension_semantics=("arbitrary",),
            vmem_limit_bytes=_vmem_limit(blocks, scratch)),
        name="retention",
    )(qkvg, qkvg, qkvg, qkvg, gng2, *pdec, qkv3, qkv3, qkv3, qkv3, gng2, state, *sdec)
    return gated_p, gated_s.reshape(bs * ts, h * dv), s_p, s_s


def _window_sum_rows(win_ref, halo, r0, rows, win, stride):
    if stride % V7X_SUBLANES == 0:
        acc = win_ref[pl.ds(halo + r0, rows), :]
        for k in range(1, win):
            acc = acc + win_ref[pl.ds(halo - k * stride + r0, rows), :]
        return acc
    assert stride == 1 and halo == 16 and win <= 16
    blk = win_ref[pl.ds(r0, rows + 16), :]
    shift = 1
    while shift < win:
        blk = blk + pltpu.roll(blk, shift, 0)
        shift *= 2
    return blk[16:, :]


def _pool_kernel(ssq_ref, gam_ref, sc_ref, sh_ref, hist_ref, x_ref, gate_ref, w_ref, scale_ref, *rest,
                 tm, sub, stride, tiles_per_seq, hist_len, chunk_rows, d_model):
    o_ref, hist_out_ref, win_ref, m_ref = rest[-4:]
    g = pl.program_id(0)
    i = pl.program_id(1)
    halo = 16 * stride
    tile_in_seq = i % tiles_per_seq

    if tiles_per_seq > 1:
        @pl.when(tile_in_seq != 0)
        def _():
            win_ref[0:halo, :] = win_ref[tm:tm + halo, :]

    @pl.when(tile_in_seq == 0)
    def _():
        win_ref[0:halo, :] = hist_ref[0]

    gam = gam_ref[...]
    for s in range(tm // sub):
        rows = slice(s * sub, (s + 1) * sub)
        ssq = ssq_ref[0, rows, :]
        for c in range(1, ssq_ref.shape[0]):
            ssq = ssq + ssq_ref[c, rows, :]
        rs = lax.rsqrt(ssq / d_model + EPS)
        win_ref[halo + s * sub:halo + (s + 1) * sub, :] = (
            ((x_ref[rows, :] * rs) * gam) * (1.0 + sc_ref[...]) + sh_ref[...])

    for gi, win in enumerate(POOL_WINDOWS):
        @pl.when(g == gi)
        def _(win=win):
            def body(c, carry):
                r0 = pl.multiple_of(c * chunk_rows, chunk_rows)
                cur = win_ref[pl.ds(halo + r0, chunk_rows), :]
                acc = _window_sum_rows(win_ref, halo, r0, chunk_rows, win, stride)
                t = (tile_in_seq * tm + r0 + lax.broadcasted_iota(jnp.int32, (chunk_rows, 1), 0)) // stride
                cnt = jnp.minimum(win, t + 1 + hist_len).astype(F32)
                m_ref[pl.ds(r0, chunk_rows), :] = (acc / cnt - cur).astype(m_ref.dtype)
                return carry

            lax.fori_loop(0, tm // chunk_rows, body, 0)

    y = jnp.dot(m_ref[...], w_ref[0], preferred_element_type=F32) * scale_ref[...]
    gate = gate_ref[...]
    for s in range(tm // sub):
        rows = slice(s * sub, (s + 1) * sub)
        o_ref[rows, :] = x_ref[rows, :] + gate * y[rows, :]
    hist_out_ref[0] = win_ref[tm + stride:tm + halo, :]


def _pool_call(ssq, gam, hist, x, mod, layer, w, scale, x_prev, *, row0, n_rows, n_seq, stride, hist_len, mod_seq0,
               lay, tm=1024):
    n, d = x.shape
    sub = lay["sub"]
    n_g = len(POOL_WINDOWS)
    gc = d // n_g
    n_tiles = n_rows // tm
    tps = n_tiles // n_seq
    blk0 = row0 // tm
    halo = 16 * stride
    n_parts = ssq.shape[0]
    blocks = (2 * _nbytes((tm, gc), F32) + _nbytes((halo, gc), F32) + 3 * _nbytes((sub, gc), F32)
              + _nbytes((gc, gc), BF16) + _nbytes((15 * stride, gc), F32) + n_parts * _nbytes((tm, V7X_LANES), F32))
    scratch = _nbytes((halo + tm, gc), F32) + _nbytes((tm, gc), BF16)
    sh_col0, sc_col0, gate_col0 = 0, n_g, 2 * n_g
    mod_blk = lambda col0: pl.BlockSpec((None, sub, gc), lambda g, i: (layer, mod_seq0 + i // tps, col0 + g))
    in_specs = [
        pl.BlockSpec((n_parts, tm, 1), lambda g, i: (0, blk0 + i, 0)),
        pl.BlockSpec((1, gc), lambda g, i: (0, g)),
        mod_blk(sc_col0),
        mod_blk(sh_col0),
        pl.BlockSpec((1, halo, gc), lambda g, i: (i // tps, 0, g)),
        pl.BlockSpec((tm, gc), lambda g, i: (blk0 + i, g)),
        mod_blk(gate_col0),
        pl.BlockSpec((1, gc, gc), lambda g, i: (g, 0, 0)),
        pl.BlockSpec((1, gc), lambda g, i: (0, g)),
    ]
    args = [ssq, gam.reshape(1, d), mod, mod, hist, x, mod, w, scale.reshape(1, d)]
    aliases = {}
    if x_prev is not None:
        in_specs.append(pl.BlockSpec(memory_space=pl.ANY))
        args.append(x_prev)
        aliases = {len(args) - 1: 0}
    return pl.pallas_call(
        functools.partial(_pool_kernel, tm=tm, sub=sub, stride=stride, tiles_per_seq=tps,
                          hist_len=hist_len, chunk_rows=64, d_model=d),
        grid=(n_g, n_tiles),
        in_specs=in_specs,
        out_specs=[
            pl.BlockSpec((tm, gc), lambda g, i: (blk0 + i, g)),
            pl.BlockSpec((1, 15 * stride, gc), lambda g, i: (i // tps, 0, g)),
        ],
        out_shape=[jax.ShapeDtypeStruct((n, d), F32),
                   jax.ShapeDtypeStruct((n_seq, 15 * stride, d), F32)],
        scratch_shapes=[pltpu.VMEM((halo + tm, gc), F32), pltpu.VMEM((tm, gc), BF16)],
        input_output_aliases=aliases,
        compiler_params=pltpu.CompilerParams(
            dimension_semantics=("arbitrary", "arbitrary"),
            vmem_limit_bytes=_vmem_limit(blocks, scratch)),
        name="pool_mixer_s%d" % stride,
    )(*args)


def _rope_tables(pos, dk):
    half = dk // 2
    inv = 1.0 / (ROPE_BASE ** (jnp.arange(half, dtype=F32) / half))
    ang = pos[:, None] * inv[None, :]
    return jnp.cos(ang), jnp.sin(ang)


def kernel(x_prompt, x_sample, c_prompt, c_sample, state_ret, state_pool, norm_mix_g, norm_ffn_g, ada_w, ada_b,
           ret_w_in, ret_gn_g, ret_w_out, pool_w, pool_scale, ffn_w_in, ffn_w_out, final_norm_g):
    bp, tp, d = x_prompt.shape
    bs, ts, _ = x_sample.shape
    depth = ada_w.shape[0]
    dk = state_ret.shape[3]
    dv = state_ret.shape[4]
    n_hist = state_pool.shape[2]
    assert bp <= C_PAD and n_hist == max(POOL_WINDOWS) - 1 and depth == 2
    n_p = bp * tp
    n_s = bs * ts
    lay = dict(n_prompt=bp, t_prompt=tp, n_prompt_rows=n_p, n_sample_rows=n_s, sub=bs, t_sample=ts, dk=dk, dv=dv)

    xp = x_prompt.reshape(n_p, d)
    xs = x_sample.transpose(1, 0, 2).reshape(n_s, d)
    c_all = jnp.concatenate([c_prompt, jnp.zeros((C_PAD - bp, d), F32), c_sample], axis=0)
    mod = _ada_call(c_all, ada_w, ada_b, bp, bs)

    cos_p, sin_p = _rope_tables(jnp.arange(tp, dtype=F32), dk)
    cos_s, sin_s = _rope_tables(float(PAST_LEN) + jnp.arange(ts, dtype=F32), dk)
    cos = jnp.concatenate([jnp.tile(cos_p, (bp, 1)), jnp.repeat(cos_s, bs, axis=0)], axis=0)
    sin = jnp.concatenate([jnp.tile(sin_p, (bp, 1)), jnp.repeat(sin_s, bs, axis=0)], axis=0)

    h0 = _norm_call((xp, xs), norm_mix_g[0], mod, 0, 1, 0, lay, BF16)
    tn_ffn = 512
    d_ff = ffn_w_out.shape[1]
    cast_in = (d // 2, tn_ffn)
    cast_out = (d_ff // 8, tn_ffn)
    qkvg, (w_ret_out, w_ffn_in0) = _ret_proj_call(
        h0, ret_w_in, 0, lay, [(ret_w_out, 0, (ret_w_out.shape[1] // 8, tn_ffn)), (ffn_w_in, 0, cast_in)], (cos, sin))
    gated_p, gated_s, s_ret_p, s_ret_s = _retention_call(qkvg, ret_gn_g[0], state_ret[0], lay)
    x, w_ffn_out0, _ = _out_proj_call([(gated_p, xp), (gated_s, xs)], w_ret_out, mod, 0, 2, lay,
                                      side=(ffn_w_out, 0, cast_out))
    hmid, w_ffn_in1 = _ffn_in_call(x, norm_ffn_g[0], mod, 0, w_ffn_in0, lay, side=(ffn_w_in, 1, cast_in))
    x, w_ffn_out1, ssq = _out_proj_call([(hmid, x)], w_ffn_out0, mod, 0, 5, lay, side=(ffn_w_out, 1, cast_out),
                                        emit_ssq=True)

    pw = pool_w[0].astype(BF16)
    hist_p = jnp.zeros((bp, 16, d), F32)
    hist_s = jnp.concatenate([jnp.zeros((1, bs, d), F32), state_pool[0].transpose(1, 0, 2)], axis=0)
    x1, nh_p = _pool_call(ssq, norm_mix_g[1], hist_p, x, mod, 1, pw, pool_scale[0], None, row0=0, n_rows=n_p,
                          n_seq=bp, stride=1, hist_len=0, mod_seq0=0, lay=lay)
    x, nh_s = _pool_call(ssq, norm_mix_g[1], hist_s.reshape(1, 16 * bs, d), x, mod, 1, pw, pool_scale[0], x1,
                         row0=n_p, n_rows=n_s, n_seq=1, stride=bs, hist_len=min(PAST_LEN, n_hist), mod_seq0=bp,
                         lay=lay)
    hmid, _ = _ffn_in_call(x, norm_ffn_g[1], mod, 1, w_ffn_in1, lay)
    x, _, _ = _out_proj_call([(hmid, x)], w_ffn_out1, mod, 1, 5, lay)

    y_p, y_s = _norm_call((x,), final_norm_g, None, 0, 0, 0, lay, F32, split_out=True)
    y_prompt = y_p.reshape(bp, tp, d)
    y_sample = y_s.reshape(ts, bs, d).transpose(1, 0, 2)
    state_pool_sample = nh_s.reshape(n_hist, bs, d).transpose(1, 0, 2)
    return (y_prompt, y_sample, s_ret_p[None], nh_p[None], s_ret_s[None], state_pool_sample[None])
```

```python
import functools

import jax
import jax.numpy as jnp
from jax import lax
from jax.experimental import pallas as pl
from jax.experimental.pallas import tpu as pltpu

F32 = jnp.float32
BF16 = jnp.bfloat16

RET_HEADS = 8
RET_CHUNK = 128
ROPE_BASE = 10000.0
POOL_WINDOWS = (2, 4, 8, 16)
N_ADA = 6
EPS = 1e-6
PAST_LEN = 16384

V7X_VMEM_BYTES = 64 * 1024 * 1024
V7X_LANES = 128
V7X_SUBLANES = 8
NORM_ROWS = 16
C_PAD = 8


def _vmem_limit(block_bytes, scratch_bytes=0):
    need = 2 * block_bytes + scratch_bytes + 12 * 1024 * 1024
    return int(min(need, V7X_VMEM_BYTES - 6 * 1024 * 1024))


def _nbytes(shape, dtype):
    n = 1
    for s in shape:
        n *= s
    return n * jnp.dtype(dtype).itemsize


def _silu(x):
    return x / (1.0 + jnp.exp(-x))


def _ada_kernel(c_ref, w_ref, b_ref, o_ref, *, n_prompt, sub):
    cs = _silu(c_ref[...])
    res = jnp.dot(cs.astype(BF16), w_ref[0].astype(BF16), preferred_element_type=F32) + b_ref[0]
    tn = res.shape[-1]
    for b in range(n_prompt):
        o_ref[0, b * sub:(b + 1) * sub, :] = jnp.broadcast_to(res[b:b + 1, :], (sub, tn))
    o_ref[0, n_prompt * sub:, :] = res[C_PAD:C_PAD + sub, :]


def _ada_call(c_all, ada_w, ada_b, n_prompt, sub, tn=1024):
    depth, d, n6 = ada_w.shape
    rows = c_all.shape[0]
    out_rows = (n_prompt + 1) * sub
    blocks = (_nbytes((rows, d), F32) + _nbytes((d, tn), F32) + _nbytes((1, tn), F32)
              + _nbytes((out_rows, tn), F32))
    return pl.pallas_call(
        functools.partial(_ada_kernel, n_prompt=n_prompt, sub=sub),
        grid=(depth, n6 // tn),
        in_specs=[
            pl.BlockSpec((rows, d), lambda l, j: (0, 0)),
            pl.BlockSpec((1, d, tn), lambda l, j: (l, 0, j)),
            pl.BlockSpec((1, 1, tn), lambda l, j: (l, 0, j)),
        ],
        out_specs=pl.BlockSpec((1, out_rows, tn), lambda l, j: (l, 0, j)),
        out_shape=jax.ShapeDtypeStruct((depth, out_rows, n6), F32),
        compiler_params=pltpu.CompilerParams(
            dimension_semantics=("arbitrary", "arbitrary"),
            vmem_limit_bytes=_vmem_limit(blocks, _nbytes((d, tn), BF16))),
        name="ada_mod",
    )(c_all, ada_w, ada_b.reshape(depth, 1, n6))


def _norm_mod_rows(x_ref, gam_ref, sc_ref, sh_ref, out_ref, *, tm, sub):
    gam = gam_ref[...]

    def body(m, carry):
        m0 = pl.multiple_of(m * NORM_ROWS, NORM_ROWS)
        scale = gam
        if sc_ref is not None:
            scale = gam * (1.0 + sc_ref[pl.ds(m0, NORM_ROWS), :])
            shift = sh_ref[pl.ds(m0, NORM_ROWS), :]
        for s in range(tm // sub):
            rows = pl.ds(pl.multiple_of(s * sub + m0, NORM_ROWS), NORM_ROWS)
            x = x_ref[rows, :]
            hn = (x * lax.rsqrt(jnp.mean(x * x, axis=-1, keepdims=True) + EPS)) * scale
            if sc_ref is not None:
                hn = hn + shift
            out_ref[rows, :] = hn.astype(out_ref.dtype)
        return carry

    lax.fori_loop(0, sub // NORM_ROWS, body, 0)


def _mod_index(i, n_prompt_tiles, tiles_per_seq, n_prompt):
    return jnp.where(i < n_prompt_tiles, i // tiles_per_seq, n_prompt)


def _tiling(lay, tm):
    npt = lay["n_prompt_rows"] // tm
    midx = functools.partial(_mod_index, n_prompt_tiles=npt, tiles_per_seq=lay["t_prompt"] // tm,
                             n_prompt=lay["n_prompt"])
    return npt, midx


def _norm_kernel(*refs, tm, sub, n_in, n_out, has_mod, npt):
    x_refs = refs[:n_in]
    gam_ref = refs[n_in]
    sc_ref, sh_ref = (refs[n_in + 1], refs[n_in + 2]) if has_mod else (None, None)
    o_refs = refs[len(refs) - n_out:]
    if n_in == 1 and n_out == 1:
        _norm_mod_rows(x_refs[0], gam_ref, sc_ref, sh_ref, o_refs[0], tm=tm, sub=sub)
        return
    i = pl.program_id(0)

    @pl.when(i < npt)
    def _():
        _norm_mod_rows(x_refs[0], gam_ref, sc_ref, sh_ref, o_refs[0], tm=tm, sub=sub)

    @pl.when(i >= npt)
    def _():
        _norm_mod_rows(x_refs[-1], gam_ref, sc_ref, sh_ref, o_refs[-1], tm=tm, sub=sub)


def _norm_call(xs, gam, mod, layer, col_sc, col_sh, lay, out_dtype, split_out=False, tm=1024):
    d = xs[0].shape[1]
    sub = lay["sub"]
    n_p, n_s = lay["n_prompt_rows"], lay["n_sample_rows"]
    npt, midx = _tiling(lay, tm)
    prompt_blk = lambda i: (jnp.minimum(i, npt - 1), 0)
    sample_blk = lambda i: (jnp.maximum(i - npt, 0), 0)
    whole_blk = lambda i: (i, 0)
    in_specs = ([pl.BlockSpec((tm, d), whole_blk)] if len(xs) == 1
                else [pl.BlockSpec((tm, d), prompt_blk), pl.BlockSpec((tm, d), sample_blk)])
    in_specs.append(pl.BlockSpec((1, d), lambda i: (0, 0)))
    args = list(xs) + [gam.reshape(1, d)]
    if mod is not None:
        in_specs += [pl.BlockSpec((None, sub, d), lambda i: (layer, midx(i), col_sc)),
                     pl.BlockSpec((None, sub, d), lambda i: (layer, midx(i), col_sh))]
        args += [mod, mod]
    if split_out:
        out_specs = [pl.BlockSpec((tm, d), prompt_blk), pl.BlockSpec((tm, d), sample_blk)]
        out_shape = [jax.ShapeDtypeStruct((n_p, d), out_dtype), jax.ShapeDtypeStruct((n_s, d), out_dtype)]
    else:
        out_specs = pl.BlockSpec((tm, d), whole_blk)
        out_shape = jax.ShapeDtypeStruct((n_p + n_s, d), out_dtype)
    n_out = 2 if split_out else 1
    blocks = (len(xs) + n_out) * _nbytes((tm, d), F32) + 2 * _nbytes((sub, d), F32)
    return pl.pallas_call(
        functools.partial(_norm_kernel, tm=tm, sub=sub, n_in=len(xs), n_out=n_out,
                          has_mod=mod is not None, npt=npt),
        grid=((n_p + n_s) // tm,),
        in_specs=in_specs,
        out_specs=out_specs,
        out_shape=out_shape,
        compiler_params=pltpu.CompilerParams(
            dimension_semantics=("arbitrary",), vmem_limit_bytes=_vmem_limit(blocks)),
        name="row_norm",
    )(*args)


def _load_weight(w_ref, copy_ref):
    w = w_ref[...]
    if copy_ref is not None:
        w = w.astype(BF16)
        copy_ref[...] = w
    return w


class _SideCast:
    def __init__(self, src3, layer, block, n_i, n_j):
        _, r, c = src3.shape
        br, tn = block
        ncb = c // tn
        n_blocks = (r // br) * ncb
        assert n_blocks <= n_i * n_j and r % br == 0 and c % tn == 0

        def blk(i, j):
            t = jnp.minimum(i * n_j + j, n_blocks - 1)
            return t // ncb, t % ncb

        self.arg = src3
        self.in_spec = pl.BlockSpec((None, br, tn), lambda i, j: (layer, *blk(i, j)))
        self.out_spec = pl.BlockSpec((None, br, tn), lambda i, j: (blk(i, j)[1], blk(i, j)[0], 0))
        self.out_shape = jax.ShapeDtypeStruct((ncb, r, tn), BF16)
        self.block_bytes = _nbytes(block, F32) + _nbytes(block, BF16)


def _ret_proj_kernel(h_ref, w_ref, cos_ref, sin_ref, *rest, tn, d_qk, head_dk, k_scale, cast_w, has_prev, n_side):
    n_in = (1 if has_prev else 0) + n_side
    o_ref = rest[n_in]
    for s in range(n_side):
        rest[len(rest) - n_side + s][...] = rest[n_in - n_side + s][...].astype(BF16)
    w = _load_weight(w_ref, rest[n_in + 1] if cast_w else None)
    acc = jnp.dot(h_ref[...], w, preferred_element_type=F32)
    col0 = pl.program_id(1) * tn
    is_qk = col0 < 2 * d_qk
    scale = jnp.where(jnp.logical_and(is_qk, col0 >= d_qk), k_scale, 1.0).astype(F32)
    cs = jnp.where(is_qk, cos_ref[...], 1.0) * scale
    sn = jnp.where(is_qk, sin_ref[...], 0.0) * scale
    half = head_dk // 2
    for hh in range(tn // head_dk):
        x1 = acc[:, hh * head_dk:hh * head_dk + half]
        x2 = acc[:, hh * head_dk + half:(hh + 1) * head_dk]
        o_ref[:, hh * head_dk:hh * head_dk + half] = (x1 * cs - x2 * sn).astype(o_ref.dtype)
        o_ref[:, hh * head_dk + half:(hh + 1) * head_dk] = (x1 * sn + x2 * cs).astype(o_ref.dtype)


def _ret_proj_call(h, w3, w_layer, lay, sides, rope_tables, tm=1024, tn=1024, tm_first=2048, tn_first=512):
    n, d = h.shape
    n_cols = w3.shape[2]
    d_qk = RET_HEADS * lay["dk"]
    half = lay["dk"] // 2

    per_tile = tn // tn_first

    def run(tm_, i0, n_i, tn_, weights, cast_w, prev, side_jobs):
        w_spec = (pl.BlockSpec((None, d, tn_), lambda i, j: (w_layer, 0, j)) if cast_w
                  else pl.BlockSpec((None, d, tn_), lambda i, j: (j, 0, 0)))
        in_specs = [pl.BlockSpec((tm_, d), lambda i, j: (i + i0, 0)), w_spec,
                    pl.BlockSpec((tm_, half), lambda i, j: (i + i0, 0)),
                    pl.BlockSpec((tm_, half), lambda i, j: (i + i0, 0))]
        args = [h, weights, *rope_tables]
        aliases = {}
        if prev is not None:
            in_specs.append(pl.BlockSpec(memory_space=pl.ANY))
            args.append(prev)
            aliases = {len(args) - 1: 0}
        out_specs = [pl.BlockSpec((tm_, tn_), lambda i, j: (i + i0, j))]
        out_shape = [jax.ShapeDtypeStruct((n, n_cols), BF16)]
        w_bytes = _nbytes((d, tn_), BF16)
        if cast_w:
            out_specs.append(pl.BlockSpec((None, d, tn_), lambda i, j: (j // per_tile, 0, j % per_tile)))
            out_shape.append(jax.ShapeDtypeStruct((n_cols // tn, d, tn), BF16))
            w_bytes = _nbytes((d, tn_), F32) + 2 * _nbytes((d, tn_), BF16)
        casts = [_SideCast(*job, n_i, n_cols // tn_) for job in side_jobs]
        in_specs += [sc.in_spec for sc in casts]
        args += [sc.arg for sc in casts]
        out_specs += [sc.out_spec for sc in casts]
        out_shape += [sc.out_shape for sc in casts]
        w_bytes += sum(sc.block_bytes for sc in casts)
        blocks = _nbytes((tm_, d), BF16) + w_bytes + 2 * _nbytes((tm_, half), F32) + _nbytes((tm_, tn_), BF16)
        return pl.pallas_call(
            functools.partial(_ret_proj_kernel, tn=tn_, d_qk=d_qk, head_dk=lay["dk"],
                              k_scale=float(lay["dk"]) ** -0.5, cast_w=cast_w,
                              has_prev=prev is not None, n_side=len(casts)),
            grid=(n_i, n_cols // tn_),
            in_specs=in_specs,
            out_specs=out_specs,
            out_shape=out_shape,
            input_output_aliases=aliases,
            compiler_params=pltpu.CompilerParams(
                dimension_semantics=("arbitrary", "arbitrary"),
                vmem_limit_bytes=_vmem_limit(blocks, _nbytes((tm_, tn_), F32))),
            name="ret_in_proj" + ("_first" if cast_w else ""),
        )(*args)

    assert tm_first % tm == 0 and (n - tm_first) % tm == 0
    out0, w_bf = run(tm_first, 0, 1, tn_first, w3, True, None, [])
    out, *side_w = run(tm, tm_first // tm, (n - tm_first) // tm, tn, w_bf, False, out0, sides)
    return out, side_w


def _ffn_in_kernel(x_ref, gam_ref, sc_ref, sh_ref, wg_ref, wu_ref, *rest, tm, sub, has_side):
    if has_side:
        side_in, o_ref, side_out, h_ref = rest
        side_out[...] = side_in[...].astype(BF16)
    else:
        o_ref, h_ref = rest

    @pl.when(pl.program_id(1) == 0)
    def _():
        _norm_mod_rows(x_ref, gam_ref, sc_ref, sh_ref, h_ref, tm=tm, sub=sub)

    h = h_ref[...]
    gate = jnp.dot(h, wg_ref[...], preferred_element_type=F32)
    up = jnp.dot(h, wu_ref[...], preferred_element_type=F32)
    o_ref[...] = (_silu(gate) * up).astype(o_ref.dtype)


def _ffn_in_call(x, gam, mod, layer, w, lay, side=None, tm=1024):
    n, d = x.shape
    sub = lay["sub"]
    tn = w.shape[2]
    n_j = w.shape[0] // 2
    d_ff = n_j * tn
    _, midx = _tiling(lay, tm)
    in_specs = [
        pl.BlockSpec((tm, d), lambda i, j: (i, 0)),
        pl.BlockSpec((1, d), lambda i, j: (0, 0)),
        pl.BlockSpec((None, sub, d), lambda i, j: (layer, midx(i), 4)),
        pl.BlockSpec((None, sub, d), lambda i, j: (layer, midx(i), 3)),
        pl.BlockSpec((None, d, tn), lambda i, j: (j, 0, 0)),
        pl.BlockSpec((None, d, tn), lambda i, j: (j + n_j, 0, 0)),
    ]
    args = [x, gam.reshape(1, d), mod, mod, w, w]
    out_specs = [pl.BlockSpec((tm, tn), lambda i, j: (i, j))]
    out_shape = [jax.ShapeDtypeStruct((n, d_ff), BF16)]
    blocks = (_nbytes((tm, d), F32) + 2 * _nbytes((sub, d), F32) + 2 * _nbytes((d, tn), BF16)
              + _nbytes((tm, tn), BF16))
    if side is not None:
        sc = _SideCast(*side, n // tm, n_j)
        in_specs.append(sc.in_spec)
        args.append(sc.arg)
        out_specs.append(sc.out_spec)
        out_shape.append(sc.out_shape)
        blocks += sc.block_bytes
    res = pl.pallas_call(
        functools.partial(_ffn_in_kernel, tm=tm, sub=sub, has_side=side is not None),
        grid=(n // tm, n_j),
        in_specs=in_specs,
        out_specs=out_specs,
        out_shape=out_shape,
        scratch_shapes=[pltpu.VMEM((tm, d), BF16)],
        compiler_params=pltpu.CompilerParams(
            dimension_semantics=("arbitrary", "arbitrary"),
            vmem_limit_bytes=_vmem_limit(blocks, _nbytes((tm, d), BF16) + 2 * _nbytes((tm, tn), F32))),
        name="ffn_in_swiglu",
    )(*args)
    return res[0], (res[1] if side is not None else None)


def _out_proj_kernel(*refs, tm, sub, npt, n_src, has_side, emit_ssq):
    a_refs, w_ref = refs[:n_src], refs[n_src]
    x_refs, gate_ref = refs[n_src + 1:2 * n_src + 1], refs[2 * n_src + 1]
    rest = list(refs[2 * n_src + 2:])
    ssq_ref = rest.pop() if emit_ssq else None
    if has_side:
        side_in, o_ref, side_out = rest
        side_out[...] = side_in[...].astype(BF16)
    else:
        (o_ref,) = rest

    def body(a_ref, x_ref):
        acc = jnp.dot(a_ref[...], w_ref[...], preferred_element_type=F32)
        gate = gate_ref[...]
        for s in range(tm // sub):
            rows = slice(s * sub, (s + 1) * sub)
            x_new = x_ref[rows, :] + gate * acc[rows, :]
            o_ref[rows, :] = x_new
            if emit_ssq:
                ssq_ref[rows, :] = jnp.sum(x_new * x_new, axis=-1, keepdims=True)

    if n_src == 1:
        body(a_refs[0], x_refs[0])
    else:
        is_prompt = pl.program_id(0) < npt
        pl.when(is_prompt)(functools.partial(body, a_refs[0], x_refs[0]))
        pl.when(jnp.logical_not(is_prompt))(functools.partial(body, a_refs[1], x_refs[1]))


def _out_proj_call(srcs, w, mod, layer, gate_col, lay, side=None, emit_ssq=False, tm=1024):
    n = sum(a.shape[0] for a, _ in srcs)
    k = srcs[0][0].shape[1]
    n_j, _, tn = w.shape
    d = n_j * tn
    sub = lay["sub"]
    npt, midx = _tiling(lay, tm)
    if len(srcs) == 1:
        a_specs = [pl.BlockSpec((tm, k), lambda i, j: (i, 0))]
        x_specs = [pl.BlockSpec((tm, tn), lambda i, j: (i, j))]
        a_bytes = 2 * _nbytes((tm, k), BF16)
    else:
        a_specs = [pl.BlockSpec((tm, k), lambda i, j: (jnp.minimum(i, npt - 1), 0)),
                   pl.BlockSpec((tm, k), lambda i, j: (jnp.maximum(i - npt, 0), 0), pipeline_mode=pl.Buffered(1))]
        x_specs = [pl.BlockSpec((tm, tn), lambda i, j: (jnp.minimum(i, npt - 1), j)),
                   pl.BlockSpec((tm, tn), lambda i, j: (jnp.maximum(i - npt, 0), j))]
        a_bytes = 3 * _nbytes((tm, k), BF16)
    in_specs = (a_specs + [pl.BlockSpec((None, k, tn), lambda i, j: (j, 0, 0))] + x_specs
                + [pl.BlockSpec((None, sub, tn), lambda i, j: (layer, midx(i), gate_col * n_j + j))])
    args = [a for a, _ in srcs] + [w] + [x for _, x in srcs] + [mod]
    out_specs = [pl.BlockSpec((tm, tn), lambda i, j: (i, j))]
    out_shape = [jax.ShapeDtypeStruct((n, d), F32)]
    blocks = (a_bytes // 2 + _nbytes((k, tn), BF16) + (1 + len(srcs)) * _nbytes((tm, tn), F32)
              + _nbytes((sub, tn), F32))
    if side is not None:
        sc = _SideCast(*side, n // tm, n_j)
        in_specs.append(sc.in_spec)
        args.append(sc.arg)
        out_specs.append(sc.out_spec)
        out_shape.append(sc.out_shape)
        blocks += sc.block_bytes
    if emit_ssq:
        out_specs.append(pl.BlockSpec((None, tm, 1), lambda i, j: (j, i, 0)))
        out_shape.append(jax.ShapeDtypeStruct((n_j, n, 1), F32))
        blocks += _nbytes((tm, V7X_LANES), F32)
    res = pl.pallas_call(
        functools.partial(_out_proj_kernel, tm=tm, sub=sub, npt=npt, n_src=len(srcs), has_side=side is not None,
                          emit_ssq=emit_ssq),
        grid=(n // tm, n_j),
        in_specs=in_specs,
        out_specs=out_specs,
        out_shape=out_shape,
        compiler_params=pltpu.CompilerParams(
            dimension_semantics=("arbitrary", "arbitrary"),
            vmem_limit_bytes=_vmem_limit(blocks, _nbytes((tm, tn), F32))),
        name="out_proj_residual",
    )(*args)
    return res[0], (res[1] if side is not None else None), (res[-1] if emit_ssq else None)


def _retention_decays(c):
    lg = jnp.log1p(-jnp.exp2(-5.0 - jnp.arange(RET_HEADS, dtype=F32)))
    idx = jnp.arange(c, dtype=F32)
    diff = idx[:, None] - idx[None, :]
    causal = diff >= 0
    dmask = jnp.where(causal[None], jnp.exp(jnp.where(causal, diff, 0.0)[None] * lg[:, None, None]), 0.0)
    qdec = jnp.exp((idx + 1.0)[None, :] * lg[:, None])
    kdec = jnp.exp((c - 1.0 - idx)[None, :] * lg[:, None])
    sdec = jnp.exp(c * lg)
    return dmask, qdec[:, :, None], kdec[:, :, None], sdec[:, None, None]


def _group_norm_gate(o, g, gng):
    mu = jnp.mean(o, axis=-1, keepdims=True)
    dlt = o - mu
    var = jnp.mean(dlt * dlt, axis=-1, keepdims=True)
    return _silu(g) * (dlt * lax.rsqrt(var + EPS) * gng)


_NT = (((1,), (1,)), ((), ()))
_TN = (((0,), (0,)), ((), ()))


def _retention_chunk(q, k, v, s_prev, dmask, qdec, kdec, sdec):
    scores = lax.dot_general(q, k, _NT, preferred_element_type=F32) * dmask
    o = (jnp.dot(scores.astype(BF16), v, preferred_element_type=F32)
         + qdec * jnp.dot(q, s_prev.astype(BF16), preferred_element_type=F32))
    kd = (k.astype(F32) * kdec).astype(BF16)
    s_new = sdec * s_prev + lax.dot_general(kd, v, _TN, preferred_element_type=F32)
    return o, s_new


def _retention_kernel(pq_ref, pk_ref, pv_ref, pg_ref, pgng_ref, pdmask_ref, pqdec_ref, pkdec_ref, psdec_ref,
                      sq_ref, sk_ref, sv_ref, sg_ref, sgng_ref, s0_ref, sdmask_ref, sqdec_ref, skdec_ref, ssdec_ref,
                      po_ref, ps_out_ref, so_ref, ss_out_ref,
                      ps_ref, qf_ref, kf_ref, vf_ref, of_ref, *, chunk, n_chunks, bb):
    @pl.when(pl.program_id(0) % 2 == 0)
    def _():
        ps_ref[...] = jnp.zeros_like(ps_ref)

    pdmask, pqdec, pkdec, psdec, pgng = pdmask_ref[0], pqdec_ref[0], pkdec_ref[0], psdec_ref[0], pgng_ref[...]

    def prompt_body(c, carry):
        rows = pl.ds(pl.multiple_of(c * chunk, chunk), chunk)
        o, s_new = _retention_chunk(pq_ref[rows, :], pk_ref[rows, :], pv_ref[rows, :], ps_ref[...],
                                    pdmask, pqdec, pkdec, psdec)
        ps_ref[...] = s_new
        po_ref[rows, :] = _group_norm_gate(o, pg_ref[rows, :].astype(F32), pgng).astype(po_ref.dtype)
        return carry

    lax.fori_loop(0, n_chunks, prompt_body, 0, unroll=2)
    ps_out_ref[0, 0] = ps_ref[...]

    smask, sqdec, skdec, ssdec, sgng = sdmask_ref[0], sqdec_ref[0], skdec_ref[0], ssdec_ref[0], sgng_ref[...]
    ts, _, dk = sq_ref.shape
    dv = sv_ref.shape[2]
    rows = ts * bb
    scores = lax.dot_general(sq_ref[...].reshape(rows, dk), sk_ref[...].reshape(rows, dk), _NT,
                             preferred_element_type=F32) * smask
    o_intra = jnp.dot(scores.astype(BF16), sv_ref[...].reshape(rows, dv), preferred_element_type=F32)
    qf_ref[...] = sq_ref[...].astype(F32)
    kf_ref[...] = sk_ref[...].astype(F32)
    vf_ref[...] = sv_ref[...].astype(F32)

    def sample_body(b, carry):
        s0 = s0_ref[b, 0]
        of_ref[:, b, :] = jnp.dot(qf_ref[:, b, :].astype(BF16), s0.astype(BF16), preferred_element_type=F32)
        kd = (kf_ref[:, b, :] * skdec).astype(BF16)
        ss_out_ref[b, 0] = ssdec * s0 + lax.dot_general(kd, vf_ref[:, b, :].astype(BF16), _TN,
                                                        preferred_element_type=F32)
        return carry

    lax.fori_loop(0, bb, sample_body, 0, unroll=4)
    o = o_intra + sqdec * of_ref[...].reshape(rows, dv)
    gated = _group_norm_gate(o, sg_ref[...].reshape(rows, dv).astype(F32), sgng)
    so_ref[...] = gated.astype(so_ref.dtype).reshape(ts, bb, dv)


def _retention_call(qkvg, gng, state, lay, chunk=2 * RET_CHUNK, bb=16):
    n = qkvg.shape[0]
    bp, tp, bs, ts, dk, dv = lay["n_prompt"], lay["t_prompt"], lay["sub"], lay["t_sample"], lay["dk"], lay["dv"]
    h = RET_HEADS
    n_p = lay["n_prompt_rows"]
    half = tp // 2
    n_steps = 2 * bp * h
    assert (bs // bb) * h == n_steps and half % chunk == 0
    t_blk = n_p // (bs * ts)
    assert t_blk * bs * ts == n_p
    v_blk0 = 2 * h * dk // dv
    g_blk0 = v_blk0 + h
    pdec = _retention_decays(chunk)
    dmask_s, qdec_s, kdec_s, sdec_s = _retention_decays(ts)
    same_seq = jnp.eye(bb, dtype=F32)[None, None, :, None, :]
    block_mask = (dmask_s[:, :, None, :, None] * same_seq).reshape(h, ts * bb, ts * bb)
    qdec_rows = jnp.broadcast_to(qdec_s[:, :, None, :], (h, ts, bb, 1)).reshape(h, ts * bb, 1)
    sdec = (block_mask, qdec_rows, kdec_s, sdec_s)
    qkv3 = qkvg.reshape(n // bs, bs, qkvg.shape[1])
    gng2 = gng.reshape(1, h * dv)

    p_row = lambda s: (s // 2 // h) * 2 + s % 2
    p_head = lambda s: (s // 2) % h
    s_blk = lambda s: s // h
    s_head = lambda s: s % h
    dec_specs = lambda c, head: [
        pl.BlockSpec((1, c, c), lambda s: (head(s), 0, 0)),
        pl.BlockSpec((1, c, 1), lambda s: (head(s), 0, 0)),
        pl.BlockSpec((1, c, 1), lambda s: (head(s), 0, 0)),
        pl.BlockSpec((1, 1, 1), lambda s: (head(s), 0, 0)),
    ]
    in_specs = [
        pl.BlockSpec((half, dk), lambda s: (p_row(s), p_head(s))),
        pl.BlockSpec((half, dk), lambda s: (p_row(s), h + p_head(s))),
        pl.BlockSpec((half, dv), lambda s: (p_row(s), v_blk0 + p_head(s))),
        pl.BlockSpec((half, dv), lambda s: (p_row(s), g_blk0 + p_head(s))),
        pl.BlockSpec((1, dv), lambda s: (0, p_head(s))),
        *dec_specs(chunk, p_head),
        pl.BlockSpec((ts, bb, dk), lambda s: (t_blk, s_blk(s), s_head(s))),
        pl.BlockSpec((ts, bb, dk), lambda s: (t_blk, s_blk(s), h + s_head(s))),
        pl.BlockSpec((ts, bb, dv), lambda s: (t_blk, s_blk(s), v_blk0 + s_head(s))),
        pl.BlockSpec((ts, bb, dv), lambda s: (t_blk, s_blk(s), g_blk0 + s_head(s))),
        pl.BlockSpec((1, dv), lambda s: (0, s_head(s))),
        pl.BlockSpec((bb, 1, dk, dv), lambda s: (s_blk(s), s_head(s), 0, 0)),
        pl.BlockSpec((1, ts * bb, ts * bb), lambda s: (s_head(s), 0, 0)),
        pl.BlockSpec((1, ts * bb, 1), lambda s: (s_head(s), 0, 0)),
        pl.BlockSpec((1, ts, 1), lambda s: (s_head(s), 0, 0)),
        pl.BlockSpec((1, 1, 1), lambda s: (s_head(s), 0, 0)),
    ]
    out_specs = [
        pl.BlockSpec((half, dv), lambda s: (p_row(s), p_head(s))),
        pl.BlockSpec((1, 1, dk, dv), lambda s: (s // 2 // h, p_head(s), 0, 0)),
        pl.BlockSpec((ts, bb, dv), lambda s: (0, s_blk(s), s_head(s))),
        pl.BlockSpec((bb, 1, dk, dv), lambda s: (s_blk(s), s_head(s), 0, 0)),
    ]
    out_shape = [jax.ShapeDtypeStruct((n_p, h * dv), BF16),
                 jax.ShapeDtypeStruct((bp, h, dk, dv), F32),
                 jax.ShapeDtypeStruct((ts, bs, h * dv), BF16),
                 jax.ShapeDtypeStruct((bs, h, dk, dv), F32)]
    blocks = (2 * _nbytes((half, dk), BF16) + 3 * _nbytes((half, dv), BF16) + _nbytes((dk, dv), F32)
              + _nbytes((chunk, chunk), F32)
              + 2 * _nbytes((ts, bb, dk), BF16) + 3 * _nbytes((ts, bb, dv), BF16) + 2 * _nbytes((bb, dk, dv), F32))
    scratch = _nbytes((dk, dv), F32) + 2 * _nbytes((ts, bb, dk), F32) + 2 * _nbytes((ts, bb, dv), F32)
    gated_p, s_p, gated_s, s_s = pl.pallas_call(
        functools.partial(_retention_kernel, chunk=chunk, n_chunks=half // chunk, bb=bb),
        grid=(n_steps,),
        in_specs=in_specs,
        out_specs=out_specs,
        out_shape=out_shape,
        scratch_shapes=[pltpu.VMEM((dk, dv), F32),
                        pltpu.VMEM((ts, bb, dk), F32), pltpu.VMEM((ts, bb, dk), F32),
                        pltpu.VMEM((ts, bb, dv), F32), pltpu.VMEM((ts, bb, dv), F32)],
        compiler_params=pltpu.CompilerParams(
            dimension_semantics=("arbitrary",),
            vmem_limit_bytes=_vmem_limit(blocks, scratch)),
        name="retention",
    )(qkvg, qkvg, qkvg, qkvg, gng2, *pdec, qkv3, qkv3, qkv3, qkv3, gng2, state, *sdec)
    return gated_p, gated_s.reshape(bs * ts, h * dv), s_p, s_s


def _window_sum_rows(win_ref, halo, r0, rows, win, stride):
    if stride % V7X_SUBLANES == 0:
        acc = win_ref[pl.ds(halo + r0, rows), :]
        for k in range(1, win):
            acc = acc + win_ref[pl.ds(halo - k * stride + r0, rows), :]
        return acc
    assert stride == 1 and halo == 16 and win <= 16
    blk = win_ref[pl.ds(r0, rows + 16), :]
    shift = 1
    while shift < win:
        blk = blk + pltpu.roll(blk, shift, 0)
        shift *= 2
    return blk[16:, :]


def _pool_kernel(ssq_ref, gam_ref, sc_ref, sh_ref, hist_ref, x_ref, gate_ref, w_ref, scale_ref, *rest,
                 tm, sub, stride, tiles_per_seq, hist_len, chunk_rows, d_model):
    o_ref, hist_out_ref, win_ref, m_ref = rest[-4:]
    g = pl.program_id(0)
    i = pl.program_id(1)
    halo = 16 * stride
    tile_in_seq = i % tiles_per_seq

    if tiles_per_seq > 1:
        @pl.when(tile_in_seq != 0)
        def _():
            win_ref[0:halo, :] = win_ref[tm:tm + halo, :]

    @pl.when(tile_in_seq == 0)
    def _():
        win_ref[0:halo, :] = hist_ref[0]

    gam = gam_ref[...]
    for s in range(tm // sub):
        rows = slice(s * sub, (s + 1) * sub)
        ssq = ssq_ref[0, rows, :]
        for c in range(1, ssq_ref.shape[0]):
            ssq = ssq + ssq_ref[c, rows, :]
        rs = lax.rsqrt(ssq / d_model + EPS)
        win_ref[halo + s * sub:halo + (s + 1) * sub, :] = (
            ((x_ref[rows, :] * rs) * gam) * (1.0 + sc_ref[...]) + sh_ref[...])

    for gi, win in enumerate(POOL_WINDOWS):
        @pl.when(g == gi)
        def _(win=win):
            def body(c, carry):
                r0 = pl.multiple_of(c * chunk_rows, chunk_rows)
                cur = win_ref[pl.ds(halo + r0, chunk_rows), :]
                acc = _window_sum_rows(win_ref, halo, r0, chunk_rows, win, stride)
                t = (tile_in_seq * tm + r0 + lax.broadcasted_iota(jnp.int32, (chunk_rows, 1), 0)) // stride
                cnt = jnp.minimum(win, t + 1 + hist_len).astype(F32)
                m_ref[pl.ds(r0, chunk_rows), :] = (acc / cnt - cur).astype(m_ref.dtype)
                return carry

            lax.fori_loop(0, tm // chunk_rows, body, 0)

    y = jnp.dot(m_ref[...], w_ref[0], preferred_element_type=F32) * scale_ref[...]
    gate = gate_ref[...]
    for s in range(tm // sub):
        rows = slice(s * sub, (s + 1) * sub)
        o_ref[rows, :] = x_ref[rows, :] + gate * y[rows, :]
    hist_out_ref[0] = win_ref[tm + stride:tm + halo, :]


def _pool_call(ssq, gam, hist, x, mod, layer, w, scale, x_prev, *, row0, n_rows, n_seq, stride, hist_len, mod_seq0,
               lay, tm=1024):
    n, d = x.shape
    sub = lay["sub"]
    n_g = len(POOL_WINDOWS)
    gc = d // n_g
    n_tiles = n_rows // tm
    tps = n_tiles // n_seq
    blk0 = row0 // tm
    halo = 16 * stride
    n_parts = ssq.shape[0]
    blocks = (2 * _nbytes((tm, gc), F32) + _nbytes((halo, gc), F32) + 3 * _nbytes((sub, gc), F32)
              + _nbytes((gc, gc), BF16) + _nbytes((15 * stride, gc), F32) + n_parts * _nbytes((tm, V7X_LANES), F32))
    scratch = _nbytes((halo + tm, gc), F32) + _nbytes((tm, gc), BF16)
    sh_col0, sc_col0, gate_col0 = 0, n_g, 2 * n_g
    mod_blk = lambda col0: pl.BlockSpec((None, sub, gc), lambda g, i: (layer, mod_seq0 + i // tps, col0 + g))
    in_specs = [
        pl.BlockSpec((n_parts, tm, 1), lambda g, i: (0, blk0 + i, 0)),
        pl.BlockSpec((1, gc), lambda g, i: (0, g)),
        mod_blk(sc_col0),
        mod_blk(sh_col0),
        pl.BlockSpec((1, halo, gc), lambda g, i: (i // tps, 0, g)),
        pl.BlockSpec((tm, gc), lambda g, i: (blk0 + i, g)),
        mod_blk(gate_col0),
        pl.BlockSpec((1, gc, gc), lambda g, i: (g, 0, 0)),
        pl.BlockSpec((1, gc), lambda g, i: (0, g)),
    ]
    args = [ssq, gam.reshape(1, d), mod, mod, hist, x, mod, w, scale.reshape(1, d)]
    aliases = {}
    if x_prev is not None:
        in_specs.append(pl.BlockSpec(memory_space=pl.ANY))
        args.append(x_prev)
        aliases = {len(args) - 1: 0}
    return pl.pallas_call(
        functools.partial(_pool_kernel, tm=tm, sub=sub, stride=stride, tiles_per_seq=tps,
                          hist_len=hist_len, chunk_rows=64, d_model=d),
        grid=(n_g, n_tiles),
        in_specs=in_specs,
        out_specs=[
            pl.BlockSpec((tm, gc), lambda g, i: (blk0 + i, g)),
            pl.BlockSpec((1, 15 * stride, gc), lambda g, i: (i // tps, 0, g)),
        ],
        out_shape=[jax.ShapeDtypeStruct((n, d), F32),
                   jax.ShapeDtypeStruct((n_seq, 15 * stride, d), F32)],
        scratch_shapes=[pltpu.VMEM((halo + tm, gc), F32), pltpu.VMEM((tm, gc), BF16)],
        input_output_aliases=aliases,
        compiler_params=pltpu.CompilerParams(
            dimension_semantics=("arbitrary", "arbitrary"),
            vmem_limit_bytes=_vmem_limit(blocks, scratch)),
        name="pool_mixer_s%d" % stride,
    )(*args)


def _rope_tables(pos, dk):
    half = dk // 2
    inv = 1.0 / (ROPE_BASE ** (jnp.arange(half, dtype=F32) / half))
    ang = pos[:, None] * inv[None, :]
    return jnp.cos(ang), jnp.sin(ang)


def kernel(x_prompt, x_sample, c_prompt, c_sample, state_ret, state_pool, norm_mix_g, norm_ffn_g, ada_w, ada_b,
           ret_w_in, ret_gn_g, ret_w_out, pool_w, pool_scale, ffn_w_in, ffn_w_out, final_norm_g):
    bp, tp, d = x_prompt.shape
    bs, ts, _ = x_sample.shape
    depth = ada_w.shape[0]
    dk = state_ret.shape[3]
    dv = state_ret.shape[4]
    n_hist = state_pool.shape[2]
    assert bp <= C_PAD and n_hist == max(POOL_WINDOWS) - 1 and depth == 2
    n_p = bp * tp
    n_s = bs * ts
    lay = dict(n_prompt=bp, t_prompt=tp, n_prompt_rows=n_p, n_sample_rows=n_s, sub=bs, t_sample=ts, dk=dk, dv=dv)

    xp = x_prompt.reshape(n_p, d)
    xs = x_sample.transpose(1, 0, 2).reshape(n_s, d)
    c_all = jnp.concatenate([c_prompt, jnp.zeros((C_PAD - bp, d), F32), c_sample], axis=0)
    mod = _ada_call(c_all, ada_w, ada_b, bp, bs)

    cos_p, sin_p = _rope_tables(jnp.arange(tp, dtype=F32), dk)
    cos_s, sin_s = _rope_tables(float(PAST_LEN) + jnp.arange(ts, dtype=F32), dk)
    cos = jnp.concatenate([jnp.tile(cos_p, (bp, 1)), jnp.repeat(cos_s, bs, axis=0)], axis=0)
    sin = jnp.concatenate([jnp.tile(sin_p, (bp, 1)), jnp.repeat(sin_s, bs, axis=0)], axis=0)

    h0 = _norm_call((xp, xs), norm_mix_g[0], mod, 0, 1, 0, lay, BF16)
    tn_ffn = 512
    d_ff = ffn_w_out.shape[1]
    cast_in = (d // 2, tn_ffn)
    cast_out = (d_ff // 8, tn_ffn)
    qkvg, (w_ret_out, w_ffn_in0) = _ret_proj_call(
        h0, ret_w_in, 0, lay, [(ret_w_out, 0, (ret_w_out.shape[1] // 8, tn_ffn)), (ffn_w_in, 0, cast_in)], (cos, sin))
    gated_p, gated_s, s_ret_p, s_ret_s = _retention_call(qkvg, ret_gn_g[0], state_ret[0], lay)
    x, w_ffn_out0, _ = _out_proj_call([(gated_p, xp), (gated_s, xs)], w_ret_out, mod, 0, 2, lay,
                                      side=(ffn_w_out, 0, cast_out))
    hmid, w_ffn_in1 = _ffn_in_call(x, norm_ffn_g[0], mod, 0, w_ffn_in0, lay, side=(ffn_w_in, 1, cast_in))
    x, w_ffn_out1, ssq = _out_proj_call([(hmid, x)], w_ffn_out0, mod, 0, 5, lay, side=(ffn_w_out, 1, cast_out),
                                        emit_ssq=True)

    pw = pool_w[0].astype(BF16)
    hist_p = jnp.zeros((bp, 16, d), F32)
    hist_s = jnp.concatenate([jnp.zeros((1, bs, d), F32), state_pool[0].transpose(1, 0, 2)], axis=0)
    x1, nh_p = _pool_call(ssq, norm_mix_g[1], hist_p, x, mod, 1, pw, pool_scale[0], None, row0=0, n_rows=n_p,
                          n_seq=bp, stride=1, hist_len=0, mod_seq0=0, lay=lay)
    x, nh_s = _pool_call(ssq, norm_mix_g[1], hist_s.reshape(1, 16 * bs, d), x, mod, 1, pw, pool_scale[0], x1,
                         row0=n_p, n_rows=n_s, n_seq=1, stride=bs, hist_len=min(PAST_LEN, n_hist), mod_seq0=bp,
                         lay=lay)
    hmid, _ = _ffn_in_call(x, norm_ffn_g[1], mod, 1, w_ffn_in1, lay)
    x, _, _ = _out_proj_call([(hmid, x)], w_ffn_out1, mod, 1, 5, lay)

    y_p, y_s = _norm_call((x,), final_norm_g, None, 0, 0, 0, lay, F32, split_out=True)
    y_prompt = y_p.reshape(bp, tp, d)
    y_sample = y_s.reshape(ts, bs, d).transpose(1, 0, 2)
    state_pool_sample = nh_s.reshape(n_hist, bs, d).transpose(1, 0, 2)
    return (y_prompt, y_sample, s_ret_p[None], nh_p[None], s_ret_s[None], state_pool_sample[None])
```

```python
import functools

import jax
import jax.numpy as jnp
from jax import lax
from jax.experimental import pallas as pl
from jax.experimental.pallas import tpu as pltpu

F32 = jnp.float32
BF16 = jnp.bfloat16

RET_HEADS = 8
RET_CHUNK = 128
ROPE_BASE = 10000.0
POOL_WINDOWS = (2, 4, 8, 16)
N_ADA = 6
EPS = 1e-6
PAST_LEN = 16384

V7X_VMEM_BYTES = 64 * 1024 * 1024
V7X_LANES = 128
V7X_SUBLANES = 8
NORM_ROWS = 16
C_PAD = 8


def _vmem_limit(block_bytes, scratch_bytes=0):
    need = 2 * block_bytes + scratch_bytes + 12 * 1024 * 1024
    return int(min(need, V7X_VMEM_BYTES - 6 * 1024 * 1024))


def _nbytes(shape, dtype):
    n = 1
    for s in shape:
        n *= s
    return n * jnp.dtype(dtype).itemsize


def _silu(x):
    return x / (1.0 + jnp.exp(-x))


def _ada_kernel(c_ref, w_ref, b_ref, o_ref, *, n_prompt, sub):
    cs = _silu(c_ref[...])
    res = jnp.dot(cs.astype(BF16), w_ref[0].astype(BF16), preferred_element_type=F32) + b_ref[0]
    tn = res.shape[-1]
    for b in range(n_prompt):
        o_ref[0, b * sub:(b + 1) * sub, :] = jnp.broadcast_to(res[b:b + 1, :], (sub, tn))
    o_ref[0, n_prompt * sub:, :] = res[C_PAD:C_PAD + sub, :]


def _ada_call(c_all, ada_w, ada_b, n_prompt, sub, tn=1024):
    depth, d, n6 = ada_w.shape
    rows = c_all.shape[0]
    out_rows = (n_prompt + 1) * sub
    blocks = (_nbytes((rows, d), F32) + _nbytes((d, tn), F32) + _nbytes((1, tn), F32)
              + _nbytes((out_rows, tn), F32))
    return pl.pallas_call(
        functools.partial(_ada_kernel, n_prompt=n_prompt, sub=sub),
        grid=(depth, n6 // tn),
        in_specs=[
            pl.BlockSpec((rows, d), lambda l, j: (0, 0)),
            pl.BlockSpec((1, d, tn), lambda l, j: (l, 0, j)),
            pl.BlockSpec((1, 1, tn), lambda l, j: (l, 0, j)),
        ],
        out_specs=pl.BlockSpec((1, out_rows, tn), lambda l, j: (l, 0, j)),
        out_shape=jax.ShapeDtypeStruct((depth, out_rows, n6), F32),
        compiler_params=pltpu.CompilerParams(
            dimension_semantics=("arbitrary", "arbitrary"),
            vmem_limit_bytes=_vmem_limit(blocks, _nbytes((d, tn), BF16))),
        name="ada_mod",
    )(c_all, ada_w, ada_b.reshape(depth, 1, n6))


def _norm_mod_rows(x_ref, gam_ref, sc_ref, sh_ref, out_ref, *, tm, sub):
    gam = gam_ref[...]

    def body(m, carry):
        m0 = pl.multiple_of(m * NORM_ROWS, NORM_ROWS)
        scale = gam
        if sc_ref is not None:
            scale = gam * (1.0 + sc_ref[pl.ds(m0, NORM_ROWS), :])
            shift = sh_ref[pl.ds(m0, NORM_ROWS), :]
        for s in range(tm // sub):
            rows = pl.ds(pl.multiple_of(s * sub + m0, NORM_ROWS), NORM_ROWS)
            x = x_ref[rows, :]
            hn = (x * lax.rsqrt(jnp.mean(x * x, axis=-1, keepdims=True) + EPS)) * scale
            if sc_ref is not None:
                hn = hn + shift
            out_ref[rows, :] = hn.astype(out_ref.dtype)
        return carry

    lax.fori_loop(0, sub // NORM_ROWS, body, 0)


def _mod_index(i, n_prompt_tiles, tiles_per_seq, n_prompt):
    return jnp.where(i < n_prompt_tiles, i // tiles_per_seq, n_prompt)


def _tiling(lay, tm):
    npt = lay["n_prompt_rows"] // tm
    midx = functools.partial(_mod_index, n_prompt_tiles=npt, tiles_per_seq=lay["t_prompt"] // tm,
                             n_prompt=lay["n_prompt"])
    return npt, midx


def _norm_kernel(*refs, tm, sub, n_in, n_out, has_mod, npt):
    x_refs = refs[:n_in]
    gam_ref = refs[n_in]
    sc_ref, sh_ref = (refs[n_in + 1], refs[n_in + 2]) if has_mod else (None, None)
    o_refs = refs[len(refs) - n_out:]
    if n_in == 1 and n_out == 1:
        _norm_mod_rows(x_refs[0], gam_ref, sc_ref, sh_ref, o_refs[0], tm=tm, sub=sub)
        return
    i = pl.program_id(0)

    @pl.when(i < npt)
    def _():
        _norm_mod_rows(x_refs[0], gam_ref, sc_ref, sh_ref, o_refs[0], tm=tm, sub=sub)

    @pl.when(i >= npt)
    def _():
        _norm_mod_rows(x_refs[-1], gam_ref, sc_ref, sh_ref, o_refs[-1], tm=tm, sub=sub)


def _norm_call(xs, gam, mod, layer, col_sc, col_sh, lay, out_dtype, split_out=False, tm=1024):
    d = xs[0].shape[1]
    sub = lay["sub"]
    n_p, n_s = lay["n_prompt_rows"], lay["n_sample_rows"]
    npt, midx = _tiling(lay, tm)
    prompt_blk = lambda i: (jnp.minimum(i, npt - 1), 0)
    sample_blk = lambda i: (jnp.maximum(i - npt, 0), 0)
    whole_blk = lambda i: (i, 0)
    in_specs = ([pl.BlockSpec((tm, d), whole_blk)] if len(xs) == 1
                else [pl.BlockSpec((tm, d), prompt_blk), pl.BlockSpec((tm, d), sample_blk)])
    in_specs.append(pl.BlockSpec((1, d), lambda i: (0, 0)))
    args = list(xs) + [gam.reshape(1, d)]
    if mod is not None:
        in_specs += [pl.BlockSpec((None, sub, d), lambda i: (layer, midx(i), col_sc)),
                     pl.BlockSpec((None, sub, d), lambda i: (layer, midx(i), col_sh))]
        args += [mod, mod]
    if split_out:
        out_specs = [pl.BlockSpec((tm, d), prompt_blk), pl.BlockSpec((tm, d), sample_blk)]
        out_shape = [jax.ShapeDtypeStruct((n_p, d), out_dtype), jax.ShapeDtypeStruct((n_s, d), out_dtype)]
    else:
        out_specs = pl.BlockSpec((tm, d), whole_blk)
        out_shape = jax.ShapeDtypeStruct((n_p + n_s, d), out_dtype)
    n_out = 2 if split_out else 1
    blocks = (len(xs) + n_out) * _nbytes((tm, d), F32) + 2 * _nbytes((sub, d), F32)
    return pl.pallas_call(
        functools.partial(_norm_kernel, tm=tm, sub=sub, n_in=len(xs), n_out=n_out,
                          has_mod=mod is not None, npt=npt),
        grid=((n_p + n_s) // tm,),
        in_specs=in_specs,
        out_specs=out_specs,
        out_shape=out_shape,
        compiler_params=pltpu.CompilerParams(
            dimension_semantics=("arbitrary",), vmem_limit_bytes=_vmem_limit(blocks)),
        name="row_norm",
    )(*args)


def _load_weight(w_ref, copy_ref):
    w = w_ref[...]
    if copy_ref is not None:
        w = w.astype(BF16)
        copy_ref[...] = w
    return w


class _SideCast:
    def __init__(self, src3, layer, block, n_i, n_j):
        _, r, c = src3.shape
        br, tn = block
        ncb = c // tn
        n_blocks = (r // br) * ncb
        assert n_blocks <= n_i * n_j and r % br == 0 and c % tn == 0

        def blk(i, j):
            t = jnp.minimum(i * n_j + j, n_blocks - 1)
            return t // ncb, t % ncb

        self.arg = src3
        self.in_spec = pl.BlockSpec((None, br, tn), lambda i, j: (layer, *blk(i, j)))
        self.out_spec = pl.BlockSpec((None, br, tn), lambda i, j: (blk(i, j)[1], blk(i, j)[0], 0))
        self.out_shape = jax.ShapeDtypeStruct((ncb, r, tn), BF16)
        self.block_bytes = _nbytes(block, F32) + _nbytes(block, BF16)


def _ret_proj_kernel(h_ref, w_ref, cos_ref, sin_ref, *rest, tn, d_qk, head_dk, k_scale, cast_w, has_prev, n_side):
    n_in = (1 if has_prev else 0) + n_side
    o_ref = rest[n_in]
    for s in range(n_side):
        rest[len(rest) - n_side + s][...] = rest[n_in - n_side + s][...].astype(BF16)
    w = _load_weight(w_ref, rest[n_in + 1] if cast_w else None)
    acc = jnp.dot(h_ref[...], w, preferred_element_type=F32)
    col0 = pl.program_id(1) * tn
    is_qk = col0 < 2 * d_qk
    scale = jnp.where(jnp.logical_and(is_qk, col0 >= d_qk), k_scale, 1.0).astype(F32)
    cs = jnp.where(is_qk, cos_ref[...], 1.0) * scale
    sn = jnp.where(is_qk, sin_ref[...], 0.0) * scale
    half = head_dk // 2
    for hh in range(tn // head_dk):
        x1 = acc[:, hh * head_dk:hh * head_dk + half]
        x2 = acc[:, hh * head_dk + half:(hh + 1) * head_dk]
        o_ref[:, hh * head_dk:hh * head_dk + half] = (x1 * cs - x2 * sn).astype(o_ref.dtype)
        o_ref[:, hh * head_dk + half:(hh + 1) * head_dk] = (x1 * sn + x2 * cs).astype(o_ref.dtype)


def _ret_proj_call(h, w3, w_layer, lay, sides, rope_tables, tm=1024, tn=1024, tm_first=2048, tn_first=512):
    n, d = h.shape
    n_cols = w3.shape[2]
    d_qk = RET_HEADS * lay["dk"]
    half = lay["dk"] // 2

    per_tile = tn // tn_first

    def run(tm_, i0, n_i, tn_, weights, cast_w, prev, side_jobs):
        w_spec = (pl.BlockSpec((None, d, tn_), lambda i, j: (w_layer, 0, j)) if cast_w
                  else pl.BlockSpec((None, d, tn_), lambda i, j: (j, 0, 0)))
        in_specs = [pl.BlockSpec((tm_, d), lambda i, j: (i + i0, 0)), w_spec,
                    pl.BlockSpec((tm_, half), lambda i, j: (i + i0, 0)),
                    pl.BlockSpec((tm_, half), lambda i, j: (i + i0, 0))]
        args = [h, weights, *rope_tables]
        aliases = {}
        if prev is not None:
            in_specs.append(pl.BlockSpec(memory_space=pl.ANY))
            args.append(prev)
            aliases = {len(args) - 1: 0}
        out_specs = [pl.BlockSpec((tm_, tn_), lambda i, j: (i + i0, j))]
        out_shape = [jax.ShapeDtypeStruct((n, n_cols), BF16)]
        w_bytes = _nbytes((d, tn_), BF16)
        if cast_w:
            out_specs.append(pl.BlockSpec((None, d, tn_), lambda i, j: (j // per_tile, 0, j % per_tile)))
            out_shape.append(jax.ShapeDtypeStruct((n_cols // tn, d, tn), BF16))
            w_bytes = _nbytes((d, tn_), F32) + 2 * _nbytes((d, tn_), BF16)
        casts = [_SideCast(*job, n_i, n_cols // tn_) for job in side_jobs]
        in_specs += [sc.in_spec for sc in casts]
        args += [sc.arg for sc in casts]
        out_specs += [sc.out_spec for sc in casts]
        out_shape += [sc.out_shape for sc in casts]
        w_bytes += sum(sc.block_bytes for sc in casts)
        blocks = _nbytes((tm_, d), BF16) + w_bytes + 2 * _nbytes((tm_, half), F32) + _nbytes((tm_, tn_), BF16)
        return pl.pallas_call(
            functools.partial(_ret_proj_kernel, tn=tn_, d_qk=d_qk, head_dk=lay["dk"],
                              k_scale=float(lay["dk"]) ** -0.5, cast_w=cast_w,
                              has_prev=prev is not None, n_side=len(casts)),
            grid=(n_i, n_cols // tn_),
            in_specs=in_specs,
            out_specs=out_specs,
            out_shape=out_shape,
            input_output_aliases=aliases,
            compiler_params=pltpu.CompilerParams(
                dimension_semantics=("arbitrary", "arbitrary"),
                vmem_limit_bytes=_vmem_limit(blocks, _nbytes((tm_, tn_), F32))),
            name="ret_in_proj" + ("_first" if cast_w else ""),
        )(*args)

    assert tm_first % tm == 0 and (n - tm_first) % tm == 0
    out0, w_bf = run(tm_first, 0, 1, tn_first, w3, True, None, [])
    out, *side_w = run(tm, tm_first // tm, (n - tm_first) // tm, tn, w_bf, False, out0, sides)
    return out, side_w


def _ffn_in_kernel(x_ref, gam_ref, sc_ref, sh_ref, wg_ref, wu_ref, *rest, tm, sub, has_side):
    if has_side:
        side_in, o_ref, side_out, h_ref = rest
        side_out[...] = side_in[...].astype(BF16)
    else:
        o_ref, h_ref = rest

    @pl.when(pl.program_id(1) == 0)
    def _():
        _norm_mod_rows(x_ref, gam_ref, sc_ref, sh_ref, h_ref, tm=tm, sub=sub)

    h = h_ref[...]
    gate = jnp.dot(h, wg_ref[...], preferred_element_type=F32)
    up = jnp.dot(h, wu_ref[...], preferred_element_type=F32)
    o_ref[...] = (_silu(gate) * up).astype(o_ref.dtype)


def _ffn_in_call(x, gam, mod, layer, w, lay, side=None, tm=1024):
    n, d = x.shape
    sub = lay["sub"]
    tn = w.shape[2]
    n_j = w.shape[0] // 2
    d_ff = n_j * tn
    _, midx = _tiling(lay, tm)
    in_specs = [
        pl.BlockSpec((tm, d), lambda i, j: (i, 0)),
        pl.BlockSpec((1, d), lambda i, j: (0, 0)),
        pl.BlockSpec((None, sub, d), lambda i, j: (layer, midx(i), 4)),
        pl.BlockSpec((None, sub, d), lambda i, j: (layer, midx(i), 3)),
        pl.BlockSpec((None, d, tn), lambda i, j: (j, 0, 0)),
        pl.BlockSpec((None, d, tn), lambda i, j: (j + n_j, 0, 0)),
    ]
    args = [x, gam.reshape(1, d), mod, mod, w, w]
    out_specs = [pl.BlockSpec((tm, tn), lambda i, j: (i, j))]
    out_shape = [jax.ShapeDtypeStruct((n, d_ff), BF16)]
    blocks = (_nbytes((tm, d), F32) + 2 * _nbytes((sub, d), F32) + 2 * _nbytes((d, tn), BF16)
              + _nbytes((tm, tn), BF16))
    if side is not None:
        sc = _SideCast(*side, n // tm, n_j)
        in_specs.append(sc.in_spec)
        args.append(sc.arg)
        out_specs.append(sc.out_spec)
        out_shape.append(sc.out_shape)
        blocks += sc.block_bytes
    res = pl.pallas_call(
        functools.partial(_ffn_in_kernel, tm=tm, sub=sub, has_side=side is not None),
        grid=(n // tm, n_j),
        in_specs=in_specs,
        out_specs=out_specs,
        out_shape=out_shape,
        scratch_shapes=[pltpu.VMEM((tm, d), BF16)],
        compiler_params=pltpu.CompilerParams(
            dimension_semantics=("arbitrary", "arbitrary"),
            vmem_limit_bytes=_vmem_limit(blocks, _nbytes((tm, d), BF16) + 2 * _nbytes((tm, tn), F32))),
        name="ffn_in_swiglu",
    )(*args)
    return res[0], (res[1] if side is not None else None)


def _out_proj_kernel(*refs, tm, sub, npt, n_src, has_side, emit_ssq):
    a_refs, w_ref = refs[:n_src], refs[n_src]
    x_refs, gate_ref = refs[n_src + 1:2 * n_src + 1], refs[2 * n_src + 1]
    rest = list(refs[2 * n_src + 2:])
    ssq_ref = rest.pop() if emit_ssq else None
    if has_side:
        side_in, o_ref, side_out = rest
        side_out[...] = side_in[...].astype(BF16)
    else:
        (o_ref,) = rest

    def body(a_ref, x_ref):
        acc = jnp.dot(a_ref[...], w_ref[...], preferred_element_type=F32)
        gate = gate_ref[...]
        for s in range(tm // sub):
            rows = slice(s * sub, (s + 1) * sub)
            x_new = x_ref[rows, :] + gate * acc[rows, :]
            o_ref[rows, :] = x_new
            if emit_ssq:
                ssq_ref[rows, :] = jnp.sum(x_new * x_new, axis=-1, keepdims=True)

    if n_src == 1:
        body(a_refs[0], x_refs[0])
    else:
        is_prompt = pl.program_id(0) < npt
        pl.when(is_prompt)(functools.partial(body, a_refs[0], x_refs[0]))
        pl.when(jnp.logical_not(is_prompt))(functools.partial(body, a_refs[1], x_refs[1]))


def _out_proj_call(srcs, w, mod, layer, gate_col, lay, side=None, emit_ssq=False, tm=1024):
    n = sum(a.shape[0] for a, _ in srcs)
    k = srcs[0][0].shape[1]
    n_j, _, tn = w.shape
    d = n_j * tn
    sub = lay["sub"]
    npt, midx = _tiling(lay, tm)
    if len(srcs) == 1:
        a_specs = [pl.BlockSpec((tm, k), lambda i, j: (i, 0))]
        x_specs = [pl.BlockSpec((tm, tn), lambda i, j: (i, j))]
        a_bytes = 2 * _nbytes((tm, k), BF16)
    else:
        a_specs = [pl.BlockSpec((tm, k), lambda i, j: (jnp.minimum(i, npt - 1), 0)),
                   pl.BlockSpec((tm, k), lambda i, j: (jnp.maximum(i - npt, 0), 0), pipeline_mode=pl.Buffered(1))]
        x_specs = [pl.BlockSpec((tm, tn), lambda i, j: (jnp.minimum(i, npt - 1), j)),
                   pl.BlockSpec((tm, tn), lambda i, j: (jnp.maximum(i - npt, 0), j))]
        a_bytes = 3 * _nbytes((tm, k), BF16)
    in_specs = (a_specs + [pl.BlockSpec((None, k, tn), lambda i, j: (j, 0, 0))] + x_specs
                + [pl.BlockSpec((None, sub, tn), lambda i, j: (layer, midx(i), gate_col * n_j + j))])
    args = [a for a, _ in srcs] + [w] + [x for _, x in srcs] + [mod]
    out_specs = [pl.BlockSpec((tm, tn), lambda i, j: (i, j))]
    out_shape = [jax.ShapeDtypeStruct((n, d), F32)]
    blocks = (a_bytes // 2 + _nbytes((k, tn), BF16) + (1 + len(srcs)) * _nbytes((tm, tn), F32)
              + _nbytes((sub, tn), F32))
    if side is not None:
        sc = _SideCast(*side, n // tm, n_j)
        in_specs.append(sc.in_spec)
        args.append(sc.arg)
        out_specs.append(sc.out_spec)
        out_shape.append(sc.out_shape)
        blocks += sc.block_bytes
    if emit_ssq:
        out_specs.append(pl.BlockSpec((None, tm, 1), lambda i, j: (j, i, 0)))
        out_shape.append(jax.ShapeDtypeStruct((n_j, n, 1), F32))
        blocks += _nbytes((tm, V7X_LANES), F32)
    res = pl.pallas_call(
        functools.partial(_out_proj_kernel, tm=tm, sub=sub, npt=npt, n_src=len(srcs), has_side=side is not None,
                          emit_ssq=emit_ssq),
        grid=(n // tm, n_j),
        in_specs=in_specs,
        out_specs=out_specs,
        out_shape=out_shape,
        compiler_params=pltpu.CompilerParams(
            dimension_semantics=("arbitrary", "arbitrary"),
            vmem_limit_bytes=_vmem_limit(blocks, _nbytes((tm, tn), F32))),
        name="out_proj_residual",
    )(*args)
    return res[0], (res[1] if side is not None else None), (res[-1] if emit_ssq else None)


def _retention_decays(c):
    lg = jnp.log1p(-jnp.exp2(-5.0 - jnp.arange(RET_HEADS, dtype=F32)))
    idx = jnp.arange(c, dtype=F32)
    diff = idx[:, None] - idx[None, :]
    causal = diff >= 0
    dmask = jnp.where(causal[None], jnp.exp(jnp.where(causal, diff, 0.0)[None] * lg[:, None, None]), 0.0)
    qdec = jnp.exp((idx + 1.0)[None, :] * lg[:, None])
    kdec = jnp.exp((c - 1.0 - idx)[None, :] * lg[:, None])
    sdec = jnp.exp(c * lg)
    return dmask, qdec[:, :, None], kdec[:, :, None], sdec[:, None, None]


def _group_norm_gate(o, g, gng):
    mu = jnp.mean(o, axis=-1, keepdims=True)
    dlt = o - mu
    var = jnp.mean(dlt * dlt, axis=-1, keepdims=True)
    return _silu(g) * (dlt * lax.rsqrt(var + EPS) * gng)


_NT = (((1,), (1,)), ((), ()))
_TN = (((0,), (0,)), ((), ()))


def _retention_chunk(q, k, v, s_prev, dmask, qdec, kdec, sdec):
    scores = lax.dot_general(q, k, _NT, preferred_element_type=F32) * dmask
    o = (jnp.dot(scores.astype(BF16), v, preferred_element_type=F32)
         + qdec * jnp.dot(q, s_prev.astype(BF16), preferred_element_type=F32))
    kd = (k.astype(F32) * kdec).astype(BF16)
    s_new = sdec * s_prev + lax.dot_general(kd, v, _TN, preferred_element_type=F32)
    return o, s_new


def _retention_kernel(pq_ref, pk_ref, pv_ref, pg_ref, sq_ref, sk_ref, sv_ref, sg_ref, s0_ref, gng_ref,
                      pdmask_ref, pqdec_ref, pkdec_ref, psdec_ref, sdmask_ref, sqdec_ref, skdec_ref, ssdec_ref,
                      po_ref, ps_out_ref, so_ref, ss_out_ref,
                      ps_ref, qf_ref, kf_ref, vf_ref, of_ref, *, chunk, n_chunks, bb, n_heads):
    step = pl.program_id(0)
    hp = (step // 2) % n_heads
    hs = step % n_heads
    @pl.when(pl.program_id(0) % 2 == 0)
    def _():
        ps_ref[...] = jnp.zeros_like(ps_ref)

    pdmask, pqdec, pkdec, psdec, pgng = pdmask_ref[hp], pqdec_ref[hp], pkdec_ref[hp], psdec_ref[hp], gng_ref[hp]

    def prompt_body(c, carry):
        rows = pl.ds(pl.multiple_of(c * chunk, chunk), chunk)
        o, s_new = _retention_chunk(pq_ref[rows, :], pk_ref[rows, :], pv_ref[rows, :], ps_ref[...],
                                    pdmask, pqdec, pkdec, psdec)
        ps_ref[...] = s_new
        po_ref[rows, :] = _group_norm_gate(o, pg_ref[rows, :].astype(F32), pgng).astype(po_ref.dtype)
        return carry

    lax.fori_loop(0, n_chunks, prompt_body, 0, unroll=2)
    ps_out_ref[0, 0] = ps_ref[...]

    smask, sqdec, skdec, ssdec, sgng = sdmask_ref[hs], sqdec_ref[hs], skdec_ref[hs], ssdec_ref[hs], gng_ref[hs]
    ts, _, dk = sq_ref.shape
    dv = sv_ref.shape[2]
    rows = ts * bb
    scores = lax.dot_general(sq_ref[...].reshape(rows, dk), sk_ref[...].reshape(rows, dk), _NT,
                             preferred_element_type=F32) * smask
    o_intra = jnp.dot(scores.astype(BF16), sv_ref[...].reshape(rows, dv), preferred_element_type=F32)
    qf_ref[...] = sq_ref[...].astype(F32)
    kf_ref[...] = sk_ref[...].astype(F32)
    vf_ref[...] = sv_ref[...].astype(F32)

    def sample_body(b, carry):
        s0 = s0_ref[b, 0]
        of_ref[:, b, :] = jnp.dot(qf_ref[:, b, :].astype(BF16), s0.astype(BF16), preferred_element_type=F32)
        kd = (kf_ref[:, b, :] * skdec).astype(BF16)
        ss_out_ref[b, 0] = ssdec * s0 + lax.dot_general(kd, vf_ref[:, b, :].astype(BF16), _TN,
                                                        preferred_element_type=F32)
        return carry

    lax.fori_loop(0, bb, sample_body, 0, unroll=4)
    o = o_intra + sqdec * of_ref[...].reshape(rows, dv)
    gated = _group_norm_gate(o, sg_ref[...].reshape(rows, dv).astype(F32), sgng)
    so_ref[...] = gated.astype(so_ref.dtype).reshape(ts, bb, dv)


def _retention_call(qkvg, gng, state, lay, chunk=2 * RET_CHUNK, bb=16):
    n = qkvg.shape[0]
    bp, tp, bs, ts, dk, dv = lay["n_prompt"], lay["t_prompt"], lay["sub"], lay["t_sample"], lay["dk"], lay["dv"]
    h = RET_HEADS
    n_p = lay["n_prompt_rows"]
    half = tp // 2
    n_steps = 2 * bp * h
    assert (bs // bb) * h == n_steps and half % chunk == 0
    t_blk = n_p // (bs * ts)
    assert t_blk * bs * ts == n_p
    v_blk0 = 2 * h * dk // dv
    g_blk0 = v_blk0 + h
    pdec = _retention_decays(chunk)
    dmask_s, qdec_s, kdec_s, sdec_s = _retention_decays(ts)
    same_seq = jnp.eye(bb, dtype=F32)[None, None, :, None, :]
    block_mask = (dmask_s[:, :, None, :, None] * same_seq).reshape(h, ts * bb, ts * bb)
    qdec_rows = jnp.broadcast_to(qdec_s[:, :, None, :], (h, ts, bb, 1)).reshape(h, ts * bb, 1)
    sdec = (block_mask, qdec_rows, kdec_s, sdec_s)
    qkv3 = qkvg.reshape(n // bs, bs, qkvg.shape[1])
    gng3 = gng.reshape(h, 1, dv)

    p_row = lambda s: (s // 2 // h) * 2 + s % 2
    p_head = lambda s: (s // 2) % h
    s_blk = lambda s: s // h
    s_head = lambda s: s % h
    consts = [gng3, *pdec, *sdec]
    resident = lambda a: pl.BlockSpec(a.shape, lambda s: (0,) * a.ndim, pipeline_mode=pl.Buffered(1))
    const_bytes = sum(_nbytes(a.shape[:-1] + (max(a.shape[-1], V7X_LANES),), F32) for a in consts)
    in_specs = [
        pl.BlockSpec((half, dk), lambda s: (p_row(s), p_head(s))),
        pl.BlockSpec((half, dk), lambda s: (p_row(s), h + p_head(s))),
        pl.BlockSpec((half, dv), lambda s: (p_row(s), v_blk0 + p_head(s))),
        pl.BlockSpec((half, dv), lambda s: (p_row(s), g_blk0 + p_head(s))),
        pl.BlockSpec((ts, bb, dk), lambda s: (t_blk, s_blk(s), s_head(s))),
        pl.BlockSpec((ts, bb, dk), lambda s: (t_blk, s_blk(s), h + s_head(s))),
        pl.BlockSpec((ts, bb, dv), lambda s: (t_blk, s_blk(s), v_blk0 + s_head(s))),
        pl.BlockSpec((ts, bb, dv), lambda s: (t_blk, s_blk(s), g_blk0 + s_head(s))),
        pl.BlockSpec((bb, 1, dk, dv), lambda s: (s_blk(s), s_head(s), 0, 0)),
        *[resident(a) for a in consts],
    ]
    out_specs = [
        pl.BlockSpec((half, dv), lambda s: (p_row(s), p_head(s))),
        pl.BlockSpec((1, 1, dk, dv), lambda s: (s // 2 // h, p_head(s), 0, 0)),
        pl.BlockSpec((ts, bb, dv), lambda s: (0, s_blk(s), s_head(s))),
        pl.BlockSpec((bb, 1, dk, dv), lambda s: (s_blk(s), s_head(s), 0, 0)),
    ]
    out_shape = [jax.ShapeDtypeStruct((n_p, h * dv), BF16),
                 jax.ShapeDtypeStruct((bp, h, dk, dv), F32),
                 jax.ShapeDtypeStruct((ts, bs, h * dv), BF16),
                 jax.ShapeDtypeStruct((bs, h, dk, dv), F32)]
    blocks = (2 * _nbytes((half, dk), BF16) + 3 * _nbytes((half, dv), BF16) + _nbytes((dk, dv), F32)
              + _nbytes((chunk, chunk), F32)
              + 2 * _nbytes((ts, bb, dk), BF16) + 3 * _nbytes((ts, bb, dv), BF16) + 2 * _nbytes((bb, dk, dv), F32))
    scratch = _nbytes((dk, dv), F32) + 2 * _nbytes((ts, bb, dk), F32) + 2 * _nbytes((ts, bb, dv), F32)
    gated_p, s_p, gated_s, s_s = pl.pallas_call(
        functools.partial(_retention_kernel, chunk=chunk, n_chunks=half // chunk, bb=bb, n_heads=h),
        grid=(n_steps,),
        in_specs=in_specs,
        out_specs=out_specs,
        out_shape=out_shape,
        scratch_shapes=[pltpu.VMEM((dk, dv), F32),
                        pltpu.VMEM((ts, bb, dk), F32), pltpu.VMEM((ts, bb, dk), F32),
                        pltpu.VMEM((ts, bb, dv), F32), pltpu.VMEM((ts, bb, dv), F32)],
        compiler_params=pltpu.CompilerParams(
            dimension_semantics=("arbitrary",),
            vmem_limit_bytes=_vmem_limit(blocks, scratch + const_bytes)),
        name="retention",
    )(qkvg, qkvg, qkvg, qkvg, qkv3, qkv3, qkv3, qkv3, state, *consts)
    return gated_p, gated_s.reshape(bs * ts, h * dv), s_p, s_s


def _window_sum_rows(win_ref, halo, r0, rows, win, stride):
    if stride % V7X_SUBLANES == 0:
        acc = win_ref[pl.ds(halo + r0, rows), :]
        for k in range(1, win):
            acc = acc + win_ref[pl.ds(halo - k * stride + r0, rows), :]
        return acc
    assert stride == 1 and halo == 16 and win <= 16
    blk = win_ref[pl.ds(r0, rows + 16), :]
    shift = 1
    while shift < win:
        blk = blk + pltpu.roll(blk, shift, 0)
        shift *= 2
    return blk[16:, :]


def _pool_kernel(ssq_ref, gam_ref, sc_ref, sh_ref, hist_ref, x_ref, gate_ref, w_ref, scale_ref, *rest,
                 tm, sub, stride, tiles_per_seq, hist_len, chunk_rows, d_model):
    o_ref, hist_out_ref, win_ref, m_ref = rest[-4:]
    g = pl.program_id(0)
    i = pl.program_id(1)
    halo = 16 * stride
    tile_in_seq = i % tiles_per_seq

    if tiles_per_seq > 1:
        @pl.when(tile_in_seq != 0)
        def _():
            win_ref[0:halo, :] = win_ref[tm:tm + halo, :]

    @pl.when(tile_in_seq == 0)
    def _():
        win_ref[0:halo, :] = hist_ref[0]

    gam = gam_ref[...]
    for s in range(tm // sub):
        rows = slice(s * sub, (s + 1) * sub)
        ssq = ssq_ref[0, rows, :]
        for c in range(1, ssq_ref.shape[0]):
            ssq = ssq + ssq_ref[c, rows, :]
        rs = lax.rsqrt(ssq / d_model + EPS)
        win_ref[halo + s * sub:halo + (s + 1) * sub, :] = (
            ((x_ref[rows, :] * rs) * gam) * (1.0 + sc_ref[...]) + sh_ref[...])

    for gi, win in enumerate(POOL_WINDOWS):
        @pl.when(g == gi)
        def _(win=win):
            def body(c, carry):
                r0 = pl.multiple_of(c * chunk_rows, chunk_rows)
                cur = win_ref[pl.ds(halo + r0, chunk_rows), :]
                acc = _window_sum_rows(win_ref, halo, r0, chunk_rows, win, stride)
                t = (tile_in_seq * tm + r0 + lax.broadcasted_iota(jnp.int32, (chunk_rows, 1), 0)) // stride
                cnt = jnp.minimum(win, t + 1 + hist_len).astype(F32)
                m_ref[pl.ds(r0, chunk_rows), :] = (acc / cnt - cur).astype(m_ref.dtype)
                return carry

            lax.fori_loop(0, tm // chunk_rows, body, 0)

    y = jnp.dot(m_ref[...], w_ref[0], preferred_element_type=F32) * scale_ref[...]
    gate = gate_ref[...]
    for s in range(tm // sub):
        rows = slice(s * sub, (s + 1) * sub)
        o_ref[rows, :] = x_ref[rows, :] + gate * y[rows, :]
    hist_out_ref[0] = win_ref[tm + stride:tm + halo, :]


def _pool_call(ssq, gam, hist, x, mod, layer, w, scale, x_prev, *, row0, n_rows, n_seq, stride, hist_len, mod_seq0,
               lay, tm=1024):
    n, d = x.shape
    sub = lay["sub"]
    n_g = len(POOL_WINDOWS)
    gc = d // n_g
    n_tiles = n_rows // tm
    tps = n_tiles // n_seq
    blk0 = row0 // tm
    halo = 16 * stride
    n_parts = ssq.shape[0]
    blocks = (2 * _nbytes((tm, gc), F32) + _nbytes((halo, gc), F32) + 3 * _nbytes((sub, gc), F32)
              + _nbytes((gc, gc), BF16) + _nbytes((15 * stride, gc), F32) + n_parts * _nbytes((tm, V7X_LANES), F32))
    scratch = _nbytes((halo + tm, gc), F32) + _nbytes((tm, gc), BF16)
    sh_col0, sc_col0, gate_col0 = 0, n_g, 2 * n_g
    mod_blk = lambda col0: pl.BlockSpec((None, sub, gc), lambda g, i: (layer, mod_seq0 + i // tps, col0 + g))
    in_specs = [
        pl.BlockSpec((n_parts, tm, 1), lambda g, i: (0, blk0 + i, 0)),
        pl.BlockSpec((1, gc), lambda g, i: (0, g)),
        mod_blk(sc_col0),
        mod_blk(sh_col0),
        pl.BlockSpec((1, halo, gc), lambda g, i: (i // tps, 0, g)),
        pl.BlockSpec((tm, gc), lambda g, i: (blk0 + i, g)),
        mod_blk(gate_col0),
        pl.BlockSpec((1, gc, gc), lambda g, i: (g, 0, 0)),
        pl.BlockSpec((1, gc), lambda g, i: (0, g)),
    ]
    args = [ssq, gam.reshape(1, d), mod, mod, hist, x, mod, w, scale.reshape(1, d)]
    aliases = {}
    if x_prev is not None:
        in_specs.append(pl.BlockSpec(memory_space=pl.ANY))
        args.append(x_prev)
        aliases = {len(args) - 1: 0}
    return pl.pallas_call(
        functools.partial(_pool_kernel, tm=tm, sub=sub, stride=stride, tiles_per_seq=tps,
                          hist_len=hist_len, chunk_rows=64, d_model=d),
        grid=(n_g, n_tiles),
        in_specs=in_specs,
        out_specs=[
            pl.BlockSpec((tm, gc), lambda g, i: (blk0 + i, g)),
            pl.BlockSpec((1, 15 * stride, gc), lambda g, i: (i // tps, 0, g)),
        ],
        out_shape=[jax.ShapeDtypeStruct((n, d), F32),
                   jax.ShapeDtypeStruct((n_seq, 15 * stride, d), F32)],
        scratch_shapes=[pltpu.VMEM((halo + tm, gc), F32), pltpu.VMEM((tm, gc), BF16)],
        input_output_aliases=aliases,
        compiler_params=pltpu.CompilerParams(
            dimension_semantics=("arbitrary", "arbitrary"),
            vmem_limit_bytes=_vmem_limit(blocks, scratch)),
        name="pool_mixer_s%d" % stride,
    )(*args)


def _rope_tables(pos, dk):
    half = dk // 2
    inv = 1.0 / (ROPE_BASE ** (jnp.arange(half, dtype=F32) / half))
    ang = pos[:, None] * inv[None, :]
    return jnp.cos(ang), jnp.sin(ang)


def kernel(x_prompt, x_sample, c_prompt, c_sample, state_ret, state_pool, norm_mix_g, norm_ffn_g, ada_w, ada_b,
           ret_w_in, ret_gn_g, ret_w_out, pool_w, pool_scale, ffn_w_in, ffn_w_out, final_norm_g):
    bp, tp, d = x_prompt.shape
    bs, ts, _ = x_sample.shape
    depth = ada_w.shape[0]
    dk = state_ret.shape[3]
    dv = state_ret.shape[4]
    n_hist = state_pool.shape[2]
    assert bp <= C_PAD and n_hist == max(POOL_WINDOWS) - 1 and depth == 2
    n_p = bp * tp
    n_s = bs * ts
    lay = dict(n_prompt=bp, t_prompt=tp, n_prompt_rows=n_p, n_sample_rows=n_s, sub=bs, t_sample=ts, dk=dk, dv=dv)

    xp = x_prompt.reshape(n_p, d)
    xs = x_sample.transpose(1, 0, 2).reshape(n_s, d)
    c_all = jnp.concatenate([c_prompt, jnp.zeros((C_PAD - bp, d), F32), c_sample], axis=0)
    mod = _ada_call(c_all, ada_w, ada_b, bp, bs)

    cos_p, sin_p = _rope_tables(jnp.arange(tp, dtype=F32), dk)
    cos_s, sin_s = _rope_tables(float(PAST_LEN) + jnp.arange(ts, dtype=F32), dk)
    cos = jnp.concatenate([jnp.tile(cos_p, (bp, 1)), jnp.repeat(cos_s, bs, axis=0)], axis=0)
    sin = jnp.concatenate([jnp.tile(sin_p, (bp, 1)), jnp.repeat(sin_s, bs, axis=0)], axis=0)

    h0 = _norm_call((xp, xs), norm_mix_g[0], mod, 0, 1, 0, lay, BF16)
    tn_ffn = 512
    d_ff = ffn_w_out.shape[1]
    cast_in = (d // 2, tn_ffn)
    cast_out = (d_ff // 8, tn_ffn)
    qkvg, (w_ret_out, w_ffn_in0) = _ret_proj_call(
        h0, ret_w_in, 0, lay, [(ret_w_out, 0, (ret_w_out.shape[1] // 8, tn_ffn)), (ffn_w_in, 0, cast_in)], (cos, sin))
    gated_p, gated_s, s_ret_p, s_ret_s = _retention_call(qkvg, ret_gn_g[0], state_ret[0], lay)
    x, w_ffn_out0, _ = _out_proj_call([(gated_p, xp), (gated_s, xs)], w_ret_out, mod, 0, 2, lay,
                                      side=(ffn_w_out, 0, cast_out))
    hmid, w_ffn_in1 = _ffn_in_call(x, norm_ffn_g[0], mod, 0, w_ffn_in0, lay, side=(ffn_w_in, 1, cast_in))
    x, w_ffn_out1, ssq = _out_proj_call([(hmid, x)], w_ffn_out0, mod, 0, 5, lay, side=(ffn_w_out, 1, cast_out),
                                        emit_ssq=True)

    pw = pool_w[0].astype(BF16)
    hist_p = jnp.zeros((bp, 16, d), F32)
    hist_s = jnp.concatenate([jnp.zeros((1, bs, d), F32), state_pool[0].transpose(1, 0, 2)], axis=0)
    x1, nh_p = _pool_call(ssq, norm_mix_g[1], hist_p, x, mod, 1, pw, pool_scale[0], None, row0=0, n_rows=n_p,
                          n_seq=bp, stride=1, hist_len=0, mod_seq0=0, lay=lay)
    x, nh_s = _pool_call(ssq, norm_mix_g[1], hist_s.reshape(1, 16 * bs, d), x, mod, 1, pw, pool_scale[0], x1,
                         row0=n_p, n_rows=n_s, n_seq=1, stride=bs, hist_len=min(PAST_LEN, n_hist), mod_seq0=bp,
                         lay=lay)
    hmid, _ = _ffn_in_call(x, norm_ffn_g[1], mod, 1, w_ffn_in1, lay)
    x, _, _ = _out_proj_call([(hmid, x)], w_ffn_out1, mod, 1, 5, lay)

    y_p, y_s = _norm_call((x,), final_norm_g, None, 0, 0, 0, lay, F32, split_out=True)
    y_prompt = y_p.reshape(bp, tp, d)
    y_sample = y_s.reshape(ts, bs, d).transpose(1, 0, 2)
    state_pool_sample = nh_s.reshape(n_hist, bs, d).transpose(1, 0, 2)
    return (y_prompt, y_sample, s_ret_p[None], nh_p[None], s_ret_s[None], state_pool_sample[None])
```

```python
import functools

import jax
import jax.numpy as jnp
from jax import lax
from jax.experimental import pallas as pl
from jax.experimental.pallas import tpu as pltpu

F32 = jnp.float32
BF16 = jnp.bfloat16

RET_HEADS = 8
RET_CHUNK = 128
ROPE_BASE = 10000.0
POOL_WINDOWS = (2, 4, 8, 16)
EPS = 1e-6
PAST_LEN = 16384

V7X_VMEM_BYTES = 64 * 1024 * 1024
V7X_LANES = 128
V7X_SUBLANES = 8
NORM_ROWS = 16
C_PAD = 8


def _vmem_limit(block_bytes, scratch_bytes=0):
    need = 2 * block_bytes + scratch_bytes + 12 * 1024 * 1024
    return int(min(need, V7X_VMEM_BYTES - 6 * 1024 * 1024))


def _nbytes(shape, dtype):
    n = 1
    for s in shape:
        n *= s
    return n * jnp.dtype(dtype).itemsize


def _silu(x):
    return x / (1.0 + jnp.exp(-x))


def _ada_kernel(c_ref, w_ref, b_ref, o_ref, *, n_prompt, sub):
    cs = _silu(c_ref[...])
    res = jnp.dot(cs.astype(BF16), w_ref[0].astype(BF16), preferred_element_type=F32) + b_ref[0]
    tn = res.shape[-1]
    for b in range(n_prompt):
        o_ref[0, b * sub:(b + 1) * sub, :] = jnp.broadcast_to(res[b:b + 1, :], (sub, tn))
    o_ref[0, n_prompt * sub:, :] = res[C_PAD:C_PAD + sub, :]


def _ada_call(c_all, ada_w, ada_b, n_prompt, sub, tn=1024):
    depth, d, n6 = ada_w.shape
    rows = c_all.shape[0]
    out_rows = (n_prompt + 1) * sub
    blocks = (_nbytes((rows, d), F32) + _nbytes((d, tn), F32) + _nbytes((1, tn), F32)
              + _nbytes((out_rows, tn), F32))
    return pl.pallas_call(
        functools.partial(_ada_kernel, n_prompt=n_prompt, sub=sub),
        grid=(depth, n6 // tn),
        in_specs=[
            pl.BlockSpec((rows, d), lambda l, j: (0, 0)),
            pl.BlockSpec((1, d, tn), lambda l, j: (l, 0, j)),
            pl.BlockSpec((1, 1, tn), lambda l, j: (l, 0, j)),
        ],
        out_specs=pl.BlockSpec((1, out_rows, tn), lambda l, j: (l, 0, j)),
        out_shape=jax.ShapeDtypeStruct((depth, out_rows, n6), F32),
        compiler_params=pltpu.CompilerParams(
            dimension_semantics=("arbitrary", "arbitrary"),
            vmem_limit_bytes=_vmem_limit(blocks, _nbytes((d, tn), BF16))),
        name="ada_mod",
    )(c_all, ada_w, ada_b.reshape(depth, 1, n6))


def _norm_mod_rows(x_ref, gam_ref, sc_ref, sh_ref, out_ref, *, tm, sub):
    gam = gam_ref[...]

    def body(m, carry):
        m0 = pl.multiple_of(m * NORM_ROWS, NORM_ROWS)
        scale = gam
        if sc_ref is not None:
            scale = gam * (1.0 + sc_ref[pl.ds(m0, NORM_ROWS), :])
            shift = sh_ref[pl.ds(m0, NORM_ROWS), :]
        for s in range(tm // sub):
            rows = pl.ds(pl.multiple_of(s * sub + m0, NORM_ROWS), NORM_ROWS)
            x = x_ref[rows, :]
            hn = (x * lax.rsqrt(jnp.mean(x * x, axis=-1, keepdims=True) + EPS)) * scale
            if sc_ref is not None:
                hn = hn + shift
            out_ref[rows, :] = hn.astype(out_ref.dtype)
        return carry

    lax.fori_loop(0, sub // NORM_ROWS, body, 0)


def _mod_index(i, n_prompt_tiles, tiles_per_seq, n_prompt):
    return jnp.where(i < n_prompt_tiles, i // tiles_per_seq, n_prompt)


def _tiling(lay, tm):
    npt = lay["n_prompt_rows"] // tm
    midx = functools.partial(_mod_index, n_prompt_tiles=npt, tiles_per_seq=lay["t_prompt"] // tm,
                             n_prompt=lay["n_prompt"])
    return npt, midx


def _norm_kernel(*refs, tm, sub, n_in, n_out, has_mod, npt):
    x_refs = refs[:n_in]
    gam_ref = refs[n_in]
    sc_ref, sh_ref = (refs[n_in + 1], refs[n_in + 2]) if has_mod else (None, None)
    o_refs = refs[len(refs) - n_out:]
    if n_in == 1 and n_out == 1:
        _norm_mod_rows(x_refs[0], gam_ref, sc_ref, sh_ref, o_refs[0], tm=tm, sub=sub)
        return
    i = pl.program_id(0)

    @pl.when(i < npt)
    def _():
        _norm_mod_rows(x_refs[0], gam_ref, sc_ref, sh_ref, o_refs[0], tm=tm, sub=sub)

    @pl.when(i >= npt)
    def _():
        _norm_mod_rows(x_refs[-1], gam_ref, sc_ref, sh_ref, o_refs[-1], tm=tm, sub=sub)


def _norm_call(xs, gam, mod, layer, col_sc, col_sh, lay, out_dtype, split_out=False, tm=1024):
    d = xs[0].shape[1]
    sub = lay["sub"]
    n_p, n_s = lay["n_prompt_rows"], lay["n_sample_rows"]
    npt, midx = _tiling(lay, tm)
    prompt_blk = lambda i: (jnp.minimum(i, npt - 1), 0)
    sample_blk = lambda i: (jnp.maximum(i - npt, 0), 0)
    whole_blk = lambda i: (i, 0)
    in_specs = ([pl.BlockSpec((tm, d), whole_blk)] if len(xs) == 1
                else [pl.BlockSpec((tm, d), prompt_blk), pl.BlockSpec((tm, d), sample_blk)])
    in_specs.append(pl.BlockSpec((1, d), lambda i: (0, 0)))
    args = list(xs) + [gam.reshape(1, d)]
    if mod is not None:
        in_specs += [pl.BlockSpec((None, sub, d), lambda i: (layer, midx(i), col_sc)),
                     pl.BlockSpec((None, sub, d), lambda i: (layer, midx(i), col_sh))]
        args += [mod, mod]
    if split_out:
        out_specs = [pl.BlockSpec((tm, d), prompt_blk), pl.BlockSpec((tm, d), sample_blk)]
        out_shape = [jax.ShapeDtypeStruct((n_p, d), out_dtype), jax.ShapeDtypeStruct((n_s, d), out_dtype)]
    else:
        out_specs = pl.BlockSpec((tm, d), whole_blk)
        out_shape = jax.ShapeDtypeStruct((n_p + n_s, d), out_dtype)
    n_out = 2 if split_out else 1
    blocks = (len(xs) + n_out) * _nbytes((tm, d), F32) + 2 * _nbytes((sub, d), F32)
    return pl.pallas_call(
        functools.partial(_norm_kernel, tm=tm, sub=sub, n_in=len(xs), n_out=n_out,
                          has_mod=mod is not None, npt=npt),
        grid=((n_p + n_s) // tm,),
        in_specs=in_specs,
        out_specs=out_specs,
        out_shape=out_shape,
        compiler_params=pltpu.CompilerParams(
            dimension_semantics=("arbitrary",), vmem_limit_bytes=_vmem_limit(blocks)),
        name="row_norm",
    )(*args)


def _load_weight(w_ref, copy_ref):
    w = w_ref[...]
    if copy_ref is not None:
        w = w.astype(BF16)
        copy_ref[...] = w
    return w


class _SideCast:
    def __init__(self, src3, layer, block, n_i, n_j):
        _, r, c = src3.shape
        br, tn = block
        ncb = c // tn
        n_blocks = (r // br) * ncb
        assert n_blocks <= n_i * n_j and r % br == 0 and c % tn == 0

        def blk(i, j):
            t = jnp.minimum(i * n_j + j, n_blocks - 1)
            return t // ncb, t % ncb

        self.arg = src3
        self.in_spec = pl.BlockSpec((None, br, tn), lambda i, j: (layer, *blk(i, j)))
        self.out_spec = pl.BlockSpec((None, br, tn), lambda i, j: (blk(i, j)[1], blk(i, j)[0], 0))
        self.out_shape = jax.ShapeDtypeStruct((ncb, r, tn), BF16)
        self.block_bytes = _nbytes(block, F32) + _nbytes(block, BF16)


def _ret_proj_kernel(h_ref, w_ref, cos_ref, sin_ref, *rest, tn, d_qk, head_dk, k_scale, cast_w, has_prev, n_side):
    n_in = (1 if has_prev else 0) + n_side
    o_ref = rest[n_in]
    for s in range(n_side):
        rest[len(rest) - n_side + s][...] = rest[n_in - n_side + s][...].astype(BF16)
    w = _load_weight(w_ref, rest[n_in + 1] if cast_w else None)
    acc = jnp.dot(h_ref[...], w, preferred_element_type=F32)
    col0 = pl.program_id(1) * tn
    is_qk = col0 < 2 * d_qk
    scale = jnp.where(jnp.logical_and(is_qk, col0 >= d_qk), k_scale, 1.0).astype(F32)
    cs = jnp.where(is_qk, cos_ref[...], 1.0) * scale
    sn = jnp.where(is_qk, sin_ref[...], 0.0) * scale
    half = head_dk // 2
    for hh in range(tn // head_dk):
        x1 = acc[:, hh * head_dk:hh * head_dk + half]
        x2 = acc[:, hh * head_dk + half:(hh + 1) * head_dk]
        o_ref[:, hh * head_dk:hh * head_dk + half] = (x1 * cs - x2 * sn).astype(o_ref.dtype)
        o_ref[:, hh * head_dk + half:(hh + 1) * head_dk] = (x1 * sn + x2 * cs).astype(o_ref.dtype)


def _ret_proj_call(h, w3, w_layer, lay, sides, rope_tables, tm=1024, tn=1024, tm_first=2048, tn_first=512):
    n, d = h.shape
    n_cols = w3.shape[2]
    d_qk = RET_HEADS * lay["dk"]
    half = lay["dk"] // 2

    per_tile = tn // tn_first

    def run(tm_, i0, n_i, tn_, weights, cast_w, prev, side_jobs):
        w_spec = (pl.BlockSpec((None, d, tn_), lambda i, j: (w_layer, 0, j)) if cast_w
                  else pl.BlockSpec((None, d, tn_), lambda i, j: (j, 0, 0)))
        in_specs = [pl.BlockSpec((tm_, d), lambda i, j: (i + i0, 0)), w_spec,
                    pl.BlockSpec((tm_, half), lambda i, j: (i + i0, 0)),
                    pl.BlockSpec((tm_, half), lambda i, j: (i + i0, 0))]
        args = [h, weights, *rope_tables]
        aliases = {}
        if prev is not None:
            in_specs.append(pl.BlockSpec(memory_space=pl.ANY))
            args.append(prev)
            aliases = {len(args) - 1: 0}
        out_specs = [pl.BlockSpec((tm_, tn_), lambda i, j: (i + i0, j))]
        out_shape = [jax.ShapeDtypeStruct((n, n_cols), BF16)]
        w_bytes = _nbytes((d, tn_), BF16)
        if cast_w:
            out_specs.append(pl.BlockSpec((None, d, tn_), lambda i, j: (j // per_tile, 0, j % per_tile)))
            out_shape.append(jax.ShapeDtypeStruct((n_cols // tn, d, tn), BF16))
            w_bytes = _nbytes((d, tn_), F32) + 2 * _nbytes((d, tn_), BF16)
        casts = [_SideCast(*job, n_i, n_cols // tn_) for job in side_jobs]
        in_specs += [sc.in_spec for sc in casts]
        args += [sc.arg for sc in casts]
        out_specs += [sc.out_spec for sc in casts]
        out_shape += [sc.out_shape for sc in casts]
        w_bytes += sum(sc.block_bytes for sc in casts)
        blocks = _nbytes((tm_, d), BF16) + w_bytes + 2 * _nbytes((tm_, half), F32) + _nbytes((tm_, tn_), BF16)
        return pl.pallas_call(
            functools.partial(_ret_proj_kernel, tn=tn_, d_qk=d_qk, head_dk=lay["dk"],
                              k_scale=float(lay["dk"]) ** -0.5, cast_w=cast_w,
                              has_prev=prev is not None, n_side=len(casts)),
            grid=(n_i, n_cols // tn_),
            in_specs=in_specs,
            out_specs=out_specs,
            out_shape=out_shape,
            input_output_aliases=aliases,
            compiler_params=pltpu.CompilerParams(
                dimension_semantics=("arbitrary", "arbitrary"),
                vmem_limit_bytes=_vmem_limit(blocks, _nbytes((tm_, tn_), F32))),
            name="ret_in_proj" + ("_first" if cast_w else ""),
        )(*args)

    assert tm_first % tm == 0 and (n - tm_first) % tm == 0
    out0, w_bf = run(tm_first, 0, 1, tn_first, w3, True, None, [])
    out, *side_w = run(tm, tm_first // tm, (n - tm_first) // tm, tn, w_bf, False, out0, sides)
    return out, side_w


def _ffn_in_kernel(x_ref, gam_ref, sc_ref, sh_ref, wg_ref, wu_ref, *rest, tm, sub, has_side):
    if has_side:
        side_in, o_ref, side_out, h_ref = rest
        side_out[...] = side_in[...].astype(BF16)
    else:
        o_ref, h_ref = rest

    @pl.when(pl.program_id(1) == 0)
    def _():
        _norm_mod_rows(x_ref, gam_ref, sc_ref, sh_ref, h_ref, tm=tm, sub=sub)

    h = h_ref[...]
    gate = jnp.dot(h, wg_ref[...], preferred_element_type=F32)
    up = jnp.dot(h, wu_ref[...], preferred_element_type=F32)
    o_ref[...] = (_silu(gate) * up).astype(o_ref.dtype)


def _ffn_in_call(x, gam, mod, layer, w, lay, side=None, tm=1024):
    n, d = x.shape
    sub = lay["sub"]
    tn = w.shape[2]
    n_j = w.shape[0] // 2
    d_ff = n_j * tn
    _, midx = _tiling(lay, tm)
    n_i = n // tm

    def x_tile(i, j):
        return jnp.where(j == 0, i, jnp.minimum(i + 1, n_i - 1))

    in_specs = [
        pl.BlockSpec((tm, d), lambda i, j: (x_tile(i, j), 0)),
        pl.BlockSpec((1, d), lambda i, j: (0, 0)),
        pl.BlockSpec((None, sub, d), lambda i, j: (layer, midx(x_tile(i, j)), 4)),
        pl.BlockSpec((None, sub, d), lambda i, j: (layer, midx(x_tile(i, j)), 3)),
        pl.BlockSpec((None, d, tn), lambda i, j: (j, 0, 0)),
        pl.BlockSpec((None, d, tn), lambda i, j: (j + n_j, 0, 0)),
    ]
    args = [x, gam.reshape(1, d), mod, mod, w, w]
    out_specs = [pl.BlockSpec((tm, tn), lambda i, j: (i, j))]
    out_shape = [jax.ShapeDtypeStruct((n, d_ff), BF16)]
    blocks = (_nbytes((tm, d), F32) + 2 * _nbytes((sub, d), F32) + 2 * _nbytes((d, tn), BF16)
              + _nbytes((tm, tn), BF16))
    if side is not None:
        sc = _SideCast(*side, n // tm, n_j)
        in_specs.append(sc.in_spec)
        args.append(sc.arg)
        out_specs.append(sc.out_spec)
        out_shape.append(sc.out_shape)
        blocks += sc.block_bytes
    res = pl.pallas_call(
        functools.partial(_ffn_in_kernel, tm=tm, sub=sub, has_side=side is not None),
        grid=(n // tm, n_j),
        in_specs=in_specs,
        out_specs=out_specs,
        out_shape=out_shape,
        scratch_shapes=[pltpu.VMEM((tm, d), BF16)],
        compiler_params=pltpu.CompilerParams(
            dimension_semantics=("arbitrary", "arbitrary"),
            vmem_limit_bytes=_vmem_limit(blocks, _nbytes((tm, d), BF16) + 2 * _nbytes((tm, tn), F32))),
        name="ffn_in_swiglu",
    )(*args)
    return res[0], (res[1] if side is not None else None)


def _out_proj_kernel(*refs, tm, sub, npt, n_src, has_side, emit_ssq):
    a_refs, w_ref = refs[:n_src], refs[n_src]
    x_refs, gate_ref = refs[n_src + 1:2 * n_src + 1], refs[2 * n_src + 1]
    rest = list(refs[2 * n_src + 2:])
    ssq_ref = rest.pop() if emit_ssq else None
    if has_side:
        side_in, o_ref, side_out = rest
        side_out[...] = side_in[...].astype(BF16)
    else:
        (o_ref,) = rest

    def body(a_ref, x_ref):
        acc = jnp.dot(a_ref[...], w_ref[...], preferred_element_type=F32)
        gate = gate_ref[...]
        for s in range(tm // sub):
            rows = slice(s * sub, (s + 1) * sub)
            x_new = x_ref[rows, :] + gate * acc[rows, :]
            o_ref[rows, :] = x_new
            if emit_ssq:
                ssq_ref[rows, :] = jnp.sum(x_new * x_new, axis=-1, keepdims=True)

    if n_src == 1:
        body(a_refs[0], x_refs[0])
    else:
        is_prompt = pl.program_id(0) < npt
        pl.when(is_prompt)(functools.partial(body, a_refs[0], x_refs[0]))
        pl.when(jnp.logical_not(is_prompt))(functools.partial(body, a_refs[1], x_refs[1]))


def _out_proj_call(srcs, w, mod, layer, gate_col, lay, side=None, emit_ssq=False, tm=1024):
    n = sum(a.shape[0] for a, _ in srcs)
    k = srcs[0][0].shape[1]
    n_j, _, tn = w.shape
    d = n_j * tn
    sub = lay["sub"]
    npt, midx = _tiling(lay, tm)
    if len(srcs) == 1:
        a_specs = [pl.BlockSpec((tm, k), lambda i, j: (i, 0))]
        x_specs = [pl.BlockSpec((tm, tn), lambda i, j: (i, j))]
        a_bytes = 2 * _nbytes((tm, k), BF16)
    else:
        a_specs = [pl.BlockSpec((tm, k), lambda i, j: (jnp.minimum(i, npt - 1), 0)),
                   pl.BlockSpec((tm, k), lambda i, j: (jnp.maximum(i - npt, 0), 0), pipeline_mode=pl.Buffered(1))]
        x_specs = [pl.BlockSpec((tm, tn), lambda i, j: (jnp.minimum(i, npt - 1), j)),
                   pl.BlockSpec((tm, tn), lambda i, j: (jnp.maximum(i - npt, 0), j))]
        a_bytes = 3 * _nbytes((tm, k), BF16)
    in_specs = (a_specs + [pl.BlockSpec((None, k, tn), lambda i, j: (j, 0, 0))] + x_specs
                + [pl.BlockSpec((None, sub, tn), lambda i, j: (layer, midx(i), gate_col * n_j + j))])
    args = [a for a, _ in srcs] + [w] + [x for _, x in srcs] + [mod]
    out_specs = [pl.BlockSpec((tm, tn), lambda i, j: (i, j))]
    out_shape = [jax.ShapeDtypeStruct((n, d), F32)]
    blocks = (a_bytes // 2 + _nbytes((k, tn), BF16) + (1 + len(srcs)) * _nbytes((tm, tn), F32)
              + _nbytes((sub, tn), F32))
    if side is not None:
        sc = _SideCast(*side, n // tm, n_j)
        in_specs.append(sc.in_spec)
        args.append(sc.arg)
        out_specs.append(sc.out_spec)
        out_shape.append(sc.out_shape)
        blocks += sc.block_bytes
    if emit_ssq:
        out_specs.append(pl.BlockSpec((None, tm, 1), lambda i, j: (j, i, 0)))
        out_shape.append(jax.ShapeDtypeStruct((n_j, n, 1), F32))
        blocks += _nbytes((tm, V7X_LANES), F32)
    res = pl.pallas_call(
        functools.partial(_out_proj_kernel, tm=tm, sub=sub, npt=npt, n_src=len(srcs), has_side=side is not None,
                          emit_ssq=emit_ssq),
        grid=(n // tm, n_j),
        in_specs=in_specs,
        out_specs=out_specs,
        out_shape=out_shape,
        compiler_params=pltpu.CompilerParams(
            dimension_semantics=("arbitrary", "arbitrary"),
            vmem_limit_bytes=_vmem_limit(blocks, _nbytes((tm, tn), F32))),
        name="out_proj_residual",
    )(*args)
    return res[0], (res[1] if side is not None else None), (res[-1] if emit_ssq else None)


def _retention_decays(c):
    lg = jnp.log1p(-jnp.exp2(-5.0 - jnp.arange(RET_HEADS, dtype=F32)))
    idx = jnp.arange(c, dtype=F32)
    diff = idx[:, None] - idx[None, :]
    causal = diff >= 0
    dmask = jnp.where(causal[None], jnp.exp(jnp.where(causal, diff, 0.0)[None] * lg[:, None, None]), 0.0)
    qdec = jnp.exp((idx + 1.0)[None, :] * lg[:, None])
    kdec = jnp.exp((c - 1.0 - idx)[None, :] * lg[:, None])
    sdec = jnp.exp(c * lg)
    return dmask, qdec[:, :, None], kdec[:, :, None], sdec[:, None, None]


def _group_norm_gate(o, g, gng):
    mu = jnp.mean(o, axis=-1, keepdims=True)
    dlt = o - mu
    var = jnp.mean(dlt * dlt, axis=-1, keepdims=True)
    return _silu(g) * (dlt * lax.rsqrt(var + EPS) * gng)


_NT = (((1,), (1,)), ((), ()))
_TN = (((0,), (0,)), ((), ()))


def _retention_chunk(q, k, v, s_prev, dmask, qdec, kdec, sdec):
    scores = lax.dot_general(q, k, _NT, preferred_element_type=F32) * dmask
    o = (jnp.dot(scores.astype(BF16), v, preferred_element_type=F32)
         + qdec * jnp.dot(q, s_prev.astype(BF16), preferred_element_type=F32))
    kd = (k.astype(F32) * kdec).astype(BF16)
    s_new = sdec * s_prev + lax.dot_general(kd, v, _TN, preferred_element_type=F32)
    return o, s_new


def _retention_kernel(pq_ref, pk_ref, pv_ref, pg_ref, sq_ref, sk_ref, sv_ref, sg_ref, s0_ref, gng_ref,
                      pdmask_ref, pqdec_ref, pkdec_ref, psdec_ref, sdmask_ref, sqdec_ref, skdec_ref, ssdec_ref,
                      po_ref, ps_out_ref, so_ref, ss_out_ref,
                      ps_ref, qf_ref, kf_ref, vf_ref, of_ref, *, chunk, n_chunks, bb, n_heads):
    step = pl.program_id(0)
    hp = (step // 2) % n_heads
    hs = step % n_heads
    @pl.when(pl.program_id(0) % 2 == 0)
    def _():
        ps_ref[...] = jnp.zeros_like(ps_ref)

    pdmask, pqdec, pkdec, psdec, pgng = pdmask_ref[hp], pqdec_ref[hp], pkdec_ref[hp], psdec_ref[hp], gng_ref[hp]

    def prompt_body(c, carry):
        rows = pl.ds(pl.multiple_of(c * chunk, chunk), chunk)
        o, s_new = _retention_chunk(pq_ref[rows, :], pk_ref[rows, :], pv_ref[rows, :], ps_ref[...],
                                    pdmask, pqdec, pkdec, psdec)
        ps_ref[...] = s_new
        po_ref[rows, :] = _group_norm_gate(o, pg_ref[rows, :].astype(F32), pgng).astype(po_ref.dtype)
        return carry

    lax.fori_loop(0, n_chunks, prompt_body, 0, unroll=2)
    ps_out_ref[0, 0] = ps_ref[...]

    smask, sqdec, skdec, ssdec, sgng = sdmask_ref[hs], sqdec_ref[hs], skdec_ref[hs], ssdec_ref[hs], gng_ref[hs]
    ts, _, dk = sq_ref.shape
    dv = sv_ref.shape[2]
    rows = ts * bb
    scores = lax.dot_general(sq_ref[...].reshape(rows, dk), sk_ref[...].reshape(rows, dk), _NT,
                             preferred_element_type=F32) * smask
    o_intra = jnp.dot(scores.astype(BF16), sv_ref[...].reshape(rows, dv), preferred_element_type=F32)
    qf_ref[...] = sq_ref[...].astype(F32)
    kf_ref[...] = sk_ref[...].astype(F32)
    vf_ref[...] = sv_ref[...].astype(F32)

    def sample_body(b, carry):
        s0 = s0_ref[b, 0]
        of_ref[:, b, :] = jnp.dot(qf_ref[:, b, :].astype(BF16), s0.astype(BF16), preferred_element_type=F32)
        kd = (kf_ref[:, b, :] * skdec).astype(BF16)
        ss_out_ref[b, 0] = ssdec * s0 + lax.dot_general(kd, vf_ref[:, b, :].astype(BF16), _TN,
                                                        preferred_element_type=F32)
        return carry

    lax.fori_loop(0, bb, sample_body, 0, unroll=4)
    o = o_intra + sqdec * of_ref[...].reshape(rows, dv)
    gated = _group_norm_gate(o, sg_ref[...].reshape(rows, dv).astype(F32), sgng)
    so_ref[...] = gated.astype(so_ref.dtype).reshape(ts, bb, dv)


def _retention_call(qkvg, gng, state, lay, chunk=2 * RET_CHUNK, bb=16):
    n = qkvg.shape[0]
    bp, tp, bs, ts, dk, dv = lay["n_prompt"], lay["t_prompt"], lay["sub"], lay["t_sample"], lay["dk"], lay["dv"]
    h = RET_HEADS
    n_p = lay["n_prompt_rows"]
    half = tp // 2
    n_steps = 2 * bp * h
    assert (bs // bb) * h == n_steps and half % chunk == 0
    t_blk = n_p // (bs * ts)
    assert t_blk * bs * ts == n_p
    v_blk0 = 2 * h * dk // dv
    g_blk0 = v_blk0 + h
    pdec = _retention_decays(chunk)
    dmask_s, qdec_s, kdec_s, sdec_s = _retention_decays(ts)
    same_seq = jnp.eye(bb, dtype=F32)[None, None, :, None, :]
    block_mask = (dmask_s[:, :, None, :, None] * same_seq).reshape(h, ts * bb, ts * bb)
    qdec_rows = jnp.broadcast_to(qdec_s[:, :, None, :], (h, ts, bb, 1)).reshape(h, ts * bb, 1)
    sdec = (block_mask, qdec_rows, kdec_s, sdec_s)
    qkv3 = qkvg.reshape(n // bs, bs, qkvg.shape[1])
    gng3 = gng.reshape(h, 1, dv)

    p_row = lambda s: (s // 2 // h) * 2 + s % 2
    p_head = lambda s: (s // 2) % h
    s_blk = lambda s: s // h
    s_head = lambda s: s % h
    consts = [gng3, *pdec, *sdec]
    resident = lambda a: pl.BlockSpec(a.shape, lambda s: (0,) * a.ndim, pipeline_mode=pl.Buffered(1))
    const_bytes = sum(_nbytes(a.shape[:-1] + (max(a.shape[-1], V7X_LANES),), F32) for a in consts)
    in_specs = [
        pl.BlockSpec((half, dk), lambda s: (p_row(s), p_head(s))),
        pl.BlockSpec((half, dk), lambda s: (p_row(s), h + p_head(s))),
        pl.BlockSpec((half, dv), lambda s: (p_row(s), v_blk0 + p_head(s))),
        pl.BlockSpec((half, dv), lambda s: (p_row(s), g_blk0 + p_head(s))),
        pl.BlockSpec((ts, bb, dk), lambda s: (t_blk, s_blk(s), s_head(s))),
        pl.BlockSpec((ts, bb, dk), lambda s: (t_blk, s_blk(s), h + s_head(s))),
        pl.BlockSpec((ts, bb, dv), lambda s: (t_blk, s_blk(s), v_blk0 + s_head(s))),
        pl.BlockSpec((ts, bb, dv), lambda s: (t_blk, s_blk(s), g_blk0 + s_head(s))),
        pl.BlockSpec((bb, 1, dk, dv), lambda s: (s_blk(s), s_head(s), 0, 0)),
        *[resident(a) for a in consts],
    ]
    out_specs = [
        pl.BlockSpec((half, dv), lambda s: (p_row(s), p_head(s))),
        pl.BlockSpec((1, 1, dk, dv), lambda s: (s // 2 // h, p_head(s), 0, 0)),
        pl.BlockSpec((ts, bb, dv), lambda s: (0, s_blk(s), s_head(s))),
        pl.BlockSpec((bb, 1, dk, dv), lambda s: (s_blk(s), s_head(s), 0, 0)),
    ]
    out_shape = [jax.ShapeDtypeStruct((n_p, h * dv), BF16),
                 jax.ShapeDtypeStruct((bp, h, dk, dv), F32),
                 jax.ShapeDtypeStruct((ts, bs, h * dv), BF16),
                 jax.ShapeDtypeStruct((bs, h, dk, dv), F32)]
    blocks = (2 * _nbytes((half, dk), BF16) + 3 * _nbytes((half, dv), BF16) + _nbytes((dk, dv), F32)
              + _nbytes((chunk, chunk), F32)
              + 2 * _nbytes((ts, bb, dk), BF16) + 3 * _nbytes((ts, bb, dv), BF16) + 2 * _nbytes((bb, dk, dv), F32))
    scratch = _nbytes((dk, dv), F32) + 2 * _nbytes((ts, bb, dk), F32) + 2 * _nbytes((ts, bb, dv), F32)
    gated_p, s_p, gated_s, s_s = pl.pallas_call(
        functools.partial(_retention_kernel, chunk=chunk, n_chunks=half // chunk, bb=bb, n_heads=h),
        grid=(n_steps,),
        in_specs=in_specs,
        out_specs=out_specs,
        out_shape=out_shape,
        scratch_shapes=[pltpu.VMEM((dk, dv), F32),
                        pltpu.VMEM((ts, bb, dk), F32), pltpu.VMEM((ts, bb, dk), F32),
                        pltpu.VMEM((ts, bb, dv), F32), pltpu.VMEM((ts, bb, dv), F32)],
        compiler_params=pltpu.CompilerParams(
            dimension_semantics=("arbitrary",),
            vmem_limit_bytes=_vmem_limit(blocks, scratch + const_bytes)),
        name="retention",
    )(qkvg, qkvg, qkvg, qkvg, qkv3, qkv3, qkv3, qkv3, state, *consts)
    return gated_p, gated_s.reshape(bs * ts, h * dv), s_p, s_s


def _window_sum_rows(win_ref, halo, r0, rows, win, stride):
    if stride % V7X_SUBLANES == 0:
        acc = win_ref[pl.ds(halo + r0, rows), :]
        for k in range(1, win):
            acc = acc + win_ref[pl.ds(halo - k * stride + r0, rows), :]
        return acc
    assert stride == 1 and halo == 16 and win <= 16
    blk = win_ref[pl.ds(r0, rows + 16), :]
    shift = 1
    while shift < win:
        blk = blk + pltpu.roll(blk, shift, 0)
        shift *= 2
    return blk[16:, :]


def _pool_kernel(ssq_ref, gam_ref, sc_ref, sh_ref, hist_ref, x_ref, gate_ref, w_ref, scale_ref, *rest,
                 tm, sub, stride, tiles_per_seq, hist_len, chunk_rows, d_model):
    o_ref, hist_out_ref, win_ref, m_ref = rest[-4:]
    g = pl.program_id(0)
    i = pl.program_id(1)
    halo = 16 * stride
    tile_in_seq = i % tiles_per_seq

    if tiles_per_seq > 1:
        @pl.when(tile_in_seq != 0)
        def _():
            win_ref[0:halo, :] = win_ref[tm:tm + halo, :]

    @pl.when(tile_in_seq == 0)
    def _():
        win_ref[0:halo, :] = hist_ref[0]

    gam = gam_ref[...]
    for s in range(tm // sub):
        rows = slice(s * sub, (s + 1) * sub)
        ssq = ssq_ref[0, rows, :]
        for c in range(1, ssq_ref.shape[0]):
            ssq = ssq + ssq_ref[c, rows, :]
        rs = lax.rsqrt(ssq / d_model + EPS)
        win_ref[halo + s * sub:halo + (s + 1) * sub, :] = (
            ((x_ref[rows, :] * rs) * gam) * (1.0 + sc_ref[...]) + sh_ref[...])

    for gi, win in enumerate(POOL_WINDOWS):
        @pl.when(g == gi)
        def _(win=win):
            def body(c, carry):
                r0 = pl.multiple_of(c * chunk_rows, chunk_rows)
                cur = win_ref[pl.ds(halo + r0, chunk_rows), :]
                acc = _window_sum_rows(win_ref, halo, r0, chunk_rows, win, stride)
                t = (tile_in_seq * tm + r0 + lax.broadcasted_iota(jnp.int32, (chunk_rows, 1), 0)) // stride
                cnt = jnp.minimum(win, t + 1 + hist_len).astype(F32)
                m_ref[pl.ds(r0, chunk_rows), :] = (acc / cnt - cur).astype(m_ref.dtype)
                return carry

            lax.fori_loop(0, tm // chunk_rows, body, 0)

    y = jnp.dot(m_ref[...], w_ref[0], preferred_element_type=F32) * scale_ref[...]
    gate = gate_ref[...]
    for s in range(tm // sub):
        rows = slice(s * sub, (s + 1) * sub)
        o_ref[rows, :] = x_ref[rows, :] + gate * y[rows, :]
    hist_out_ref[0] = win_ref[tm + stride:tm + halo, :]


def _pool_call(ssq, gam, hist, x, mod, layer, w, scale, x_prev, *, row0, n_rows, n_seq, stride, hist_len, mod_seq0,
               lay, tm=1024):
    n, d = x.shape
    sub = lay["sub"]
    n_g = len(POOL_WINDOWS)
    gc = d // n_g
    n_tiles = n_rows // tm
    tps = n_tiles // n_seq
    blk0 = row0 // tm
    halo = 16 * stride
    n_parts = ssq.shape[0]
    blocks = (2 * _nbytes((tm, gc), F32) + _nbytes((halo, gc), F32) + 3 * _nbytes((sub, gc), F32)
              + _nbytes((gc, gc), BF16) + _nbytes((15 * stride, gc), F32) + n_parts * _nbytes((tm, V7X_LANES), F32))
    scratch = _nbytes((halo + tm, gc), F32) + _nbytes((tm, gc), BF16)
    sh_col0, sc_col0, gate_col0 = 0, n_g, 2 * n_g
    mod_blk = lambda col0: pl.BlockSpec((None, sub, gc), lambda g, i: (layer, mod_seq0 + i // tps, col0 + g))
    in_specs = [
        pl.BlockSpec((n_parts, tm, 1), lambda g, i: (0, blk0 + i, 0)),
        pl.BlockSpec((1, gc), lambda g, i: (0, g)),
        mod_blk(sc_col0),
        mod_blk(sh_col0),
        pl.BlockSpec((1, halo, gc), lambda g, i: (i // tps, 0, g)),
        pl.BlockSpec((tm, gc), lambda g, i: (blk0 + i, g)),
        mod_blk(gate_col0),
        pl.BlockSpec((1, gc, gc), lambda g, i: (g, 0, 0)),
        pl.BlockSpec((1, gc), lambda g, i: (0, g)),
    ]
    args = [ssq, gam.reshape(1, d), mod, mod, hist, x, mod, w, scale.reshape(1, d)]
    aliases = {}
    if x_prev is not None:
        in_specs.append(pl.BlockSpec(memory_space=pl.ANY))
        args.append(x_prev)
        aliases = {len(args) - 1: 0}
    return pl.pallas_call(
        functools.partial(_pool_kernel, tm=tm, sub=sub, stride=stride, tiles_per_seq=tps,
                          hist_len=hist_len, chunk_rows=64, d_model=d),
        grid=(n_g, n_tiles),
        in_specs=in_specs,
        out_specs=[
            pl.BlockSpec((tm, gc), lambda g, i: (blk0 + i, g)),
            pl.BlockSpec((1, 15 * stride, gc), lambda g, i: (i // tps, 0, g)),
        ],
        out_shape=[jax.ShapeDtypeStruct((n, d), F32),
                   jax.ShapeDtypeStruct((n_seq, 15 * stride, d), F32)],
        scratch_shapes=[pltpu.VMEM((halo + tm, gc), F32), pltpu.VMEM((tm, gc), BF16)],
        input_output_aliases=aliases,
        compiler_params=pltpu.CompilerParams(
            dimension_semantics=("arbitrary", "arbitrary"),
            vmem_limit_bytes=_vmem_limit(blocks, scratch)),
        name="pool_mixer_s%d" % stride,
    )(*args)


def _rope_tables(pos, dk):
    half = dk // 2
    inv = 1.0 / (ROPE_BASE ** (jnp.arange(half, dtype=F32) / half))
    ang = pos[:, None] * inv[None, :]
    return jnp.cos(ang), jnp.sin(ang)


def kernel(x_prompt, x_sample, c_prompt, c_sample, state_ret, state_pool, norm_mix_g, norm_ffn_g, ada_w, ada_b,
           ret_w_in, ret_gn_g, ret_w_out, pool_w, pool_scale, ffn_w_in, ffn_w_out, final_norm_g):
    bp, tp, d = x_prompt.shape
    bs, ts, _ = x_sample.shape
    depth = ada_w.shape[0]
    dk = state_ret.shape[3]
    dv = state_ret.shape[4]
    n_hist = state_pool.shape[2]
    assert bp <= C_PAD and n_hist == max(POOL_WINDOWS) - 1 and depth == 2
    n_p = bp * tp
    n_s = bs * ts
    lay = dict(n_prompt=bp, t_prompt=tp, n_prompt_rows=n_p, n_sample_rows=n_s, sub=bs, t_sample=ts, dk=dk, dv=dv)

    xp = x_prompt.reshape(n_p, d)
    xs = x_sample.transpose(1, 0, 2).reshape(n_s, d)
    c_all = jnp.concatenate([c_prompt, jnp.zeros((C_PAD - bp, d), F32), c_sample], axis=0)
    mod = _ada_call(c_all, ada_w, ada_b, bp, bs)

    cos_p, sin_p = _rope_tables(jnp.arange(tp, dtype=F32), dk)
    cos_s, sin_s = _rope_tables(float(PAST_LEN) + jnp.arange(ts, dtype=F32), dk)
    cos = jnp.concatenate([jnp.tile(cos_p, (bp, 1)), jnp.repeat(cos_s, bs, axis=0)], axis=0)
    sin = jnp.concatenate([jnp.tile(sin_p, (bp, 1)), jnp.repeat(sin_s, bs, axis=0)], axis=0)

    h0 = _norm_call((xp, xs), norm_mix_g[0], mod, 0, 1, 0, lay, BF16)
    tn_ffn = 512
    d_ff = ffn_w_out.shape[1]
    cast_in = (d // 2, tn_ffn)
    cast_out = (d_ff // 8, tn_ffn)
    qkvg, (w_ret_out, w_ffn_in0) = _ret_proj_call(
        h0, ret_w_in, 0, lay, [(ret_w_out, 0, (ret_w_out.shape[1] // 8, tn_ffn)), (ffn_w_in, 0, cast_in)], (cos, sin))
    gated_p, gated_s, s_ret_p, s_ret_s = _retention_call(qkvg, ret_gn_g[0], state_ret[0], lay)
    x, w_ffn_out0, _ = _out_proj_call([(gated_p, xp), (gated_s, xs)], w_ret_out, mod, 0, 2, lay,
                                      side=(ffn_w_out, 0, cast_out))
    hmid, w_ffn_in1 = _ffn_in_call(x, norm_ffn_g[0], mod, 0, w_ffn_in0, lay, side=(ffn_w_in, 1, cast_in))
    x, w_ffn_out1, ssq = _out_proj_call([(hmid, x)], w_ffn_out0, mod, 0, 5, lay, side=(ffn_w_out, 1, cast_out),
                                        emit_ssq=True)

    pw = pool_w[0].astype(BF16)
    hist_p = jnp.zeros((bp, 16, d), F32)
    hist_s = jnp.concatenate([jnp.zeros((1, bs, d), F32), state_pool[0].transpose(1, 0, 2)], axis=0)
    x1, nh_p = _pool_call(ssq, norm_mix_g[1], hist_p, x, mod, 1, pw, pool_scale[0], None, row0=0, n_rows=n_p,
                          n_seq=bp, stride=1, hist_len=0, mod_seq0=0, lay=lay)
    x, nh_s = _pool_call(ssq, norm_mix_g[1], hist_s.reshape(1, 16 * bs, d), x, mod, 1, pw, pool_scale[0], x1,
                         row0=n_p, n_rows=n_s, n_seq=1, stride=bs, hist_len=min(PAST_LEN, n_hist), mod_seq0=bp,
                         lay=lay)
    hmid, _ = _ffn_in_call(x, norm_ffn_g[1], mod, 1, w_ffn_in1, lay)
    x, _, _ = _out_proj_call([(hmid, x)], w_ffn_out1, mod, 1, 5, lay)

    y_p, y_s = _norm_call((x,), final_norm_g, None, 0, 0, 0, lay, F32, split_out=True)
    y_prompt = y_p.reshape(bp, tp, d)
    y_sample = y_s.reshape(ts, bs, d).transpose(1, 0, 2)
    state_pool_sample = nh_s.reshape(n_hist, bs, d).transpose(1, 0, 2)
    return (y_prompt, y_sample, s_ret_p[None], nh_p[None], s_ret_s[None], state_pool_sample[None])
```
